```python
import math
import jax, jax.numpy as jnp
from jax import lax
import numpy as np

D_MODEL = 2048
BATCH = 1
SEQ = 8192
DEPTH = 2
DEC_BATCH = 32
DEC_SEQ = 16
PAST_LEN = 2048

CHUNK = 64
N_META = 16
D_FF = 5632
SC_WIDTH = 3
D_SC = D_MODEL
SSD_EXPAND = 2
D_INNER = SSD_EXPAND * D_MODEL
SSD_HEADDIM = 64
SSD_HEADS = D_INNER // SSD_HEADDIM
SSD_GROUPS = 8
SSD_HEADS_PER_GROUP = SSD_HEADS // SSD_GROUPS
SSD_STATE = 128
SSD_CONV_WIDTH = 4
SSD_CONV_DIM = D_INNER + 2 * SSD_GROUPS * SSD_STATE
SSD_IN_DIM = D_INNER + SSD_CONV_DIM + SSD_HEADS
SSD_BLOCK = CHUNK
N_SC_LAYERS = (DEPTH + 1) // 2
N_SSD_LAYERS = DEPTH // 2
ALPHA = (2.0 * DEPTH) ** 0.25
BETA = (8.0 * DEPTH) ** -0.25
LN_EPS = 1e-5
RMS_EPS = 1e-5

kernel_name = "hybrid_shortconv_ssd_macaron_stream_step"


def layer_norm(x, g, b):
    xf = x.astype(jnp.float32)
    mu = jnp.mean(xf, axis=-1, keepdims=True)
    var = jnp.mean(jnp.square(xf - mu), axis=-1, keepdims=True)
    return ((xf - mu) * lax.rsqrt(var + LN_EPS)).astype(x.dtype) * g + b


def swiglu(x, w1, w3, w2):
    return (jax.nn.silu(x @ w1) * (x @ w3)) @ w2


def causal_dwconv(u, prev, w):
    k = w.shape[0]
    t = u.shape[1]
    up = jnp.concatenate([prev.astype(u.dtype), u], axis=1)
    out = w[0] * up[:, 0:t]
    for i in range(1, k):
        out = out + w[i] * up[:, i:i + t]
    return out, up[:, up.shape[1] - (k - 1):]


def short_conv_mixer(x, conv_prev, w_in, w_conv, w_out):
    b_gate, c_gate, h = jnp.split(x @ w_in, 3, axis=-1)
    u = c_gate * h
    conv, new_prev = causal_dwconv(u, conv_prev, w_conv)
    return (b_gate * conv) @ w_out, new_prev


def ssd_scan(xh, dt, a, bm, cm, s0):
    f32 = jnp.float32
    b, t = xh.shape[0], xh.shape[1]
    nc, l = t // SSD_BLOCK, SSD_BLOCK
    g, r, p, n = SSD_GROUPS, SSD_HEADS_PER_GROUP, SSD_HEADDIM, SSD_STATE
    x = xh.astype(f32).reshape(b, nc, l, g, r, p)
    dtc = dt.reshape(b, nc, l, g, r)
    bb = bm.astype(f32).reshape(b, nc, l, g, n)
    cc = cm.astype(f32).reshape(b, nc, l, g, n)
    acs = jnp.cumsum(dtc * a.reshape(g, r), axis=2)
    xdt = x * dtc[..., None]
    causal = jnp.tril(jnp.ones((l, l), dtype=bool))[:, :, None, None]
    seg = acs[:, :, :, None] - acs[:, :, None, :]
    decay = jnp.exp(jnp.where(causal, seg, -jnp.inf))
    cb = jnp.einsum('bclgn,bcsgn->bclsg', cc, bb)
    scores = cb[..., None] * decay
    y_intra = jnp.einsum('bclsgr,bcsgrp->bclgrp', scores, xdt)
    a_last = acs[:, :, -1]
    to_end = jnp.exp(a_last[:, :, None] - acs)
    blk_states = jnp.einsum('bclgn,bclgrp->bcgrpn', bb, xdt * to_end[..., None])

    def step(s, inp):
        st, al = inp
        return jnp.exp(al)[..., None, None] * s + st, s

    s_init = s0.astype(f32).reshape(b, g, r, p, n)
    s_final, s_prev = lax.scan(step, s_init,
                               (jnp.moveaxis(blk_states, 1, 0), jnp.moveaxis(a_last, 1, 0)))
    s_prev = jnp.moveaxis(s_prev, 0, 1)
    y_inter = jnp.einsum('bclgn,bcgrpn->bclgrp', cc, s_prev) * jnp.exp(acs)[..., None]
    y = (y_intra + y_inter).reshape(b, t, SSD_HEADS, p)
    return y, s_final.reshape(b, SSD_HEADS, p, n)


def ssd_mixer(x, conv_prev, ssm_prev, lead, w_in, w_conv, b_conv, dt_bias, a_log, d_skip, norm_w, w_out):
    b, t, _ = x.shape
    z, xbc, dt_raw = jnp.split(x @ w_in, [D_INNER, D_INNER + SSD_CONV_DIM], axis=-1)
    xbc_c, new_conv = causal_dwconv(xbc, conv_prev, w_conv)
    xbc_c = jax.nn.silu(xbc_c + b_conv)
    xs, bm, cm = jnp.split(xbc_c, [D_INNER, D_INNER + SSD_GROUPS * SSD_STATE], axis=-1)
    dt = jax.nn.softplus(dt_raw.astype(jnp.float32) + dt_bias.astype(jnp.float32))
    a = -jnp.exp(a_log.astype(jnp.float32))
    tail = (-(lead + t)) % SSD_BLOCK
    pad = ((0, 0), (lead, tail), (0, 0))
    xh = jnp.pad(xs, pad).reshape(b, -1, SSD_HEADS, SSD_HEADDIM)
    dtp = jnp.pad(dt, pad)
    bmp = jnp.pad(bm, pad).reshape(b, -1, SSD_GROUPS, SSD_STATE)
    cmp_ = jnp.pad(cm, pad).reshape(b, -1, SSD_GROUPS, SSD_STATE)
    y, s_new = ssd_scan(xh, dtp, a, bmp, cmp_, ssm_prev)
    y = y[:, lead:lead + t] + d_skip[:, None] * xs.reshape(b, t, SSD_HEADS, SSD_HEADDIM)
    y = y.reshape(b, t, D_INNER).astype(x.dtype) * jax.nn.silu(z)
    yg = y.reshape(b, t, SSD_GROUPS, -1).astype(jnp.float32)
    yg = yg * lax.rsqrt(jnp.mean(jnp.square(yg), axis=-1, keepdims=True) + RMS_EPS)
    y = yg.reshape(b, t, D_INNER).astype(x.dtype) * norm_w
    return y @ w_out, new_conv, s_new.astype(ssm_prev.dtype)


def run_trunk(h, sc_prev, ssd_conv_prev, ssd_prev, lead,
              ln_g, ln_b, ffn_w1, ffn_w3, ffn_w2,
              sc_w_in, sc_w_conv, sc_w_out,
              ssd_w_in, ssd_w_conv, ssd_b_conv, ssd_dt_bias, ssd_a_log, ssd_d, ssd_norm_w, ssd_w_out):
    new_sc, new_ssd_conv, new_ssd = [], [], []
    for i in range(DEPTH):
        j = i // 2
        h = layer_norm(ALPHA * h + 0.5 * swiglu(h, ffn_w1[i, 0], ffn_w3[i, 0], ffn_w2[i, 0]),
                       ln_g[i, 0], ln_b[i, 0])
        if i % 2 == 0:
            m, c = short_conv_mixer(h, sc_prev[j], sc_w_in[j], sc_w_conv[j], sc_w_out[j])
            new_sc.append(c)
        else:
            m, c, s = ssd_mixer(h, ssd_conv_prev[j], ssd_prev[j], lead,
                                ssd_w_in[j], ssd_w_conv[j], ssd_b_conv[j], ssd_dt_bias[j],
                                ssd_a_log[j], ssd_d[j], ssd_norm_w[j], ssd_w_out[j])
            new_ssd_conv.append(c)
            new_ssd.append(s)
        h = layer_norm(ALPHA * h + m, ln_g[i, 1], ln_b[i, 1])
        h = layer_norm(ALPHA * h + 0.5 * swiglu(h, ffn_w1[i, 1], ffn_w3[i, 1], ffn_w2[i, 1]),
                       ln_g[i, 2], ln_b[i, 2])
    return h, jnp.stack(new_sc), jnp.stack(new_ssd_conv), jnp.stack(new_ssd)


def setup_inputs(seed: int = 0) -> dict:
    key = jax.random.key(seed)
    ks = jax.random.split(key, 24)
    f32 = jnp.float32

    def nrm(k, shape, scale):
        return jax.random.normal(k, shape, f32) * scale

    x_prompt = nrm(ks[0], (BATCH, SEQ, D_MODEL), 1.0)
    x_sample = nrm(ks[1], (DEC_BATCH, DEC_SEQ, D_MODEL), 1.0)
    cache_sc_conv = nrm(ks[2], (N_SC_LAYERS, DEC_BATCH, SC_WIDTH - 1, D_SC), 1.0)
    state_ssd_conv = nrm(ks[3], (N_SSD_LAYERS, DEC_BATCH, SSD_CONV_WIDTH - 1, SSD_CONV_DIM), 1.0)
    state_ssd = nrm(ks[4], (N_SSD_LAYERS, DEC_BATCH, SSD_HEADS, SSD_HEADDIM, SSD_STATE), 0.5)
    meta_tokens = nrm(ks[5], (N_META, D_MODEL), 1.0)
    ln_g = 1.0 + nrm(ks[6], (DEPTH, 3, D_MODEL), 0.02)
    ln_b = nrm(ks[7], (DEPTH, 3, D_MODEL), 0.02)
    ffn_w1 = nrm(ks[8], (DEPTH, 2, D_MODEL, D_FF), D_MODEL ** -0.5)
    ffn_w3 = nrm(ks[9], (DEPTH, 2, D_MODEL, D_FF), D_MODEL ** -0.5)
    ffn_w2 = nrm(ks[10], (DEPTH, 2, D_FF, D_MODEL), BETA * D_FF ** -0.5)
    sc_w_in = nrm(ks[11], (N_SC_LAYERS, D_MODEL, 3 * D_SC), D_MODEL ** -0.5)
    sc_w_conv = nrm(ks[12], (N_SC_LAYERS, SC_WIDTH, D_SC), SC_WIDTH ** -0.5)
    sc_w_out = nrm(ks[13], (N_SC_LAYERS, D_SC, D_MODEL), BETA * D_SC ** -0.5)
    ssd_w_in = nrm(ks[14], (N_SSD_LAYERS, D_MODEL, SSD_IN_DIM), D_MODEL ** -0.5)
    ssd_w_conv = nrm(ks[15], (N_SSD_LAYERS, SSD_CONV_WIDTH, SSD_CONV_DIM), SSD_CONV_WIDTH ** -0.5)
    ssd_b_conv = nrm(ks[16], (N_SSD_LAYERS, SSD_CONV_DIM), 0.02)
    dt0 = jnp.exp(jax.random.uniform(ks[17], (N_SSD_LAYERS, SSD_HEADS), f32,
                                     math.log(1e-3), math.log(1e-1)))
    ssd_dt_bias = dt0 + jnp.log(-jnp.expm1(-dt0))
    ssd_a_log = jnp.log(jax.random.uniform(ks[18], (N_SSD_LAYERS, SSD_HEADS), f32, 1.0, 16.0))
    ssd_d = 1.0 + nrm(ks[19], (N_SSD_LAYERS, SSD_HEADS), 0.02)
    ssd_norm_w = 1.0 + nrm(ks[20], (N_SSD_LAYERS, D_INNER), 0.02)
    ssd_w_out = nrm(ks[21], (N_SSD_LAYERS, D_INNER, D_MODEL), BETA * D_INNER ** -0.5)
    return {"x_prompt": x_prompt, "x_sample": x_sample,
            "cache_sc_conv": cache_sc_conv, "state_ssd_conv": state_ssd_conv, "state_ssd": state_ssd,
            "meta_tokens": meta_tokens, "ln_g": ln_g, "ln_b": ln_b,
            "ffn_w1": ffn_w1, "ffn_w3": ffn_w3, "ffn_w2": ffn_w2,
            "sc_w_in": sc_w_in, "sc_w_conv": sc_w_conv, "sc_w_out": sc_w_out,
            "ssd_w_in": ssd_w_in, "ssd_w_conv": ssd_w_conv, "ssd_b_conv": ssd_b_conv,
            "ssd_dt_bias": ssd_dt_bias, "ssd_a_log": ssd_a_log, "ssd_d": ssd_d,
            "ssd_norm_w": ssd_norm_w, "ssd_w_out": ssd_w_out}


def reference(x_prompt, x_sample, cache_sc_conv, state_ssd_conv, state_ssd,
              meta_tokens, ln_g, ln_b, ffn_w1, ffn_w3, ffn_w2,
              sc_w_in, sc_w_conv, sc_w_out,
              ssd_w_in, ssd_w_conv, ssd_b_conv, ssd_dt_bias, ssd_a_log, ssd_d, ssd_norm_w, ssd_w_out):
    b = x_prompt.shape[0]
    meta = jnp.broadcast_to(meta_tokens[None].astype(x_prompt.dtype), (b, N_META, D_MODEL))
    h_p = jnp.concatenate([meta, x_prompt], axis=1)
    zero_sc = jnp.zeros((N_SC_LAYERS, b, SC_WIDTH - 1, D_SC), x_prompt.dtype)
    zero_ssd_conv = jnp.zeros((N_SSD_LAYERS, b, SSD_CONV_WIDTH - 1, SSD_CONV_DIM), x_prompt.dtype)
    zero_ssd = jnp.zeros((N_SSD_LAYERS, b, SSD_HEADS, SSD_HEADDIM, SSD_STATE), state_ssd.dtype)
    h_p, new_sc_conv_p, new_ssd_conv_p, new_ssd_state_p = run_trunk(
        h_p, zero_sc, zero_ssd_conv, zero_ssd, CHUNK - N_META,
        ln_g, ln_b, ffn_w1, ffn_w3, ffn_w2, sc_w_in, sc_w_conv, sc_w_out,
        ssd_w_in, ssd_w_conv, ssd_b_conv, ssd_dt_bias, ssd_a_log, ssd_d, ssd_norm_w, ssd_w_out)
    y_prompt = h_p[:, N_META:]
    y_sample, new_sc_conv_s, new_ssd_conv_s, new_ssd_state_s = run_trunk(
        x_sample, cache_sc_conv, state_ssd_conv, state_ssd, 0,
        ln_g, ln_b, ffn_w1, ffn_w3, ffn_w2, sc_w_in, sc_w_conv, sc_w_out,
        ssd_w_in, ssd_w_conv, ssd_b_conv, ssd_dt_bias, ssd_a_log, ssd_d, ssd_norm_w, ssd_w_out)
    return (y_prompt, y_sample, new_sc_conv_p, new_sc_conv_s, new_ssd_conv_p, new_ssd_conv_s,
            new_ssd_state_p, new_ssd_state_s)
```

```python
import functools
import math

import jax
import jax.numpy as jnp
from jax import lax
from jax.experimental import pallas as pl
from jax.experimental.pallas import tpu as pltpu

F32 = jnp.float32
BF16 = jnp.bfloat16

D_MODEL = 2048
SEQ = 8192
DEPTH = 2
DEC_BATCH = 32
DEC_SEQ = 16
N_META = 16
D_FF = 5632
SC_WIDTH = 3
D_INNER = 4096
HEADDIM = 64
HEADS = 64
GROUPS = 8
HEADS_PER_GROUP = 8
STATE = 128
SSD_CONV_WIDTH = 4
CONV_DIM = D_INNER + 2 * GROUPS * STATE
GROUP_W = HEADS_PER_GROUP * HEADDIM
ALPHA = (2.0 * DEPTH) ** 0.25
LN_EPS = 1e-5
RMS_EPS = 1e-5

CHUNK = 128
T_PROMPT = N_META + SEQ
N_MAIN_CHUNKS = -(-T_PROMPT // CHUNK)
T_MAIN = N_MAIN_CHUNKS * CHUNK
N_PAD = T_MAIN - T_PROMPT
N_SAMP = DEC_BATCH * DEC_SEQ
N_SAMP_CHUNKS = N_SAMP // CHUNK
SEQ_PER_CHUNK = CHUNK // DEC_SEQ
T_ALL = T_MAIN + N_SAMP

TM = 736
TF = 512
ZXD_W = 10368
ZXD_TN = 1152
DT_COL = D_INNER + CONV_DIM

VMEM_LIMIT_BYTES = 56 * 1024 * 1024


def _cparams(sem):
    return pltpu.CompilerParams(dimension_semantics=sem, vmem_limit_bytes=VMEM_LIMIT_BYTES)


def _layer_norm(v, g, b):
    mu = jnp.mean(v, axis=-1, keepdims=True)
    c = v - mu
    var = jnp.mean(c * c, axis=-1, keepdims=True)
    return c * lax.rsqrt(var + LN_EPS) * g + b


def _silu(x):
    return x * jax.nn.sigmoid(x)


def _softplus(x):
    return jnp.maximum(x, 0.0) + jnp.log1p(jnp.exp(-jnp.abs(x)))


def _ffn_kernel(x_ref, w1_ref, w3_ref, w2_ref, g_ref, b_ref, o_ref, xb_ref, *, nf):
    f = pl.program_id(1)

    @pl.when(f == 0)
    def _():
        xb_ref[...] = x_ref[...].astype(BF16)

    xb = xb_ref[...]
    h1 = jnp.dot(xb, w1_ref[...], preferred_element_type=F32)
    h3 = jnp.dot(xb, w3_ref[...], preferred_element_type=F32)
    gate = (_silu(h1) * h3).astype(BF16)
    part = jnp.dot(gate, w2_ref[...], preferred_element_type=F32)

    @pl.when(f == 0)
    def _():
        o_ref[...] = part

    @pl.when(f > 0)
    def _():
        o_ref[...] += part

    @pl.when(f == nf - 1)
    def _():
        v = ALPHA * x_ref[...] + 0.5 * o_ref[...]
        o_ref[...] = _layer_norm(v, g_ref[...], b_ref[...])


def _ffn(h, w1, w3, w2, lng, lnb, i, j, k):
    nf = D_FF // TF
    return pl.pallas_call(
        functools.partial(_ffn_kernel, nf=nf),
        grid=(T_ALL // TM, nf),
        in_specs=[
            pl.BlockSpec((TM, D_MODEL), lambda m, f: (m, 0)),
            pl.BlockSpec((None, None, D_MODEL, TF), lambda m, f: (i, j, 0, f)),
            pl.BlockSpec((None, None, D_MODEL, TF), lambda m, f: (i, j, 0, f)),
            pl.BlockSpec((None, None, TF, D_MODEL), lambda m, f: (i, j, f, 0)),
            pl.BlockSpec((None, None, 1, D_MODEL), lambda m, f: (i, k, 0, 0)),
            pl.BlockSpec((None, None, 1, D_MODEL), lambda m, f: (i, k, 0, 0)),
        ],
        out_specs=pl.BlockSpec((TM, D_MODEL), lambda m, f: (m, 0)),
        out_shape=jax.ShapeDtypeStruct((T_ALL, D_MODEL), F32),
        scratch_shapes=[pltpu.VMEM((TM, D_MODEL), BF16)],
        compiler_params=_cparams(("parallel", "arbitrary")),
        name="ffn_ln",
    )(h, w1, w3, w2, lng, lnb)


def _mm_kernel(x_ref, w_ref, o_ref, xb_ref):
    @pl.when(pl.program_id(1) == 0)
    def _():
        xb_ref[...] = x_ref[...].astype(BF16)

    o_ref[...] = jnp.dot(xb_ref[...], w_ref[...], preferred_element_type=F32)


def _mm(h, w, tn):
    k, n = w.shape
    return pl.pallas_call(
        _mm_kernel,
        grid=(T_ALL // TM, n // tn),
        in_specs=[
            pl.BlockSpec((TM, k), lambda m, j: (m, 0)),
            pl.BlockSpec((k, tn), lambda m, j: (0, j)),
        ],
        out_specs=pl.BlockSpec((TM, tn), lambda m, j: (m, j)),
        out_shape=jax.ShapeDtypeStruct((T_ALL, n), F32),
        scratch_shapes=[pltpu.VMEM((TM, k), BF16)],
        compiler_params=_cparams(("parallel", "arbitrary")),
        name="in_proj",
    )(h, w)


def _sc_in_kernel(x_ref, wb_ref, wc_ref, wh_ref, bg_ref, u_ref, xb_ref):
    @pl.when(pl.program_id(1) == 0)
    def _():
        xb_ref[...] = x_ref[...].astype(BF16)

    xb = xb_ref[...]
    bg_ref[...] = jnp.dot(xb, wb_ref[...], preferred_element_type=F32)
    c = jnp.dot(xb, wc_ref[...], preferred_element_type=F32)
    hh = jnp.dot(xb, wh_ref[...], preferred_element_type=F32)
    u_ref[...] = c * hh


def _sc_in(h, w_in, j):
    tn = 512
    nb = D_MODEL // tn
    row = pl.BlockSpec((TM, tn), lambda m, n: (m, n))
    return pl.pallas_call(
        _sc_in_kernel,
        grid=(T_ALL // TM, nb),
        in_specs=[
            pl.BlockSpec((TM, D_MODEL), lambda m, n: (m, 0)),
            pl.BlockSpec((None, D_MODEL, tn), lambda m, n: (j, 0, n)),
            pl.BlockSpec((None, D_MODEL, tn), lambda m, n: (j, 0, n + nb)),
            pl.BlockSpec((None, D_MODEL, tn), lambda m, n: (j, 0, n + 2 * nb)),
        ],
        out_specs=[row, row],
        out_shape=[jax.ShapeDtypeStruct((T_ALL, D_MODEL), F32)] * 2,
        scratch_shapes=[pltpu.VMEM((TM, D_MODEL), BF16)],
        compiler_params=_cparams(("parallel", "arbitrary")),
        name="sc_in_proj",
    )(h, w_in, w_in, w_in)


CONV_TILE = 128
CONV_CB = 2048
HALO = 8


def _conv_kernel(*refs, taps, has_bias, has_gate, act):
    u_ref, halo_ref = refs[0], refs[1]
    ov_refs = refs[2:2 + taps - 1]
    w_ref = refs[2 + taps - 1]
    pos = 3 + taps - 1
    bias_ref = gate_ref = None
    if has_bias:
        bias_ref = refs[pos]
        pos += 1
    if has_gate:
        gate_ref = refs[pos]
        pos += 1
    o_ref = refs[pos]

    i = pl.program_id(0)
    u = u_ref[...]
    halo = jnp.where(i > 0, halo_ref[...], 0.0)
    lrow = lax.broadcasted_iota(jnp.int32, (CONV_TILE, 1), 0)
    srow = lrow + i * CONV_TILE - T_MAIN
    spos = jnp.bitwise_and(srow, DEC_SEQ - 1)
    in_samp = srow >= 0
    w = w_ref[...]
    acc = w[taps - 1:taps, :] * u
    for d in range(1, taps):
        ud = pltpu.roll(u, d, axis=0)
        for r in range(d):
            ud = jnp.where(lrow == r, halo[HALO - d + r:HALO - d + r + 1, :], ud)
        ud = jnp.where(jnp.logical_and(in_samp, spos < d), ov_refs[d - 1][...], ud)
        acc = acc + w[taps - 1 - d:taps - d, :] * ud
    if has_bias:
        acc = acc + bias_ref[...]
    if act:
        acc = _silu(acc)
    if has_gate:
        acc = acc * gate_ref[...]
    o_ref[...] = acc.astype(o_ref.dtype)


def _conv_overrides(prev, taps):
    c = prev.shape[-1]
    out = []
    for d in range(1, taps):
        o = jnp.zeros((DEC_BATCH, DEC_SEQ, c), F32)
        for p in range(d):
            o = o.at[:, p].set(prev[:, taps - 1 + p - d])
        out.append(o.reshape(N_SAMP, c))
    return out


def _conv(u, col_off, n_ch, prev, w, bias, gate, act, out_dtype):
    taps = w.shape[0]
    ncb = n_ch // CONV_CB
    cb0 = col_off // CONV_CB
    tiles_per_halo = CONV_TILE // HALO
    samp_tile0 = T_MAIN // CONV_TILE
    ovs = _conv_overrides(prev, taps)
    in_specs = [
        pl.BlockSpec((CONV_TILE, CONV_CB), lambda i, c: (i, cb0 + c)),
        pl.BlockSpec((HALO, CONV_CB), lambda i, c: (jnp.maximum(i * tiles_per_halo - 1, 0), cb0 + c)),
    ]
    args = [u, u]
    for o in ovs:
        in_specs.append(pl.BlockSpec((CONV_TILE, CONV_CB), lambda i, c: (jnp.maximum(i - samp_tile0, 0), c)))
        args.append(o)
    in_specs.append(pl.BlockSpec((taps, CONV_CB), lambda i, c: (0, c)))
    args.append(w)
    if bias is not None:
        in_specs.append(pl.BlockSpec((1, CONV_CB), lambda i, c: (0, c)))
        args.append(bias.reshape(1, n_ch))
    if gate is not None:
        in_specs.append(pl.BlockSpec((CONV_TILE, CONV_CB), lambda i, c: (i, c)))
        args.append(gate)
    return pl.pallas_call(
        functools.partial(_conv_kernel, taps=taps, has_bias=bias is not None,
                          has_gate=gate is not None, act=act),
        grid=(T_ALL // CONV_TILE, ncb),
        in_specs=in_specs,
        out_specs=pl.BlockSpec((CONV_TILE, CONV_CB), lambda i, c: (i, c)),
        out_shape=jax.ShapeDtypeStruct((T_ALL, n_ch), out_dtype),
        compiler_params=_cparams(("parallel", "parallel")),
        name="causal_conv",
    )(*args)


def _proj_ln_kernel(y_ref, w_ref, h_ref, g_ref, b_ref, o_ref, *, nk):
    k = pl.program_id(1)
    part = jnp.dot(y_ref[...], w_ref[...], preferred_element_type=F32)

    @pl.when(k == 0)
    def _():
        o_ref[...] = part

    @pl.when(k > 0)
    def _():
        o_ref[...] += part

    @pl.when(k == nk - 1)
    def _():
        v = ALPHA * h_ref[...] + o_ref[...]
        o_ref[...] = _layer_norm(v, g_ref[...], b_ref[...])


def _proj_ln(y, w, j, h, lng, lnb, i, k):
    kdim = w.shape[1]
    tk = 1024
    nk = kdim // tk
    return pl.pallas_call(
        functools.partial(_proj_ln_kernel, nk=nk),
        grid=(T_ALL // TM, nk),
        in_specs=[
            pl.BlockSpec((TM, tk), lambda m, kk: (m, kk)),
            pl.BlockSpec((None, tk, D_MODEL), lambda m, kk: (j, kk, 0)),
            pl.BlockSpec((TM, D_MODEL), lambda m, kk: (m, 0)),
            pl.BlockSpec((None, None, 1, D_MODEL), lambda m, kk: (i, k, 0, 0)),
            pl.BlockSpec((None, None, 1, D_MODEL), lambda m, kk: (i, k, 0, 0)),
        ],
        out_specs=pl.BlockSpec((TM, D_MODEL), lambda m, kk: (m, 0)),
        out_shape=jax.ShapeDtypeStruct((T_ALL, D_MODEL), F32),
        compiler_params=_cparams(("parallel", "arbitrary")),
        name="out_proj_ln",
    )(y, w, h, lng, lnb)


_NT = (((1,), (1,)), ((), ()))
_TN = (((0,), (0,)), ((), ()))
_HI = lax.Precision.HIGHEST


def _ssd_chunk(x_ref, bm_ref, cm_ref, z_ref, dt_ref, dtt_ref, dtb_r_ref, dtb_c_ref, al_r_ref, al_c_ref,
               d_ref, nw_ref, y_ref, ysc_ref, xw_ref, yint, valid_c, valid_r, causal, same):
    dt = jnp.where(valid_c, _softplus(dt_ref[...] + dtb_r_ref[...]), 0.0)
    dtt = jnp.where(valid_r, _softplus(dtt_ref[...] + dtb_c_ref[...]), 0.0)
    da = dt * (-jnp.exp(al_r_ref[...]))
    dat = dtt * (-jnp.exp(al_c_ref[...]))
    cmask = causal.astype(F32)
    acs = jnp.dot(cmask, da, precision=_HI, preferred_element_type=F32)
    tot = jnp.dot(same.astype(F32), da, precision=_HI, preferred_element_type=F32)
    acst = lax.dot_general(dat, cmask, _NT, precision=_HI, preferred_element_type=F32)

    bmb = bm_ref[...].astype(BF16)
    cmb = cm_ref[...].astype(BF16)
    cb = lax.dot_general(cmb, bmb, _NT, preferred_element_type=F32)
    d_row = d_ref[...]
    for r in range(HEADS_PER_GROUP):
        lo, hi = r * HEADDIM, (r + 1) * HEADDIM
        col = acs[:, r:r + 1]
        row = acst[r:r + 1, :]
        decay = jnp.exp(jnp.where(causal, col - row, -jnp.inf))
        scores = (cb * decay).astype(BF16)
        xh = jnp.where(valid_c, x_ref[:, lo:hi], 0.0)
        xdt = xh * dt[:, r:r + 1]
        yh = jnp.dot(scores, xdt.astype(BF16), preferred_element_type=F32)
        yh = yh + yint[:, lo:hi] * jnp.exp(col)
        yh = yh + d_row[:, r:r + 1] * xh
        ysc_ref[:, lo:hi] = yh
        to_end = jnp.exp(tot[:, r:r + 1] - col)
        xw_ref[:, lo:hi] = (xdt * to_end).astype(BF16)
    y = ysc_ref[...] * _silu(z_ref[...])
    ms = jnp.mean(y * y, axis=-1, keepdims=True)
    y_ref[...] = (y * lax.rsqrt(ms + RMS_EPS) * nw_ref[...]).astype(y_ref.dtype)
    return bmb, cmb, tot


def _chunk_masks():
    li = lax.broadcasted_iota(jnp.int32, (CHUNK, CHUNK), 0)
    si = lax.broadcasted_iota(jnp.int32, (CHUNK, CHUNK), 1)
    return li, si


def _ssd_main_kernel(x_ref, bm_ref, cm_ref, z_ref, dt_ref, dtt_ref, dtb_r_ref, dtb_c_ref, al_r_ref, al_c_ref,
                     d_ref, nw_ref, y_ref, sfin_ref, s_ref, ysc_ref, xw_ref):
    c = pl.program_id(1)

    @pl.when(c == 0)
    def _():
        s_ref[...] = jnp.zeros_like(s_ref)

    li, si = _chunk_masks()
    causal = si <= li
    same = li >= 0
    base = c * CHUNK
    valid_c = lax.broadcasted_iota(jnp.int32, (CHUNK, 1), 0) + base < T_PROMPT
    valid_r = lax.broadcasted_iota(jnp.int32, (1, CHUNK), 1) + base < T_PROMPT

    cmb = cm_ref[...].astype(BF16)
    yint = lax.dot_general(cmb, s_ref[...].astype(BF16), _NT, preferred_element_type=F32)
    bmb, _, tot = _ssd_chunk(x_ref, bm_ref, cm_ref, z_ref, dt_ref, dtt_ref, dtb_r_ref, dtb_c_ref,
                             al_r_ref, al_c_ref, d_ref, nw_ref, y_ref, ysc_ref, xw_ref,
                             yint, valid_c, valid_r, causal, same)
    upd = lax.dot_general(xw_ref[...], bmb, _TN, preferred_element_type=F32)
    for r in range(HEADS_PER_GROUP):
        lo, hi = r * HEADDIM, (r + 1) * HEADDIM
        s_ref[lo:hi, :] = s_ref[lo:hi, :] * jnp.exp(tot[0:1, r:r + 1]) + upd[lo:hi, :]

    @pl.when(c == N_MAIN_CHUNKS - 1)
    def _():
        sfin_ref[...] = s_ref[...]


def _ssd_samp_kernel(x_ref, bm_ref, cm_ref, z_ref, dt_ref, dtt_ref, dtb_r_ref, dtb_c_ref, al_r_ref, al_c_ref,
                     d_ref, nw_ref, s0_ref, yprev_ref, y_ref, s1_ref, yint_ref, ysc_ref, xw_ref):
    del yprev_ref
    li, si = _chunk_masks()
    same = (li // DEC_SEQ) == (si // DEC_SEQ)
    causal = jnp.logical_and(si <= li, same)
    valid_c = lax.broadcasted_iota(jnp.int32, (CHUNK, 1), 0) >= 0
    valid_r = lax.broadcasted_iota(jnp.int32, (1, CHUNK), 1) >= 0

    cmb = cm_ref[...].astype(BF16)
    for q in range(SEQ_PER_CHUNK):
        r0, r1 = q * DEC_SEQ, (q + 1) * DEC_SEQ
        yint_ref[r0:r1, :] = lax.dot_general(cmb[r0:r1, :], s0_ref[q].astype(BF16), _NT,
                                             preferred_element_type=F32)
    bmb, _, tot = _ssd_chunk(x_ref, bm_ref, cm_ref, z_ref, dt_ref, dtt_ref, dtb_r_ref, dtb_c_ref,
                             al_r_ref, al_c_ref, d_ref, nw_ref, y_ref, ysc_ref, xw_ref,
                             yint_ref[...], valid_c, valid_r, causal, same)
    for q in range(SEQ_PER_CHUNK):
        r0, r1 = q * DEC_SEQ, (q + 1) * DEC_SEQ
        upd = lax.dot_general(xw_ref[r0:r1, :], bmb[r0:r1, :], _TN, preferred_element_type=F32)
        for r in range(HEADS_PER_GROUP):
            lo, hi = r * HEADDIM, (r + 1) * HEADDIM
            s1_ref[q, lo:hi, :] = s0_ref[q, lo:hi, :] * jnp.exp(tot[r0:r0 + 1, r:r + 1]) + upd[lo:hi, :]


def _ssd_common_specs(c0):
    gw = GROUP_W // STATE
    return [
        pl.BlockSpec((CHUNK, GROUP_W), lambda g, c: (c0 + c, g)),
        pl.BlockSpec((CHUNK, STATE), lambda g, c: (c0 + c, D_INNER // STATE + g)),
        pl.BlockSpec((CHUNK, STATE), lambda g, c: (c0 + c, D_INNER // STATE + GROUPS + g)),
        pl.BlockSpec((CHUNK, GROUP_W), lambda g, c: (c0 + c, g)),
        pl.BlockSpec((None, CHUNK, 128), lambda g, c: (g, c0 + c, 0)),
        pl.BlockSpec((None, HEADS_PER_GROUP, CHUNK), lambda g, c: (g, 0, c0 + c)),
        pl.BlockSpec((None, 1, 128), lambda g, c: (g, 0, 0)),
        pl.BlockSpec((None, HEADS_PER_GROUP, 1), lambda g, c: (g, 0, 0)),
        pl.BlockSpec((None, 1, 128), lambda g, c: (g, 0, 0)),
        pl.BlockSpec((None, HEADS_PER_GROUP, 1), lambda g, c: (g, 0, 0)),
        pl.BlockSpec((None, 1, 128), lambda g, c: (g, 0, 0)),
        pl.BlockSpec((1, GROUP_W), lambda g, c: (0, g)),
    ]


def _ssd(xbc, zxd, dt_g, dt_gt, dtb_r, dtb_c, al_r, al_c, d_r, norm_w, state0):
    common = (xbc, xbc, xbc, zxd, dt_g, dt_gt, dtb_r, dtb_c, al_r, al_c, d_r, norm_w)
    y, s_fin = pl.pallas_call(
        _ssd_main_kernel,
        grid=(GROUPS, N_MAIN_CHUNKS),
        in_specs=_ssd_common_specs(0),
        out_specs=[
            pl.BlockSpec((CHUNK, GROUP_W), lambda g, c: (c, g)),
            pl.BlockSpec((None, GROUP_W, STATE), lambda g, c: (g, 0, 0)),
        ],
        out_shape=[
            jax.ShapeDtypeStruct((T_ALL, D_INNER), BF16),
            jax.ShapeDtypeStruct((GROUPS, GROUP_W, STATE), F32),
        ],
        scratch_shapes=[
            pltpu.VMEM((GROUP_W, STATE), F32),
            pltpu.VMEM((CHUNK, GROUP_W), F32),
            pltpu.VMEM((CHUNK, GROUP_W), BF16),
        ],
        compiler_params=_cparams(("parallel", "arbitrary")),
        name="ssd_prompt",
    )(*common)

    y, s_new = pl.pallas_call(
        _ssd_samp_kernel,
        grid=(GROUPS, N_SAMP_CHUNKS),
        in_specs=_ssd_common_specs(N_MAIN_CHUNKS) + [
            pl.BlockSpec((SEQ_PER_CHUNK, None, GROUP_W, STATE), lambda g, c: (c, g, 0, 0)),
            pl.BlockSpec(memory_space=pl.ANY),
        ],
        out_specs=[
            pl.BlockSpec((CHUNK, GROUP_W), lambda g, c: (N_MAIN_CHUNKS + c, g)),
            pl.BlockSpec((SEQ_PER_CHUNK, None, GROUP_W, STATE), lambda g, c: (c, g, 0, 0)),
        ],
        out_shape=[
            jax.ShapeDtypeStruct((T_ALL, D_INNER), BF16),
            jax.ShapeDtypeStruct((DEC_BATCH, GROUPS, GROUP_W, STATE), F32),
        ],
        scratch_shapes=[
            pltpu.VMEM((CHUNK, GROUP_W), F32),
            pltpu.VMEM((CHUNK, GROUP_W), F32),
            pltpu.VMEM((CHUNK, GROUP_W), BF16),
        ],
        input_output_aliases={13: 0},
        compiler_params=_cparams(("parallel", "arbitrary")),
        name="ssd_sample",
    )(*common, state0, y)
    return y, s_fin, s_new


def _group_rows(v):
    return jnp.pad(v.reshape(GROUPS, 1, HEADS_PER_GROUP).astype(F32), ((0, 0), (0, 0), (0, 128 - HEADS_PER_GROUP)))


def _group_cols(v):
    return v.reshape(GROUPS, HEADS_PER_GROUP, 1).astype(F32)


def _tail_rows(a, n):
    c = a.shape[-1]
    p = a[T_PROMPT - n:T_PROMPT].reshape(1, 1, n, c)
    s = a[T_MAIN:].reshape(DEC_BATCH, DEC_SEQ, c)[:, DEC_SEQ - n:].reshape(1, DEC_BATCH, n, c)
    return p, s


def kernel(x_prompt, x_sample, cache_sc_conv, state_ssd_conv, state_ssd, meta_tokens, ln_g, ln_b,
           ffn_w1, ffn_w3, ffn_w2, sc_w_in, sc_w_conv, sc_w_out,
           ssd_w_in, ssd_w_conv, ssd_b_conv, ssd_dt_bias, ssd_a_log, ssd_d, ssd_norm_w, ssd_w_out):
    w1 = ffn_w1.astype(BF16)
    w3 = ffn_w3.astype(BF16)
    w2 = ffn_w2.astype(BF16)
    sc_in_w = sc_w_in.astype(BF16)
    sc_out_w = sc_w_out.astype(BF16)
    ssd_in_w = jnp.pad(ssd_w_in[0], ((0, 0), (0, ZXD_W - ssd_w_in.shape[-1]))).astype(BF16)
    ssd_out_w = ssd_w_out.astype(BF16)
    lng = ln_g.reshape(DEPTH, 3, 1, D_MODEL)
    lnb = ln_b.reshape(DEPTH, 3, 1, D_MODEL)

    h = jnp.concatenate([meta_tokens.astype(F32), x_prompt[0], jnp.zeros((N_PAD, D_MODEL), F32),
                         x_sample.reshape(N_SAMP, D_MODEL)], axis=0)

    h = _ffn(h, w1, w3, w2, lng, lnb, 0, 0, 0)
    bgate, u = _sc_in(h, sc_in_w, 0)
    v = _conv(u, 0, D_MODEL, cache_sc_conv[0], sc_w_conv[0], None, bgate, False, BF16)
    new_sc_p, new_sc_s = _tail_rows(u, SC_WIDTH - 1)
    h = _proj_ln(v, sc_out_w, 0, h, lng, lnb, 0, 1)
    h = _ffn(h, w1, w3, w2, lng, lnb, 0, 1, 2)

    h = _ffn(h, w1, w3, w2, lng, lnb, 1, 0, 0)
    zxd = _mm(h, ssd_in_w, ZXD_TN)
    xbc = _conv(zxd, D_INNER, CONV_DIM, state_ssd_conv[0], ssd_w_conv[0], ssd_b_conv[0], None, True, F32)
    new_conv_p, new_conv_s = _tail_rows(zxd[:, D_INNER:DT_COL], SSD_CONV_WIDTH - 1)

    dt_raw = zxd[:, DT_COL:DT_COL + HEADS]
    dt_g = jnp.pad(dt_raw.reshape(T_ALL, GROUPS, HEADS_PER_GROUP).transpose(1, 0, 2),
                   ((0, 0), (0, 0), (0, 128 - HEADS_PER_GROUP)))
    dt_gt = dt_raw.T.reshape(GROUPS, HEADS_PER_GROUP, T_ALL)
    state0 = state_ssd[0].reshape(DEC_BATCH, GROUPS, GROUP_W, STATE)
    y, s_fin, s_new = _ssd(xbc, zxd, dt_g, dt_gt,
                           _group_rows(ssd_dt_bias[0]), _group_cols(ssd_dt_bias[0]),
                           _group_rows(ssd_a_log[0]), _group_cols(ssd_a_log[0]),
                           _group_rows(ssd_d[0]), ssd_norm_w.astype(F32), state0)
    h = _proj_ln(y, ssd_out_w, 0, h, lng, lnb, 1, 1)
    h = _ffn(h, w1, w3, w2, lng, lnb, 1, 1, 2)

    y_prompt = h[N_META:T_PROMPT].reshape(1, SEQ, D_MODEL)
    y_sample = h[T_MAIN:].reshape(DEC_BATCH, DEC_SEQ, D_MODEL)
    new_ssd_p = s_fin.reshape(1, 1, HEADS, HEADDIM, STATE).astype(state_ssd.dtype)
    new_ssd_s = s_new.reshape(1, DEC_BATCH, HEADS, HEADDIM, STATE).astype(state_ssd.dtype)
    return (y_prompt, y_sample, new_sc_p, new_sc_s, new_conv_p, new_conv_s, new_ssd_p, new_ssd_s)
```

```python
import functools
import math

import jax
import jax.numpy as jnp
from jax import lax
from jax.experimental import pallas as pl
from jax.experimental.pallas import tpu as pltpu

F32 = jnp.float32
BF16 = jnp.bfloat16

D_MODEL = 2048
SEQ = 8192
DEPTH = 2
DEC_BATCH = 32
DEC_SEQ = 16
N_META = 16
D_FF = 5632
SC_WIDTH = 3
D_INNER = 4096
HEADDIM = 64
HEADS = 64
GROUPS = 8
HEADS_PER_GROUP = 8
STATE = 128
SSD_CONV_WIDTH = 4
CONV_DIM = D_INNER + 2 * GROUPS * STATE
GROUP_W = HEADS_PER_GROUP * HEADDIM
ALPHA = (2.0 * DEPTH) ** 0.25
LN_EPS = 1e-5
RMS_EPS = 1e-5

CHUNK = 128
T_PROMPT = N_META + SEQ
N_MAIN_CHUNKS = -(-T_PROMPT // CHUNK)
T_MAIN = N_MAIN_CHUNKS * CHUNK
N_PAD = T_MAIN - T_PROMPT
N_SAMP = DEC_BATCH * DEC_SEQ
N_SAMP_CHUNKS = N_SAMP // CHUNK
SEQ_PER_CHUNK = CHUNK // DEC_SEQ
T_ALL = T_MAIN + N_SAMP

TM = 736
TF = 512
FFN_ROW_SPLIT = 2
PROJ_ROW_SPLIT = 2
ZXD_W = 10368
ZXD_TN = 1152
DT_COL = D_INNER + CONV_DIM

VMEM_LIMIT_BYTES = 56 * 1024 * 1024


def _cparams(sem):
    return pltpu.CompilerParams(dimension_semantics=sem, vmem_limit_bytes=VMEM_LIMIT_BYTES)


def _layer_norm(v, g, b):
    mu = jnp.mean(v, axis=-1, keepdims=True)
    c = v - mu
    var = jnp.mean(c * c, axis=-1, keepdims=True)
    return c * lax.rsqrt(var + LN_EPS) * g + b


def _silu(x):
    return x * jax.nn.sigmoid(x)


def _softplus(x):
    return jnp.maximum(x, 0.0) + jnp.log1p(jnp.exp(-jnp.abs(x)))


def _ffn_kernel(x_ref, w1_ref, w3_ref, w2_ref, g_ref, b_ref, o_ref, xb_ref, *, nf):
    f = pl.program_id(1)

    @pl.when(f == 0)
    def _():
        xb_ref[...] = x_ref[...].astype(BF16)
        o_ref[...] = jnp.zeros_like(o_ref)

    rs = TM // FFN_ROW_SPLIT
    for s in range(FFN_ROW_SPLIT):
        rows = pl.ds(s * rs, rs)
        xb = xb_ref[rows, :]
        h1 = jnp.dot(xb, w1_ref[...], preferred_element_type=F32)
        h3 = jnp.dot(xb, w3_ref[...], preferred_element_type=F32)
        gate = (_silu(h1) * h3).astype(BF16)
        o_ref[rows, :] += jnp.dot(gate, w2_ref[...], preferred_element_type=F32)

    @pl.when(f == nf - 1)
    def _():
        v = ALPHA * x_ref[...] + 0.5 * o_ref[...]
        o_ref[...] = _layer_norm(v, g_ref[...], b_ref[...])


def _ffn(h, w1, w3, w2, lng, lnb, i, j, k):
    nf = D_FF // TF
    return pl.pallas_call(
        functools.partial(_ffn_kernel, nf=nf),
        grid=(T_ALL // TM, nf),
        in_specs=[
            pl.BlockSpec((TM, D_MODEL), lambda m, f: (m, 0)),
            pl.BlockSpec((None, None, D_MODEL, TF), lambda m, f: (i, j, 0, f)),
            pl.BlockSpec((None, None, D_MODEL, TF), lambda m, f: (i, j, 0, f)),
            pl.BlockSpec((None, None, TF, D_MODEL), lambda m, f: (i, j, f, 0)),
            pl.BlockSpec((None, None, 1, D_MODEL), lambda m, f: (i, k, 0, 0)),
            pl.BlockSpec((None, None, 1, D_MODEL), lambda m, f: (i, k, 0, 0)),
        ],
        out_specs=pl.BlockSpec((TM, D_MODEL), lambda m, f: (m, 0)),
        out_shape=jax.ShapeDtypeStruct((T_ALL, D_MODEL), F32),
        scratch_shapes=[pltpu.VMEM((TM, D_MODEL), BF16)],
        compiler_params=_cparams(("parallel", "arbitrary")),
        name="ffn_ln",
    )(h, w1, w3, w2, lng, lnb)


def _mm_kernel(x_ref, w_ref, o_ref, xb_ref):
    @pl.when(pl.program_id(1) == 0)
    def _():
        xb_ref[...] = x_ref[...].astype(BF16)

    o_ref[...] = jnp.dot(xb_ref[...], w_ref[...], preferred_element_type=F32)


def _mm(h, w, tn):
    k, n = w.shape
    return pl.pallas_call(
        _mm_kernel,
        grid=(T_ALL // TM, n // tn),
        in_specs=[
            pl.BlockSpec((TM, k), lambda m, j: (m, 0)),
            pl.BlockSpec((k, tn), lambda m, j: (0, j)),
        ],
        out_specs=pl.BlockSpec((TM, tn), lambda m, j: (m, j)),
        out_shape=jax.ShapeDtypeStruct((T_ALL, n), F32),
        scratch_shapes=[pltpu.VMEM((TM, k), BF16)],
        compiler_params=_cparams(("parallel", "arbitrary")),
        name="in_proj",
    )(h, w)


def _sc_in_kernel(x_ref, wb_ref, wc_ref, wh_ref, bg_ref, u_ref, xb_ref):
    @pl.when(pl.program_id(1) == 0)
    def _():
        xb_ref[...] = x_ref[...].astype(BF16)

    xb = xb_ref[...]
    bg_ref[...] = jnp.dot(xb, wb_ref[...], preferred_element_type=F32)
    c = jnp.dot(xb, wc_ref[...], preferred_element_type=F32)
    hh = jnp.dot(xb, wh_ref[...], preferred_element_type=F32)
    u_ref[...] = c * hh


def _sc_in(h, w_in, j):
    tn = 512
    nb = D_MODEL // tn
    row = pl.BlockSpec((TM, tn), lambda m, n: (m, n))
    return pl.pallas_call(
        _sc_in_kernel,
        grid=(T_ALL // TM, nb),
        in_specs=[
            pl.BlockSpec((TM, D_MODEL), lambda m, n: (m, 0)),
            pl.BlockSpec((None, D_MODEL, tn), lambda m, n: (j, 0, n)),
            pl.BlockSpec((None, D_MODEL, tn), lambda m, n: (j, 0, n + nb)),
            pl.BlockSpec((None, D_MODEL, tn), lambda m, n: (j, 0, n + 2 * nb)),
        ],
        out_specs=[row, row],
        out_shape=[jax.ShapeDtypeStruct((T_ALL, D_MODEL), F32)] * 2,
        scratch_shapes=[pltpu.VMEM((TM, D_MODEL), BF16)],
        compiler_params=_cparams(("parallel", "arbitrary")),
        name="sc_in_proj",
    )(h, w_in, w_in, w_in)


CONV_TILE = 128
CONV_CB = 2048
HALO = 8


def _conv_kernel(*refs, taps, has_bias, has_gate, act):
    u_ref, halo_ref = refs[0], refs[1]
    ov_refs = refs[2:2 + taps - 1]
    w_ref = refs[2 + taps - 1]
    pos = 3 + taps - 1
    bias_ref = gate_ref = None
    if has_bias:
        bias_ref = refs[pos]
        pos += 1
    if has_gate:
        gate_ref = refs[pos]
        pos += 1
    o_ref = refs[pos]

    i = pl.program_id(0)
    u = u_ref[...]
    halo = jnp.where(i > 0, halo_ref[...], 0.0)
    lrow = lax.broadcasted_iota(jnp.int32, (CONV_TILE, 1), 0)
    srow = lrow + i * CONV_TILE - T_MAIN
    spos = jnp.bitwise_and(srow, DEC_SEQ - 1)
    in_samp = srow >= 0
    w = w_ref[...]
    acc = w[taps - 1:taps, :] * u
    for d in range(1, taps):
        ud = pltpu.roll(u, d, axis=0)
        for r in range(d):
            ud = jnp.where(lrow == r, halo[HALO - d + r:HALO - d + r + 1, :], ud)
        ud = jnp.where(jnp.logical_and(in_samp, spos < d), ov_refs[d - 1][...], ud)
        acc = acc + w[taps - 1 - d:taps - d, :] * ud
    if has_bias:
        acc = acc + bias_ref[...]
    if act:
        acc = _silu(acc)
    if has_gate:
        acc = acc * gate_ref[...]
    o_ref[...] = acc.astype(o_ref.dtype)


def _conv_overrides(prev, taps):
    c = prev.shape[-1]
    out = []
    for d in range(1, taps):
        o = jnp.zeros((DEC_BATCH, DEC_SEQ, c), F32)
        for p in range(d):
            o = o.at[:, p].set(prev[:, taps - 1 + p - d])
        out.append(o.reshape(N_SAMP, c))
    return out


def _conv(u, col_off, n_ch, prev, w, bias, gate, act, out_dtype):
    taps = w.shape[0]
    ncb = n_ch // CONV_CB
    cb0 = col_off // CONV_CB
    tiles_per_halo = CONV_TILE // HALO
    samp_tile0 = T_MAIN // CONV_TILE
    ovs = _conv_overrides(prev, taps)
    in_specs = [
        pl.BlockSpec((CONV_TILE, CONV_CB), lambda i, c: (i, cb0 + c)),
        pl.BlockSpec((HALO, CONV_CB), lambda i, c: (jnp.maximum(i * tiles_per_halo - 1, 0), cb0 + c)),
    ]
    args = [u, u]
    for o in ovs:
        in_specs.append(pl.BlockSpec((CONV_TILE, CONV_CB), lambda i, c: (jnp.maximum(i - samp_tile0, 0), c)))
        args.append(o)
    in_specs.append(pl.BlockSpec((taps, CONV_CB), lambda i, c: (0, c)))
    args.append(w)
    if bias is not None:
        in_specs.append(pl.BlockSpec((1, CONV_CB), lambda i, c: (0, c)))
        args.append(bias.reshape(1, n_ch))
    if gate is not None:
        in_specs.append(pl.BlockSpec((CONV_TILE, CONV_CB), lambda i, c: (i, c)))
        args.append(gate)
    return pl.pallas_call(
        functools.partial(_conv_kernel, taps=taps, has_bias=bias is not None,
                          has_gate=gate is not None, act=act),
        grid=(T_ALL // CONV_TILE, ncb),
        in_specs=in_specs,
        out_specs=pl.BlockSpec((CONV_TILE, CONV_CB), lambda i, c: (i, c)),
        out_shape=jax.ShapeDtypeStruct((T_ALL, n_ch), out_dtype),
        compiler_params=_cparams(("parallel", "parallel")),
        name="causal_conv",
    )(*args)


def _proj_ln_kernel(y_ref, w_ref, h_ref, g_ref, b_ref, o_ref, *, nk):
    rs = TM // PROJ_ROW_SPLIT

    def slab_dot(rows):
        return jnp.dot(y_ref[rows, :], w_ref[...], preferred_element_type=F32)

    def finish(rows, acc):
        v = ALPHA * h_ref[rows, :] + acc
        o_ref[rows, :] = _layer_norm(v, g_ref[...], b_ref[...])

    slabs = [pl.ds(s * rs, rs) for s in range(PROJ_ROW_SPLIT)]
    if nk == 1:
        for rows in slabs:
            finish(rows, slab_dot(rows))
        return

    k = pl.program_id(1)

    @pl.when(k == 0)
    def _():
        for rows in slabs:
            o_ref[rows, :] = slab_dot(rows)

    @pl.when(k == 1)
    def _():
        for rows in slabs:
            finish(rows, o_ref[rows, :] + slab_dot(rows))


def _proj_ln(y, w, j, h, lng, lnb, i, k):
    kdim = w.shape[1]
    tk = 2048
    nk = kdim // tk
    assert nk in (1, 2)
    return pl.pallas_call(
        functools.partial(_proj_ln_kernel, nk=nk),
        grid=(T_ALL // TM, nk),
        in_specs=[
            pl.BlockSpec((TM, tk), lambda m, kk: (m, kk)),
            pl.BlockSpec((None, tk, D_MODEL), lambda m, kk: (j, kk, 0)),
            pl.BlockSpec((TM, D_MODEL), lambda m, kk: (m, 0)),
            pl.BlockSpec((None, None, 1, D_MODEL), lambda m, kk: (i, k, 0, 0)),
            pl.BlockSpec((None, None, 1, D_MODEL), lambda m, kk: (i, k, 0, 0)),
        ],
        out_specs=pl.BlockSpec((TM, D_MODEL), lambda m, kk: (m, 0)),
        out_shape=jax.ShapeDtypeStruct((T_ALL, D_MODEL), F32),
        compiler_params=_cparams(("parallel", "arbitrary")),
        name="out_proj_ln",
    )(y, w, h, lng, lnb)


_NT = (((1,), (1,)), ((), ()))
_TN = (((0,), (0,)), ((), ()))
_HI = lax.Precision.HIGHEST


def _ssd_chunk(x_ref, bm_ref, cm_ref, z_ref, dt_ref, dtt_ref, dtb_r_ref, dtb_c_ref, al_r_ref, al_c_ref,
               d_ref, nw_ref, y_ref, ysc_ref, xw_ref, yint, valid_c, valid_r, causal, same):
    dt = jnp.where(valid_c, _softplus(dt_ref[...] + dtb_r_ref[...]), 0.0)
    dtt = jnp.where(valid_r, _softplus(dtt_ref[...] + dtb_c_ref[...]), 0.0)
    da = dt * (-jnp.exp(al_r_ref[...]))
    dat = dtt * (-jnp.exp(al_c_ref[...]))
    cmask = causal.astype(F32)
    acs = jnp.dot(cmask, da, precision=_HI, preferred_element_type=F32)
    tot = jnp.dot(same.astype(F32), da, precision=_HI, preferred_element_type=F32)
    acst = lax.dot_general(dat, cmask, _NT, precision=_HI, preferred_element_type=F32)

    bmb = bm_ref[...].astype(BF16)
    cmb = cm_ref[...].astype(BF16)
    cb = lax.dot_general(cmb, bmb, _NT, preferred_element_type=F32)
    d_row = d_ref[...]
    for r in range(HEADS_PER_GROUP):
        lo, hi = r * HEADDIM, (r + 1) * HEADDIM
        col = acs[:, r:r + 1]
        row = acst[r:r + 1, :]
        decay = jnp.exp(jnp.where(causal, col - row, -jnp.inf))
        scores = (cb * decay).astype(BF16)
        xh = jnp.where(valid_c, x_ref[:, lo:hi], 0.0)
        xdt = xh * dt[:, r:r + 1]
        yh = jnp.dot(scores, xdt.astype(BF16), preferred_element_type=F32)
        yh = yh + yint[:, lo:hi] * jnp.exp(col)
        yh = yh + d_row[:, r:r + 1] * xh
        ysc_ref[:, lo:hi] = yh
        to_end = jnp.exp(tot[:, r:r + 1] - col)
        xw_ref[:, lo:hi] = (xdt * to_end).astype(BF16)
    y = ysc_ref[...] * _silu(z_ref[...])
    ms = jnp.mean(y * y, axis=-1, keepdims=True)
    y_ref[...] = (y * lax.rsqrt(ms + RMS_EPS) * nw_ref[...]).astype(y_ref.dtype)
    return bmb, cmb, tot


def _chunk_masks():
    li = lax.broadcasted_iota(jnp.int32, (CHUNK, CHUNK), 0)
    si = lax.broadcasted_iota(jnp.int32, (CHUNK, CHUNK), 1)
    return li, si


def _ssd_main_kernel(x_ref, bm_ref, cm_ref, z_ref, dt_ref, dtt_ref, dtb_r_ref, dtb_c_ref, al_r_ref, al_c_ref,
                     d_ref, nw_ref, y_ref, sfin_ref, s_ref, ysc_ref, xw_ref):
    c = pl.program_id(1)

    @pl.when(c == 0)
    def _():
        s_ref[...] = jnp.zeros_like(s_ref)

    li, si = _chunk_masks()
    causal = si <= li
    same = li >= 0
    base = c * CHUNK
    valid_c = lax.broadcasted_iota(jnp.int32, (CHUNK, 1), 0) + base < T_PROMPT
    valid_r = lax.broadcasted_iota(jnp.int32, (1, CHUNK), 1) + base < T_PROMPT

    cmb = cm_ref[...].astype(BF16)
    yint = lax.dot_general(cmb, s_ref[...].astype(BF16), _NT, preferred_element_type=F32)
    bmb, _, tot = _ssd_chunk(x_ref, bm_ref, cm_ref, z_ref, dt_ref, dtt_ref, dtb_r_ref, dtb_c_ref,
                             al_r_ref, al_c_ref, d_ref, nw_ref, y_ref, ysc_ref, xw_ref,
                             yint, valid_c, valid_r, causal, same)
    upd = lax.dot_general(xw_ref[...], bmb, _TN, preferred_element_type=F32)
    for r in range(HEADS_PER_GROUP):
        lo, hi = r * HEADDIM, (r + 1) * HEADDIM
        s_ref[lo:hi, :] = s_ref[lo:hi, :] * jnp.exp(tot[0:1, r:r + 1]) + upd[lo:hi, :]

    @pl.when(c == N_MAIN_CHUNKS - 1)
    def _():
        sfin_ref[...] = s_ref[...]


def _ssd_samp_kernel(x_ref, bm_ref, cm_ref, z_ref, dt_ref, dtt_ref, dtb_r_ref, dtb_c_ref, al_r_ref, al_c_ref,
                     d_ref, nw_ref, s0_ref, yprev_ref, y_ref, s1_ref, yint_ref, ysc_ref, xw_ref):
    del yprev_ref
    li, si = _chunk_masks()
    same = (li // DEC_SEQ) == (si // DEC_SEQ)
    causal = jnp.logical_and(si <= li, same)
    valid_c = lax.broadcasted_iota(jnp.int32, (CHUNK, 1), 0) >= 0
    valid_r = lax.broadcasted_iota(jnp.int32, (1, CHUNK), 1) >= 0

    cmb = cm_ref[...].astype(BF16)
    for q in range(SEQ_PER_CHUNK):
        r0, r1 = q * DEC_SEQ, (q + 1) * DEC_SEQ
        yint_ref[r0:r1, :] = lax.dot_general(cmb[r0:r1, :], s0_ref[q].astype(BF16), _NT,
                                             preferred_element_type=F32)
    bmb, _, tot = _ssd_chunk(x_ref, bm_ref, cm_ref, z_ref, dt_ref, dtt_ref, dtb_r_ref, dtb_c_ref,
                             al_r_ref, al_c_ref, d_ref, nw_ref, y_ref, ysc_ref, xw_ref,
                             yint_ref[...], valid_c, valid_r, causal, same)
    for q in range(SEQ_PER_CHUNK):
        r0, r1 = q * DEC_SEQ, (q + 1) * DEC_SEQ
        upd = lax.dot_general(xw_ref[r0:r1, :], bmb[r0:r1, :], _TN, preferred_element_type=F32)
        for r in range(HEADS_PER_GROUP):
            lo, hi = r * HEADDIM, (r + 1) * HEADDIM
            s1_ref[q, lo:hi, :] = s0_ref[q, lo:hi, :] * jnp.exp(tot[r0:r0 + 1, r:r + 1]) + upd[lo:hi, :]


def _ssd_common_specs(c0):
    gw = GROUP_W // STATE
    return [
        pl.BlockSpec((CHUNK, GROUP_W), lambda g, c: (c0 + c, g)),
        pl.BlockSpec((CHUNK, STATE), lambda g, c: (c0 + c, D_INNER // STATE + g)),
        pl.BlockSpec((CHUNK, STATE), lambda g, c: (c0 + c, D_INNER // STATE + GROUPS + g)),
        pl.BlockSpec((CHUNK, GROUP_W), lambda g, c: (c0 + c, g)),
        pl.BlockSpec((None, CHUNK, 128), lambda g, c: (g, c0 + c, 0)),
        pl.BlockSpec((None, HEADS_PER_GROUP, CHUNK), lambda g, c: (g, 0, c0 + c)),
        pl.BlockSpec((None, 1, 128), lambda g, c: (g, 0, 0)),
        pl.BlockSpec((None, HEADS_PER_GROUP, 1), lambda g, c: (g, 0, 0)),
        pl.BlockSpec((None, 1, 128), lambda g, c: (g, 0, 0)),
        pl.BlockSpec((None, HEADS_PER_GROUP, 1), lambda g, c: (g, 0, 0)),
        pl.BlockSpec((None, 1, 128), lambda g, c: (g, 0, 0)),
        pl.BlockSpec((1, GROUP_W), lambda g, c: (0, g)),
    ]


def _ssd(xbc, zxd, dt_g, dt_gt, dtb_r, dtb_c, al_r, al_c, d_r, norm_w, state0):
    common = (xbc, xbc, xbc, zxd, dt_g, dt_gt, dtb_r, dtb_c, al_r, al_c, d_r, norm_w)
    y, s_fin = pl.pallas_call(
        _ssd_main_kernel,
        grid=(GROUPS, N_MAIN_CHUNKS),
        in_specs=_ssd_common_specs(0),
        out_specs=[
            pl.BlockSpec((CHUNK, GROUP_W), lambda g, c: (c, g)),
            pl.BlockSpec((None, GROUP_W, STATE), lambda g, c: (g, 0, 0)),
        ],
        out_shape=[
            jax.ShapeDtypeStruct((T_ALL, D_INNER), BF16),
            jax.ShapeDtypeStruct((GROUPS, GROUP_W, STATE), F32),
        ],
        scratch_shapes=[
            pltpu.VMEM((GROUP_W, STATE), F32),
            pltpu.VMEM((CHUNK, GROUP_W), F32),
            pltpu.VMEM((CHUNK, GROUP_W), BF16),
        ],
        compiler_params=_cparams(("parallel", "arbitrary")),
        name="ssd_prompt",
    )(*common)

    y, s_new = pl.pallas_call(
        _ssd_samp_kernel,
        grid=(GROUPS, N_SAMP_CHUNKS),
        in_specs=_ssd_common_specs(N_MAIN_CHUNKS) + [
            pl.BlockSpec((SEQ_PER_CHUNK, None, GROUP_W, STATE), lambda g, c: (c, g, 0, 0)),
            pl.BlockSpec(memory_space=pl.ANY),
        ],
        out_specs=[
            pl.BlockSpec((CHUNK, GROUP_W), lambda g, c: (N_MAIN_CHUNKS + c, g)),
            pl.BlockSpec((SEQ_PER_CHUNK, None, GROUP_W, STATE), lambda g, c: (c, g, 0, 0)),
        ],
        out_shape=[
            jax.ShapeDtypeStruct((T_ALL, D_INNER), BF16),
            jax.ShapeDtypeStruct((DEC_BATCH, GROUPS, GROUP_W, STATE), F32),
        ],
        scratch_shapes=[
            pltpu.VMEM((CHUNK, GROUP_W), F32),
            pltpu.VMEM((CHUNK, GROUP_W), F32),
            pltpu.VMEM((CHUNK, GROUP_W), BF16),
        ],
        input_output_aliases={13: 0},
        compiler_params=_cparams(("parallel", "arbitrary")),
        name="ssd_sample",
    )(*common, state0, y)
    return y, s_fin, s_new


def _group_rows(v):
    return jnp.pad(v.reshape(GROUPS, 1, HEADS_PER_GROUP).astype(F32), ((0, 0), (0, 0), (0, 128 - HEADS_PER_GROUP)))


def _group_cols(v):
    return v.reshape(GROUPS, HEADS_PER_GROUP, 1).astype(F32)


def _tail_rows(a, n):
    c = a.shape[-1]
    p = a[T_PROMPT - n:T_PROMPT].reshape(1, 1, n, c)
    s = a[T_MAIN:].reshape(DEC_BATCH, DEC_SEQ, c)[:, DEC_SEQ - n:].reshape(1, DEC_BATCH, n, c)
    return p, s


def kernel(x_prompt, x_sample, cache_sc_conv, state_ssd_conv, state_ssd, meta_tokens, ln_g, ln_b,
           ffn_w1, ffn_w3, ffn_w2, sc_w_in, sc_w_conv, sc_w_out,
           ssd_w_in, ssd_w_conv, ssd_b_conv, ssd_dt_bias, ssd_a_log, ssd_d, ssd_norm_w, ssd_w_out):
    w1 = ffn_w1.astype(BF16)
    w3 = ffn_w3.astype(BF16)
    w2 = ffn_w2.astype(BF16)
    sc_in_w = sc_w_in.astype(BF16)
    sc_out_w = sc_w_out.astype(BF16)
    ssd_in_w = jnp.pad(ssd_w_in[0], ((0, 0), (0, ZXD_W - ssd_w_in.shape[-1]))).astype(BF16)
    ssd_out_w = ssd_w_out.astype(BF16)
    lng = ln_g.reshape(DEPTH, 3, 1, D_MODEL)
    lnb = ln_b.reshape(DEPTH, 3, 1, D_MODEL)

    h = jnp.concatenate([meta_tokens.astype(F32), x_prompt[0], jnp.zeros((N_PAD, D_MODEL), F32),
                         x_sample.reshape(N_SAMP, D_MODEL)], axis=0)

    h = _ffn(h, w1, w3, w2, lng, lnb, 0, 0, 0)
    bgate, u = _sc_in(h, sc_in_w, 0)
    v = _conv(u, 0, D_MODEL, cache_sc_conv[0], sc_w_conv[0], None, bgate, False, BF16)
    new_sc_p, new_sc_s = _tail_rows(u, SC_WIDTH - 1)
    h = _proj_ln(v, sc_out_w, 0, h, lng, lnb, 0, 1)
    h = _ffn(h, w1, w3, w2, lng, lnb, 0, 1, 2)

    h = _ffn(h, w1, w3, w2, lng, lnb, 1, 0, 0)
    zxd = _mm(h, ssd_in_w, ZXD_TN)
    xbc = _conv(zxd, D_INNER, CONV_DIM, state_ssd_conv[0], ssd_w_conv[0], ssd_b_conv[0], None, True, F32)
    new_conv_p, new_conv_s = _tail_rows(zxd[:, D_INNER:DT_COL], SSD_CONV_WIDTH - 1)

    dt_raw = zxd[:, DT_COL:DT_COL + HEADS]
    dt_g = jnp.pad(dt_raw.reshape(T_ALL, GROUPS, HEADS_PER_GROUP).transpose(1, 0, 2),
                   ((0, 0), (0, 0), (0, 128 - HEADS_PER_GROUP)))
    dt_gt = dt_raw.T.reshape(GROUPS, HEADS_PER_GROUP, T_ALL)
    state0 = state_ssd[0].reshape(DEC_BATCH, GROUPS, GROUP_W, STATE)
    y, s_fin, s_new = _ssd(xbc, zxd, dt_g, dt_gt,
                           _group_rows(ssd_dt_bias[0]), _group_cols(ssd_dt_bias[0]),
                           _group_rows(ssd_a_log[0]), _group_cols(ssd_a_log[0]),
                           _group_rows(ssd_d[0]), ssd_norm_w.astype(F32), state0)
    h = _proj_ln(y, ssd_out_w, 0, h, lng, lnb, 1, 1)
    h = _ffn(h, w1, w3, w2, lng, lnb, 1, 1, 2)

    y_prompt = h[N_META:T_PROMPT].reshape(1, SEQ, D_MODEL)
    y_sample = h[T_MAIN:].reshape(DEC_BATCH, DEC_SEQ, D_MODEL)
    new_ssd_p = s_fin.reshape(1, 1, HEADS, HEADDIM, STATE).astype(state_ssd.dtype)
    new_ssd_s = s_new.reshape(1, DEC_BATCH, HEADS, HEADDIM, STATE).astype(state_ssd.dtype)
    return (y_prompt, y_sample, new_sc_p, new_sc_s, new_conv_p, new_conv_s, new_ssd_p, new_ssd_s)
```

```python
import functools

import numpy as np

import jax
import jax.numpy as jnp
from jax import lax
from jax.experimental import pallas as pl
from jax.experimental.pallas import tpu as pltpu

F32 = jnp.float32
BF16 = jnp.bfloat16

D_MODEL = 2048
SEQ = 8192
DEPTH = 2
DEC_BATCH = 32
DEC_SEQ = 16
N_META = 16
D_FF = 5632
SC_WIDTH = 3
D_INNER = 4096
HEADDIM = 64
HEADS = 64
GROUPS = 8
HEADS_PER_GROUP = 8
STATE = 128
SSD_CONV_WIDTH = 4
CONV_DIM = D_INNER + 2 * GROUPS * STATE
GROUP_W = HEADS_PER_GROUP * HEADDIM
XBC_W = GROUP_W + 2 * STATE
ALPHA = (2.0 * DEPTH) ** 0.25
LN_EPS = 1e-5
RMS_EPS = 1e-5

CHUNK = 128
T_PROMPT = N_META + SEQ
N_MAIN_CHUNKS = -(-T_PROMPT // CHUNK)
T_MAIN = N_MAIN_CHUNKS * CHUNK
N_PAD = T_MAIN - T_PROMPT
N_SAMP = DEC_BATCH * DEC_SEQ
N_SAMP_CHUNKS = N_SAMP // CHUNK
SEQ_PER_CHUNK = CHUNK // DEC_SEQ
T_ALL = T_MAIN + N_SAMP

TM = 736
TF = 512
FFN_ROW_SPLIT = 2
PROJ_ROW_SPLIT = 2
ZXD_W = 10368
ZXD_TN = 1152
DT_COL = D_INNER + CONV_DIM
SSD_NB = 5
SSD_RB = SSD_NB * CHUNK
LANES = 128
SUBLANES = 8

VMEM_LIMIT_BYTES = 56 * 1024 * 1024


def _cparams(sem):
    return pltpu.CompilerParams(dimension_semantics=sem, vmem_limit_bytes=VMEM_LIMIT_BYTES)


def _layer_norm(v, g, b):
    mu = jnp.mean(v, axis=-1, keepdims=True)
    c = v - mu
    var = jnp.mean(c * c, axis=-1, keepdims=True)
    return c * lax.rsqrt(var + LN_EPS) * g + b


def _silu(x):
    return x * jax.nn.sigmoid(x)


def _softplus(x):
    return jnp.maximum(x, 0.0) + jnp.log1p(jnp.exp(-jnp.abs(x)))


def _ffn_kernel(x_ref, w1_ref, w3_ref, w2_ref, g_ref, b_ref, o_ref, xb_ref, *, nf):
    f = pl.program_id(1)

    @pl.when(f == 0)
    def _():
        xb_ref[...] = x_ref[...].astype(BF16)
        o_ref[...] = jnp.zeros_like(o_ref)

    rs = TM // FFN_ROW_SPLIT
    for s in range(FFN_ROW_SPLIT):
        rows = pl.ds(s * rs, rs)
        xb = xb_ref[rows, :]
        h1 = jnp.dot(xb, w1_ref[...], preferred_element_type=F32)
        h3 = jnp.dot(xb, w3_ref[...], preferred_element_type=F32)
        gate = (_silu(h1) * h3).astype(BF16)
        o_ref[rows, :] += jnp.dot(gate, w2_ref[...], preferred_element_type=F32)

    @pl.when(f == nf - 1)
    def _():
        v = ALPHA * x_ref[...] + 0.5 * o_ref[...]
        o_ref[...] = _layer_norm(v, g_ref[...], b_ref[...])


def _ffn(h, w1, w3, w2, lng, lnb, i, j, k):
    nf = D_FF // TF
    return pl.pallas_call(
        functools.partial(_ffn_kernel, nf=nf),
        grid=(T_ALL // TM, nf),
        in_specs=[
            pl.BlockSpec((TM, D_MODEL), lambda m, f: (m, 0)),
            pl.BlockSpec((None, None, D_MODEL, TF), lambda m, f: (i, j, 0, f)),
            pl.BlockSpec((None, None, D_MODEL, TF), lambda m, f: (i, j, 0, f)),
            pl.BlockSpec((None, None, TF, D_MODEL), lambda m, f: (i, j, f, 0)),
            pl.BlockSpec((None, None, 1, D_MODEL), lambda m, f: (i, k, 0, 0)),
            pl.BlockSpec((None, None, 1, D_MODEL), lambda m, f: (i, k, 0, 0)),
        ],
        out_specs=pl.BlockSpec((TM, D_MODEL), lambda m, f: (m, 0)),
        out_shape=jax.ShapeDtypeStruct((T_ALL, D_MODEL), F32),
        scratch_shapes=[pltpu.VMEM((TM, D_MODEL), BF16)],
        compiler_params=_cparams(("parallel", "arbitrary")),
        name="ffn_ln",
    )(h, w1, w3, w2, lng, lnb)


def _mm_kernel(x_ref, w_ref, o_ref, xb_ref):
    @pl.when(pl.program_id(1) == 0)
    def _():
        xb_ref[...] = x_ref[...].astype(BF16)

    o_ref[...] = jnp.dot(xb_ref[...], w_ref[...], preferred_element_type=F32)


def _mm(h, w, tn):
    k, n = w.shape
    return pl.pallas_call(
        _mm_kernel,
        grid=(T_ALL // TM, n // tn),
        in_specs=[
            pl.BlockSpec((TM, k), lambda m, j: (m, 0)),
            pl.BlockSpec((k, tn), lambda m, j: (0, j)),
        ],
        out_specs=pl.BlockSpec((TM, tn), lambda m, j: (m, j)),
        out_shape=jax.ShapeDtypeStruct((T_ALL, n), F32),
        scratch_shapes=[pltpu.VMEM((TM, k), BF16)],
        compiler_params=_cparams(("parallel", "arbitrary")),
        name="in_proj",
    )(h, w)


def _sc_in_kernel(x_ref, wb_ref, wc_ref, wh_ref, bg_ref, u_ref, xb_ref):
    @pl.when(pl.program_id(1) == 0)
    def _():
        xb_ref[...] = x_ref[...].astype(BF16)

    xb = xb_ref[...]
    bg_ref[...] = jnp.dot(xb, wb_ref[...], preferred_element_type=F32)
    c = jnp.dot(xb, wc_ref[...], preferred_element_type=F32)
    hh = jnp.dot(xb, wh_ref[...], preferred_element_type=F32)
    u_ref[...] = c * hh


def _sc_in(h, w_in, j):
    tn = 512
    nb = D_MODEL // tn
    row = pl.BlockSpec((TM, tn), lambda m, n: (m, n))
    return pl.pallas_call(
        _sc_in_kernel,
        grid=(T_ALL // TM, nb),
        in_specs=[
            pl.BlockSpec((TM, D_MODEL), lambda m, n: (m, 0)),
            pl.BlockSpec((None, D_MODEL, tn), lambda m, n: (j, 0, n)),
            pl.BlockSpec((None, D_MODEL, tn), lambda m, n: (j, 0, n + nb)),
            pl.BlockSpec((None, D_MODEL, tn), lambda m, n: (j, 0, n + 2 * nb)),
        ],
        out_specs=[row, row],
        out_shape=[jax.ShapeDtypeStruct((T_ALL, D_MODEL), F32)] * 2,
        scratch_shapes=[pltpu.VMEM((TM, D_MODEL), BF16)],
        compiler_params=_cparams(("parallel", "arbitrary")),
        name="sc_in_proj",
    )(h, w_in, w_in, w_in)


CONV_TILE = 128
CONV_CB = 2048
HALO = 8


def _conv_kernel(*refs, taps, has_bias, has_gate, act):
    u_ref, halo_ref = refs[0], refs[1]
    ov_refs = refs[2:2 + taps - 1]
    w_ref = refs[2 + taps - 1]
    pos = 3 + taps - 1
    bias_ref = gate_ref = None
    if has_bias:
        bias_ref = refs[pos]
        pos += 1
    if has_gate:
        gate_ref = refs[pos]
        pos += 1
    o_ref = refs[pos]

    i = pl.program_id(0)
    u = u_ref[...]
    halo = jnp.where(i > 0, halo_ref[...], 0.0)
    lrow = lax.broadcasted_iota(jnp.int32, (CONV_TILE, 1), 0)
    srow = lrow + i * CONV_TILE - T_MAIN
    spos = jnp.bitwise_and(srow, DEC_SEQ - 1)
    in_samp = srow >= 0
    w = w_ref[...]
    acc = w[taps - 1:taps, :] * u
    for d in range(1, taps):
        ud = pltpu.roll(u, d, axis=0)
        for r in range(d):
            ud = jnp.where(lrow == r, halo[HALO - d + r:HALO - d + r + 1, :], ud)
        ud = jnp.where(jnp.logical_and(in_samp, spos < d), ov_refs[d - 1][...], ud)
        acc = acc + w[taps - 1 - d:taps - d, :] * ud
    if has_bias:
        acc = acc + bias_ref[...]
    if act:
        acc = _silu(acc)
    if has_gate:
        acc = acc * gate_ref[...]
    o_ref[...] = acc.astype(o_ref.dtype)


def _conv_overrides(prev, taps):
    c = prev.shape[-1]
    out = []
    for d in range(1, taps):
        o = jnp.zeros((DEC_BATCH, DEC_SEQ, c), F32)
        for p in range(d):
            o = o.at[:, p].set(prev[:, taps - 1 + p - d])
        out.append(o.reshape(N_SAMP, c))
    return out


def _conv(u, col_off, n_ch, prev, w, bias, gate, act, out_dtype):
    taps = w.shape[0]
    ncb = n_ch // CONV_CB
    cb0 = col_off // CONV_CB
    tiles_per_halo = CONV_TILE // HALO
    samp_tile0 = T_MAIN // CONV_TILE
    ovs = _conv_overrides(prev, taps)
    in_specs = [
        pl.BlockSpec((CONV_TILE, CONV_CB), lambda i, c: (i, cb0 + c)),
        pl.BlockSpec((HALO, CONV_CB), lambda i, c: (jnp.maximum(i * tiles_per_halo - 1, 0), cb0 + c)),
    ]
    args = [u, u]
    for o in ovs:
        in_specs.append(pl.BlockSpec((CONV_TILE, CONV_CB), lambda i, c: (jnp.maximum(i - samp_tile0, 0), c)))
        args.append(o)
    in_specs.append(pl.BlockSpec((taps, CONV_CB), lambda i, c: (0, c)))
    args.append(w)
    if bias is not None:
        in_specs.append(pl.BlockSpec((1, CONV_CB), lambda i, c: (0, c)))
        args.append(bias.reshape(1, n_ch))
    if gate is not None:
        in_specs.append(pl.BlockSpec((CONV_TILE, CONV_CB), lambda i, c: (i, c)))
        args.append(gate)
    return pl.pallas_call(
        functools.partial(_conv_kernel, taps=taps, has_bias=bias is not None,
                          has_gate=gate is not None, act=act),
        grid=(T_ALL // CONV_TILE, ncb),
        in_specs=in_specs,
        out_specs=pl.BlockSpec((CONV_TILE, CONV_CB), lambda i, c: (i, c)),
        out_shape=jax.ShapeDtypeStruct((T_ALL, n_ch), out_dtype),
        compiler_params=_cparams(("parallel", "parallel")),
        name="causal_conv",
    )(*args)


def _proj_ln_kernel(y_ref, w_ref, h_ref, g_ref, b_ref, o_ref, *, nk):
    rs = TM // PROJ_ROW_SPLIT

    def slab_dot(rows):
        return jnp.dot(y_ref[rows, :], w_ref[...], preferred_element_type=F32)

    def finish(rows, acc):
        v = ALPHA * h_ref[rows, :] + acc
        o_ref[rows, :] = _layer_norm(v, g_ref[...], b_ref[...])

    slabs = [pl.ds(s * rs, rs) for s in range(PROJ_ROW_SPLIT)]
    if nk == 1:
        for rows in slabs:
            finish(rows, slab_dot(rows))
        return

    k = pl.program_id(1)

    @pl.when(k == 0)
    def _():
        for rows in slabs:
            o_ref[rows, :] = slab_dot(rows)

    @pl.when(k == 1)
    def _():
        for rows in slabs:
            finish(rows, o_ref[rows, :] + slab_dot(rows))


def _proj_ln(y, w, j, h, lng, lnb, i, k):
    kdim = w.shape[1]
    tk = 2048
    nk = kdim // tk
    assert nk in (1, 2)
    return pl.pallas_call(
        functools.partial(_proj_ln_kernel, nk=nk),
        grid=(T_ALL // TM, nk),
        in_specs=[
            pl.BlockSpec((TM, tk), lambda m, kk: (m, kk)),
            pl.BlockSpec((None, tk, D_MODEL), lambda m, kk: (j, kk, 0)),
            pl.BlockSpec((TM, D_MODEL), lambda m, kk: (m, 0)),
            pl.BlockSpec((None, None, 1, D_MODEL), lambda m, kk: (i, k, 0, 0)),
            pl.BlockSpec((None, None, 1, D_MODEL), lambda m, kk: (i, k, 0, 0)),
        ],
        out_specs=pl.BlockSpec((TM, D_MODEL), lambda m, kk: (m, 0)),
        out_shape=jax.ShapeDtypeStruct((T_ALL, D_MODEL), F32),
        compiler_params=_cparams(("parallel", "arbitrary")),
        name="out_proj_ln",
    )(y, w, h, lng, lnb)


_NT = (((1,), (1,)), ((), ()))
_TN = (((0,), (0,)), ((), ()))
SPLIT_K = 2 * LANES
CONV_HEAD = SUBLANES


def _bf16_pieces(v):
    hi = v.astype(BF16)
    r1 = v - hi.astype(F32)
    mid = r1.astype(BF16)
    lo = (r1 - mid.astype(F32)).astype(BF16)
    return hi, mid, lo


def _split_pack(v, lane_lo):
    hi, mid, lo = _bf16_pieces(v)
    a = jnp.where(lane_lo, hi.astype(F32), pltpu.roll(mid.astype(F32), HEADS, axis=1))
    b = jnp.where(lane_lo, lo.astype(F32), 0.0)
    return jnp.concatenate([a, b], axis=1).astype(BF16)


def _masked_cumsum(mask_bf, da):
    hi, mid, lo = _bf16_pieces(da)
    p = jnp.dot(mask_bf, jnp.concatenate([hi, mid, lo], axis=1), preferred_element_type=F32)
    return p[:, :LANES] + p[:, LANES:2 * LANES] + p[:, 2 * LANES:]


def _conv_silu(buf_ref, base, n, cw, cbias):
    taps = SSD_CONV_WIDTH
    acc = cw[taps - 1:taps, :] * buf_ref[base:base + n, :]
    for d in range(1, taps):
        acc = acc + cw[taps - 1 - d:taps - d, :] * buf_ref[base - d:base - d + n, :]
    return _silu(acc + cbias)


def _intra_chunk(xbc, dt, acs, te, arow, causal, lane_lo, r_ref, e_ref):
    xs = xbc[:, :GROUP_W]
    bmb = xbc[:, GROUP_W:GROUP_W + STATE].astype(BF16)
    cmb = xbc[:, GROUP_W + STATE:].astype(BF16)
    colmat = jnp.dot(_split_pack(acs, lane_lo), r_ref[...], preferred_element_type=F32)
    packed = jnp.concatenate([_split_pack(dt, lane_lo), _split_pack(te, lane_lo),
                              _split_pack(jnp.exp(acs), lane_lo)], axis=0)
    ex = jnp.dot(packed, e_ref[...], preferred_element_type=F32)
    dtx, tex, eax = ex[:CHUNK], ex[CHUNK:2 * CHUNK], ex[2 * CHUNK:]
    xdt = xs * dtx
    cb = lax.dot_general(cmb, bmb, _NT, preferred_element_type=F32)
    ys = []
    for q in range(HEADS_PER_GROUP // 2):
        scs = []
        for r in (2 * q, 2 * q + 1):
            seg = colmat[:, r * CHUNK:(r + 1) * CHUNK] - arow[r:r + 1, :]
            scs.append((cb * jnp.exp(jnp.where(causal, seg, -jnp.inf))).astype(BF16))
        xp = xdt[:, q * LANES:(q + 1) * LANES]
        rhs = jnp.concatenate([jnp.where(lane_lo, xp, 0.0).astype(BF16),
                               jnp.where(lane_lo, 0.0, xp).astype(BF16)], axis=0)
        ys.append(jnp.dot(jnp.concatenate(scs, axis=1), rhs, preferred_element_type=F32))
    y_intra = jnp.concatenate(ys, axis=1)
    xw = (xdt * tex).astype(BF16)
    return xs, bmb, cmb, y_intra, eax, xw


def _gate_norm(y, z, nw):
    y = y * _silu(z)
    ms = jnp.mean(y * y, axis=-1, keepdims=True)
    return (y * lax.rsqrt(ms + RMS_EPS) * nw).astype(BF16)


def _chunk_iotas():
    li = lax.broadcasted_iota(jnp.int32, (CHUNK, CHUNK), 0)
    si = lax.broadcasted_iota(jnp.int32, (CHUNK, CHUNK), 1)
    return li, si


def _ssd_main_kernel(z_ref, xr_ref, br_ref, cr_ref, dt_ref, cw_ref, cbias_ref, dtb_ref, al_ref, dx_ref, nw_ref,
                     r_ref, e_ref, y_ref, sfin_ref, buf_ref, st_ref, acst_ref):
    g = pl.program_id(0)
    rb = pl.program_id(1)

    @pl.when(rb == 0)
    def _():
        st_ref[...] = jnp.zeros_like(st_ref)
        buf_ref[0:CONV_HEAD, :] = jnp.zeros((CONV_HEAD, XBC_W), F32)

    @pl.when(rb > 0)
    def _():
        buf_ref[0:CONV_HEAD, :] = buf_ref[SSD_RB:SSD_RB + CONV_HEAD, :]

    buf_ref[CONV_HEAD:, 0:GROUP_W] = xr_ref[...]
    buf_ref[CONV_HEAD:, GROUP_W:GROUP_W + STATE] = br_ref[...]
    buf_ref[CONV_HEAD:, GROUP_W + STATE:] = cr_ref[...]

    li, si = _chunk_iotas()
    causal = si <= li
    causal_bf = causal.astype(F32).astype(BF16)
    lane_lo = si < HEADS
    row_iota = lax.broadcasted_iota(jnp.int32, (CHUNK, 1), 0)
    g8 = pl.multiple_of(g * HEADS_PER_GROUP, SUBLANES)
    cw = cw_ref[...]
    cbias = cbias_ref[...]
    a_row = -jnp.exp(al_ref[...])

    for k in range(SSD_NB):
        rows = pl.ds(k * CHUNK, CHUNK)
        xbc = _conv_silu(buf_ref, CONV_HEAD + k * CHUNK, CHUNK, cw, cbias)
        valid = row_iota + (rb * SSD_RB + k * CHUNK) < T_PROMPT
        dt = jnp.where(valid, _softplus(dt_ref[rows, :] + dtb_ref[...]), 0.0)
        acs = _masked_cumsum(causal_bf, dt * a_row)
        te = jnp.exp(acs[CHUNK - 1:CHUNK, :] - acs)
        acst_ref[...] = acs.T
        arow = acst_ref[pl.ds(g8, HEADS_PER_GROUP), :]
        xs, bmb, cmb, y_intra, eax, xw = _intra_chunk(xbc, dt, acs, te, arow, causal, lane_lo, r_ref, e_ref)
        st = st_ref[...]
        y = y_intra + jnp.dot(cmb, st.astype(BF16), preferred_element_type=F32) * eax + dx_ref[...] * xs
        y_ref[rows, :] = _gate_norm(y, z_ref[rows, :], nw_ref[...])
        st_ref[...] = st * eax[CHUNK - 1:CHUNK, :] + lax.dot_general(bmb, xw, _TN, preferred_element_type=F32)

    @pl.when(rb == T_MAIN // SSD_RB - 1)
    def _():
        sfin_ref[...] = st_ref[...]


def _ssd_samp_kernel(z_ref, xr_ref, br_ref, cr_ref, dt_ref, cw_ref, cbias_ref, dtb_ref, al_ref, dx_ref, nw_ref,
                     r_ref, e_ref, prev_ref, s0_ref, yprev_ref, y_ref, s1_ref,
                     buf_ref, xbc_ref, yint_ref, acst_ref, tott_ref):
    del yprev_ref
    g = pl.program_id(0)
    slot = CONV_HEAD + DEC_SEQ
    cw = cw_ref[...]
    cbias = cbias_ref[...]
    for q in range(SEQ_PER_CHUNK):
        r0 = q * DEC_SEQ
        top = q * slot + CONV_HEAD
        buf_ref[top - (SSD_CONV_WIDTH - 1):top, :] = prev_ref[q]
        buf_ref[top:top + DEC_SEQ, 0:GROUP_W] = xr_ref[r0:r0 + DEC_SEQ, :]
        buf_ref[top:top + DEC_SEQ, GROUP_W:GROUP_W + STATE] = br_ref[r0:r0 + DEC_SEQ, :]
        buf_ref[top:top + DEC_SEQ, GROUP_W + STATE:] = cr_ref[r0:r0 + DEC_SEQ, :]
    for q in range(SEQ_PER_CHUNK):
        r0 = q * DEC_SEQ
        xbc_ref[r0:r0 + DEC_SEQ, :] = _conv_silu(buf_ref, q * slot + CONV_HEAD, DEC_SEQ, cw, cbias)

    li, si = _chunk_iotas()
    same = (li // DEC_SEQ) == (si // DEC_SEQ)
    causal = jnp.logical_and(si <= li, same)
    lane_lo = si < HEADS
    g8 = pl.multiple_of(g * HEADS_PER_GROUP, SUBLANES)

    dt = _softplus(dt_ref[...] + dtb_ref[...])
    da = dt * (-jnp.exp(al_ref[...]))
    acs = _masked_cumsum(causal.astype(F32).astype(BF16), da)
    tot = _masked_cumsum(same.astype(F32).astype(BF16), da)
    te = jnp.exp(tot - acs)
    acst_ref[...] = acs.T
    tott_ref[...] = tot.T
    arow = acst_ref[pl.ds(g8, HEADS_PER_GROUP), :]
    trow = tott_ref[pl.ds(g8, HEADS_PER_GROUP), :]
    xs, bmb, cmb, y_intra, eax, xw = _intra_chunk(xbc_ref[...], dt, acs, te, arow, causal, lane_lo, r_ref, e_ref)
    for q in range(SEQ_PER_CHUNK):
        r0 = q * DEC_SEQ
        yint_ref[r0:r0 + DEC_SEQ, :] = lax.dot_general(cmb[r0:r0 + DEC_SEQ, :], s0_ref[q].astype(BF16), _NT,
                                                       preferred_element_type=F32)
    y = y_intra + yint_ref[...] * eax + dx_ref[...] * xs
    y_ref[...] = _gate_norm(y, z_ref[...], nw_ref[...])
    for q in range(SEQ_PER_CHUNK):
        r0 = q * DEC_SEQ
        upd = lax.dot_general(xw[r0:r0 + DEC_SEQ, :], bmb[r0:r0 + DEC_SEQ, :], _TN, preferred_element_type=F32)
        for r in range(HEADS_PER_GROUP):
            lo, hi = r * HEADDIM, (r + 1) * HEADDIM
            s1_ref[q, lo:hi, :] = s0_ref[q, lo:hi, :] * jnp.exp(trow[r:r + 1, r0:r0 + 1]) + upd[lo:hi, :]


def _ssd_specs(rows, row0):
    xcol = D_INNER // GROUP_W
    bcol = (2 * D_INNER) // STATE
    return [
        pl.BlockSpec((rows, GROUP_W), lambda g, c: (row0 + c, g)),
        pl.BlockSpec((rows, GROUP_W), lambda g, c: (row0 + c, xcol + g)),
        pl.BlockSpec((rows, STATE), lambda g, c: (row0 + c, bcol + g)),
        pl.BlockSpec((rows, STATE), lambda g, c: (row0 + c, bcol + GROUPS + g)),
        pl.BlockSpec((rows, LANES), lambda g, c: (row0 + c, DT_COL // LANES)),
        pl.BlockSpec((None, SSD_CONV_WIDTH, XBC_W), lambda g, c: (g, 0, 0)),
        pl.BlockSpec((None, 1, XBC_W), lambda g, c: (g, 0, 0)),
        pl.BlockSpec((1, LANES), lambda g, c: (0, 0)),
        pl.BlockSpec((1, LANES), lambda g, c: (0, 0)),
        pl.BlockSpec((1, GROUP_W), lambda g, c: (0, g)),
        pl.BlockSpec((1, GROUP_W), lambda g, c: (0, g)),
        pl.BlockSpec((None, SPLIT_K, HEADS_PER_GROUP * CHUNK), lambda g, c: (g, 0, 0)),
        pl.BlockSpec((None, SPLIT_K, GROUP_W), lambda g, c: (g, 0, 0)),
    ]


def _spread_matrices():
    k = np.arange(SPLIT_K)
    head = k % HEADS
    used = (k // HEADS) < 3
    g = np.arange(GROUPS)[:, None, None]
    rj = np.arange(HEADS_PER_GROUP * CHUNK)[None, None, :] // CHUNK
    ej = np.arange(GROUP_W)[None, None, :] // HEADDIM
    hk = head[None, :, None]
    uk = used[None, :, None]
    r = (uk & (hk == g * HEADS_PER_GROUP + rj)).astype(np.float32)
    e = (uk & (hk == g * HEADS_PER_GROUP + ej)).astype(np.float32)
    return jnp.asarray(r, BF16), jnp.asarray(e, BF16)


def _per_group(a):
    lead = a.shape[:-1]
    x = a[..., :D_INNER].reshape(*lead, GROUPS, GROUP_W)
    b = a[..., D_INNER:D_INNER + GROUPS * STATE].reshape(*lead, GROUPS, STATE)
    c = a[..., D_INNER + GROUPS * STATE:].reshape(*lead, GROUPS, STATE)
    return jnp.moveaxis(jnp.concatenate([x, b, c], axis=-1), -2, 0)


def _ssd(zxd, conv_w, conv_b, dt_bias, a_log, d_skip, norm_w, conv_prev, state0):
    r_mat, e_mat = _spread_matrices()
    pad = ((0, 0), (0, LANES - HEADS))
    params = (_per_group(conv_w), _per_group(conv_b.reshape(1, CONV_DIM)),
              jnp.pad(dt_bias.reshape(1, HEADS).astype(F32), pad),
              jnp.pad(a_log.reshape(1, HEADS).astype(F32), pad),
              jnp.repeat(d_skip.astype(F32), HEADDIM).reshape(1, D_INNER),
              norm_w.reshape(1, D_INNER).astype(F32), r_mat, e_mat)
    common = (zxd,) * 5 + params
    y, s_fin = pl.pallas_call(
        _ssd_main_kernel,
        grid=(GROUPS, T_MAIN // SSD_RB),
        in_specs=_ssd_specs(SSD_RB, 0),
        out_specs=[
            pl.BlockSpec((SSD_RB, GROUP_W), lambda g, c: (c, g)),
            pl.BlockSpec((None, STATE, GROUP_W), lambda g, c: (g, 0, 0)),
        ],
        out_shape=[
            jax.ShapeDtypeStruct((T_ALL, D_INNER), BF16),
            jax.ShapeDtypeStruct((GROUPS, STATE, GROUP_W), F32),
        ],
        scratch_shapes=[
            pltpu.VMEM((CONV_HEAD + SSD_RB, XBC_W), F32),
            pltpu.VMEM((STATE, GROUP_W), F32),
            pltpu.VMEM((LANES, CHUNK), F32),
        ],
        compiler_params=_cparams(("parallel", "arbitrary")),
        name="ssd_prompt",
    )(*common)

    n_in = len(common)
    y, s_new = pl.pallas_call(
        _ssd_samp_kernel,
        grid=(GROUPS, N_SAMP_CHUNKS),
        in_specs=_ssd_specs(CHUNK, N_MAIN_CHUNKS) + [
            pl.BlockSpec((None, SEQ_PER_CHUNK, SSD_CONV_WIDTH - 1, XBC_W), lambda g, c: (g, c, 0, 0)),
            pl.BlockSpec((SEQ_PER_CHUNK, None, GROUP_W, STATE), lambda g, c: (c, g, 0, 0)),
            pl.BlockSpec(memory_space=pl.ANY),
        ],
        out_specs=[
            pl.BlockSpec((CHUNK, GROUP_W), lambda g, c: (N_MAIN_CHUNKS + c, g)),
            pl.BlockSpec((SEQ_PER_CHUNK, None, GROUP_W, STATE), lambda g, c: (c, g, 0, 0)),
        ],
        out_shape=[
            jax.ShapeDtypeStruct((T_ALL, D_INNER), BF16),
            jax.ShapeDtypeStruct((DEC_BATCH, GROUPS, GROUP_W, STATE), F32),
        ],
        scratch_shapes=[
            pltpu.VMEM((SEQ_PER_CHUNK * (CONV_HEAD + DEC_SEQ), XBC_W), F32),
            pltpu.VMEM((CHUNK, XBC_W), F32),
            pltpu.VMEM((CHUNK, GROUP_W), F32),
            pltpu.VMEM((LANES, CHUNK), F32),
            pltpu.VMEM((LANES, CHUNK), F32),
        ],
        input_output_aliases={n_in + 2: 0},
        compiler_params=_cparams(("parallel", "arbitrary")),
        name="ssd_sample",
    )(*common, _per_group(conv_prev), state0, y)
    return y, s_fin, s_new


def _tail_rows(a, n):
    c = a.shape[-1]
    p = a[T_PROMPT - n:T_PROMPT].reshape(1, 1, n, c)
    s = a[T_MAIN:].reshape(DEC_BATCH, DEC_SEQ, c)[:, DEC_SEQ - n:].reshape(1, DEC_BATCH, n, c)
    return p, s


def kernel(x_prompt, x_sample, cache_sc_conv, state_ssd_conv, state_ssd, meta_tokens, ln_g, ln_b,
           ffn_w1, ffn_w3, ffn_w2, sc_w_in, sc_w_conv, sc_w_out,
           ssd_w_in, ssd_w_conv, ssd_b_conv, ssd_dt_bias, ssd_a_log, ssd_d, ssd_norm_w, ssd_w_out):
    w1 = ffn_w1.astype(BF16)
    w3 = ffn_w3.astype(BF16)
    w2 = ffn_w2.astype(BF16)
    sc_in_w = sc_w_in.astype(BF16)
    sc_out_w = sc_w_out.astype(BF16)
    ssd_in_w = jnp.pad(ssd_w_in[0], ((0, 0), (0, ZXD_W - ssd_w_in.shape[-1]))).astype(BF16)
    ssd_out_w = ssd_w_out.astype(BF16)
    lng = ln_g.reshape(DEPTH, 3, 1, D_MODEL)
    lnb = ln_b.reshape(DEPTH, 3, 1, D_MODEL)

    h = jnp.concatenate([meta_tokens.astype(F32), x_prompt[0], jnp.zeros((N_PAD, D_MODEL), F32),
                         x_sample.reshape(N_SAMP, D_MODEL)], axis=0)

    h = _ffn(h, w1, w3, w2, lng, lnb, 0, 0, 0)
    bgate, u = _sc_in(h, sc_in_w, 0)
    v = _conv(u, 0, D_MODEL, cache_sc_conv[0], sc_w_conv[0], None, bgate, False, BF16)
    new_sc_p, new_sc_s = _tail_rows(u, SC_WIDTH - 1)
    h = _proj_ln(v, sc_out_w, 0, h, lng, lnb, 0, 1)
    h = _ffn(h, w1, w3, w2, lng, lnb, 0, 1, 2)

    h = _ffn(h, w1, w3, w2, lng, lnb, 1, 0, 0)
    zxd = _mm(h, ssd_in_w, ZXD_TN)
    new_conv_p, new_conv_s = _tail_rows(zxd[:, D_INNER:DT_COL], SSD_CONV_WIDTH - 1)
    state0 = state_ssd[0].reshape(DEC_BATCH, GROUPS, GROUP_W, STATE)
    y, s_fin, s_new = _ssd(zxd, ssd_w_conv[0], ssd_b_conv[0], ssd_dt_bias[0], ssd_a_log[0], ssd_d[0],
                           ssd_norm_w[0], state_ssd_conv[0], state0)
    h = _proj_ln(y, ssd_out_w, 0, h, lng, lnb, 1, 1)
    h = _ffn(h, w1, w3, w2, lng, lnb, 1, 1, 2)

    y_prompt = h[N_META:T_PROMPT].reshape(1, SEQ, D_MODEL)
    y_sample = h[T_MAIN:].reshape(DEC_BATCH, DEC_SEQ, D_MODEL)
    new_ssd_p = jnp.swapaxes(s_fin, 1, 2).reshape(1, 1, HEADS, HEADDIM, STATE).astype(state_ssd.dtype)
    new_ssd_s = s_new.reshape(1, DEC_BATCH, HEADS, HEADDIM, STATE).astype(state_ssd.dtype)
    return (y_prompt, y_sample, new_sc_p, new_sc_s, new_conv_p, new_conv_s, new_ssd_p, new_ssd_s)
```

```python
import functools

import numpy as np

import jax
import jax.numpy as jnp
from jax import lax
from jax.experimental import pallas as pl
from jax.experimental.pallas import tpu as pltpu

F32 = jnp.float32
BF16 = jnp.bfloat16

D_MODEL = 2048
SEQ = 8192
DEPTH = 2
DEC_BATCH = 32
DEC_SEQ = 16
N_META = 16
D_FF = 5632
SC_WIDTH = 3
D_INNER = 4096
HEADDIM = 64
HEADS = 64
GROUPS = 8
HEADS_PER_GROUP = 8
STATE = 128
SSD_CONV_WIDTH = 4
CONV_DIM = D_INNER + 2 * GROUPS * STATE
GROUP_W = HEADS_PER_GROUP * HEADDIM
XBC_W = GROUP_W + 2 * STATE
ALPHA = (2.0 * DEPTH) ** 0.25
LN_EPS = 1e-5
RMS_EPS = 1e-5

CHUNK = 128
T_PROMPT = N_META + SEQ
N_MAIN_CHUNKS = -(-T_PROMPT // CHUNK)
T_MAIN = N_MAIN_CHUNKS * CHUNK
N_PAD = T_MAIN - T_PROMPT
N_SAMP = DEC_BATCH * DEC_SEQ
N_SAMP_CHUNKS = N_SAMP // CHUNK
SEQ_PER_CHUNK = CHUNK // DEC_SEQ
T_ALL = T_MAIN + N_SAMP

TM = 736
TF = 512
TF_HEAD = 256
FFN_ROW_SPLIT = 2
PROJ_ROW_SPLIT = 2
ZXD_W = 10368
ZXD_TN = 1152
DT_COL = D_INNER + CONV_DIM
SSD_NB = 5
SSD_RB = SSD_NB * CHUNK
LANES = 128
SUBLANES = 8

VMEM_LIMIT_BYTES = 56 * 1024 * 1024


def _cparams(sem):
    return pltpu.CompilerParams(dimension_semantics=sem, vmem_limit_bytes=VMEM_LIMIT_BYTES)


def _layer_norm(v, g, b):
    mu = jnp.mean(v, axis=-1, keepdims=True)
    c = v - mu
    var = jnp.mean(c * c, axis=-1, keepdims=True)
    return c * lax.rsqrt(var + LN_EPS) * g + b


def _silu(x):
    return x * jax.nn.sigmoid(x)


def _softplus(x):
    return jnp.maximum(x, 0.0) + jnp.log(1.0 + jnp.exp(-jnp.abs(x)))


def _ffn_kernel(*refs, nf, n_alias, emit_bf16, convert):
    x_ref, w1_ref, w3_ref, w2_ref, g_ref, b_ref = refs[:6]
    outs = refs[6 + n_alias:]
    o_ref = outs[0]
    pos = 1
    ob_ref = None
    if emit_bf16:
        ob_ref = outs[pos]
        pos += 1
    if convert:
        wb_refs = outs[pos:pos + 3]
        for src, dst in zip((w1_ref, w3_ref, w2_ref), wb_refs):
            dst[...] = src[...].astype(BF16)
        w1_ref, w3_ref, w2_ref = wb_refs
        pos += 3
    xb_ref = outs[pos]
    f = pl.program_id(1)

    @pl.when(f == 0)
    def _():
        xb_ref[...] = x_ref[...].astype(BF16)
        o_ref[...] = jnp.zeros_like(o_ref)

    rs = TM // FFN_ROW_SPLIT
    for s in range(FFN_ROW_SPLIT):
        rows = pl.ds(s * rs, rs)
        xb = xb_ref[rows, :]
        h1 = jnp.dot(xb, w1_ref[...], preferred_element_type=F32)
        h3 = jnp.dot(xb, w3_ref[...], preferred_element_type=F32)
        gate = (_silu(h1) * h3).astype(BF16)
        o_ref[rows, :] += jnp.dot(gate, w2_ref[...], preferred_element_type=F32)

    @pl.when(f == nf - 1)
    def _():
        v = ALPHA * x_ref[...] + 0.5 * o_ref[...]
        res = _layer_norm(v, g_ref[...], b_ref[...])
        o_ref[...] = res
        if ob_ref is not None:
            ob_ref[...] = res.astype(BF16)


def _ffn(h, w1, w3, w2, lng, lnb, i, j, k, emit_bf16=False):
    ln_spec = pl.BlockSpec((None, None, 1, D_MODEL), lambda m, f: (i, k, 0, 0))
    h_shapes = [jax.ShapeDtypeStruct((T_ALL, D_MODEL), F32)]
    if emit_bf16:
        h_shapes.append(jax.ShapeDtypeStruct((T_ALL, D_MODEL), BF16))
    n_h = len(h_shapes)
    scratch = [pltpu.VMEM((TM, D_MODEL), BF16)]

    head_row = pl.BlockSpec((TM, D_MODEL), lambda m, f: (0, 0))
    nf_head = D_FF // TF_HEAD
    head = pl.pallas_call(
        functools.partial(_ffn_kernel, nf=nf_head, n_alias=0, emit_bf16=emit_bf16, convert=True),
        grid=(1, nf_head),
        in_specs=[
            head_row,
            pl.BlockSpec((None, None, D_MODEL, TF_HEAD), lambda m, f: (i, j, 0, f)),
            pl.BlockSpec((None, None, D_MODEL, TF_HEAD), lambda m, f: (i, j, 0, f)),
            pl.BlockSpec((None, None, TF_HEAD, D_MODEL), lambda m, f: (i, j, f, 0)),
            ln_spec, ln_spec,
        ],
        out_specs=[head_row] * n_h + [
            pl.BlockSpec((D_MODEL, TF_HEAD), lambda m, f: (0, f)),
            pl.BlockSpec((D_MODEL, TF_HEAD), lambda m, f: (0, f)),
            pl.BlockSpec((TF_HEAD, D_MODEL), lambda m, f: (f, 0)),
        ],
        out_shape=h_shapes + [
            jax.ShapeDtypeStruct((D_MODEL, D_FF), BF16),
            jax.ShapeDtypeStruct((D_MODEL, D_FF), BF16),
            jax.ShapeDtypeStruct((D_FF, D_MODEL), BF16),
        ],
        scratch_shapes=scratch,
        compiler_params=_cparams(("parallel", "arbitrary")),
        name="ffn_ln_head",
    )(h, w1, w3, w2, lng, lnb)
    h_parts, (w1b, w3b, w2b) = head[:n_h], head[n_h:]

    nf = D_FF // TF
    row = pl.BlockSpec((TM, D_MODEL), lambda m, f: (m + 1, 0))
    out = pl.pallas_call(
        functools.partial(_ffn_kernel, nf=nf, n_alias=n_h, emit_bf16=emit_bf16, convert=False),
        grid=(T_ALL // TM - 1, nf),
        in_specs=[
            row,
            pl.BlockSpec((D_MODEL, TF), lambda m, f: (0, f)),
            pl.BlockSpec((D_MODEL, TF), lambda m, f: (0, f)),
            pl.BlockSpec((TF, D_MODEL), lambda m, f: (f, 0)),
            ln_spec, ln_spec,
        ] + [pl.BlockSpec(memory_space=pl.ANY)] * n_h,
        out_specs=[row] * n_h,
        out_shape=h_shapes,
        scratch_shapes=scratch,
        input_output_aliases={6 + a: a for a in range(n_h)},
        compiler_params=_cparams(("parallel", "arbitrary")),
        name="ffn_ln",
    )(h, w1b, w3b, w2b, lng, lnb, *h_parts)
    return tuple(out) if emit_bf16 else out[0]


def _mm_kernel(x_ref, wt_ref, o_ref, wb_ref):
    @pl.when(pl.program_id(1) == 0)
    def _():
        wb_ref[...] = wt_ref[...].astype(BF16)

    o_ref[...] = lax.dot_general(x_ref[...], wb_ref[...], (((1,), (1,)), ((), ())),
                                 preferred_element_type=F32)


def _mm(xb, wt, n_out, tn):
    k = wt.shape[1]
    return pl.pallas_call(
        _mm_kernel,
        grid=(n_out // tn, T_ALL // TM),
        in_specs=[
            pl.BlockSpec((TM, k), lambda j, m: (m, 0)),
            pl.BlockSpec((tn, k), lambda j, m: (j, 0)),
        ],
        out_specs=pl.BlockSpec((TM, tn), lambda j, m: (m, j)),
        out_shape=jax.ShapeDtypeStruct((T_ALL, n_out), F32),
        scratch_shapes=[pltpu.VMEM((tn, k), BF16)],
        compiler_params=_cparams(("parallel", "arbitrary")),
        name="in_proj",
    )(xb, wt)


def _sc_in_kernel(x_ref, wb_ref, wc_ref, wh_ref, bg_ref, u_ref, xb_ref):
    @pl.when(pl.program_id(1) == 0)
    def _():
        xb_ref[...] = x_ref[...].astype(BF16)

    xb = xb_ref[...]
    bg_ref[...] = jnp.dot(xb, wb_ref[...], preferred_element_type=F32)
    c = jnp.dot(xb, wc_ref[...], preferred_element_type=F32)
    hh = jnp.dot(xb, wh_ref[...], preferred_element_type=F32)
    u_ref[...] = c * hh


def _sc_in(h, w_in, j):
    tn = 512
    nb = D_MODEL // tn
    row = pl.BlockSpec((TM, tn), lambda m, n: (m, n))
    return pl.pallas_call(
        _sc_in_kernel,
        grid=(T_ALL // TM, nb),
        in_specs=[
            pl.BlockSpec((TM, D_MODEL), lambda m, n: (m, 0)),
            pl.BlockSpec((None, D_MODEL, tn), lambda m, n: (j, 0, n)),
            pl.BlockSpec((None, D_MODEL, tn), lambda m, n: (j, 0, n + nb)),
            pl.BlockSpec((None, D_MODEL, tn), lambda m, n: (j, 0, n + 2 * nb)),
        ],
        out_specs=[row, row],
        out_shape=[jax.ShapeDtypeStruct((T_ALL, D_MODEL), F32)] * 2,
        scratch_shapes=[pltpu.VMEM((TM, D_MODEL), BF16)],
        compiler_params=_cparams(("parallel", "arbitrary")),
        name="sc_in_proj",
    )(h, w_in, w_in, w_in)


CONV_TILE = 128
CONV_CB = 2048
HALO = 8


def _conv_kernel(*refs, taps, has_bias, has_gate, act):
    u_ref, halo_ref = refs[0], refs[1]
    ov_refs = refs[2:2 + taps - 1]
    w_ref = refs[2 + taps - 1]
    pos = 3 + taps - 1
    bias_ref = gate_ref = None
    if has_bias:
        bias_ref = refs[pos]
        pos += 1
    if has_gate:
        gate_ref = refs[pos]
        pos += 1
    o_ref = refs[pos]

    i = pl.program_id(0)
    u = u_ref[...]
    halo = jnp.where(i > 0, halo_ref[...], 0.0)
    lrow = lax.broadcasted_iota(jnp.int32, (CONV_TILE, 1), 0)
    srow = lrow + i * CONV_TILE - T_MAIN
    spos = jnp.bitwise_and(srow, DEC_SEQ - 1)
    in_samp = srow >= 0
    w = w_ref[...]
    acc = w[taps - 1:taps, :] * u
    for d in range(1, taps):
        ud = pltpu.roll(u, d, axis=0)
        for r in range(d):
            ud = jnp.where(lrow == r, halo[HALO - d + r:HALO - d + r + 1, :], ud)
        ud = jnp.where(jnp.logical_and(in_samp, spos < d), ov_refs[d - 1][...], ud)
        acc = acc + w[taps - 1 - d:taps - d, :] * ud
    if has_bias:
        acc = acc + bias_ref[...]
    if act:
        acc = _silu(acc)
    if has_gate:
        acc = acc * gate_ref[...]
    o_ref[...] = acc.astype(o_ref.dtype)


def _conv_overrides(prev, taps):
    c = prev.shape[-1]
    out = []
    for d in range(1, taps):
        o = jnp.zeros((DEC_BATCH, DEC_SEQ, c), F32)
        for p in range(d):
            o = o.at[:, p].set(prev[:, taps - 1 + p - d])
        out.append(o.reshape(N_SAMP, c))
    return out


def _conv(u, col_off, n_ch, prev, w, bias, gate, act, out_dtype):
    taps = w.shape[0]
    ncb = n_ch // CONV_CB
    cb0 = col_off // CONV_CB
    tiles_per_halo = CONV_TILE // HALO
    samp_tile0 = T_MAIN // CONV_TILE
    ovs = _conv_overrides(prev, taps)
    in_specs = [
        pl.BlockSpec((CONV_TILE, CONV_CB), lambda i, c: (i, cb0 + c)),
        pl.BlockSpec((HALO, CONV_CB), lambda i, c: (jnp.maximum(i * tiles_per_halo - 1, 0), cb0 + c)),
    ]
    args = [u, u]
    for o in ovs:
        in_specs.append(pl.BlockSpec((CONV_TILE, CONV_CB), lambda i, c: (jnp.maximum(i - samp_tile0, 0), c)))
        args.append(o)
    in_specs.append(pl.BlockSpec((taps, CONV_CB), lambda i, c: (0, c)))
    args.append(w)
    if bias is not None:
        in_specs.append(pl.BlockSpec((1, CONV_CB), lambda i, c: (0, c)))
        args.append(bias.reshape(1, n_ch))
    if gate is not None:
        in_specs.append(pl.BlockSpec((CONV_TILE, CONV_CB), lambda i, c: (i, c)))
        args.append(gate)
    return pl.pallas_call(
        functools.partial(_conv_kernel, taps=taps, has_bias=bias is not None,
                          has_gate=gate is not None, act=act),
        grid=(T_ALL // CONV_TILE, ncb),
        in_specs=in_specs,
        out_specs=pl.BlockSpec((CONV_TILE, CONV_CB), lambda i, c: (i, c)),
        out_shape=jax.ShapeDtypeStruct((T_ALL, n_ch), out_dtype),
        compiler_params=_cparams(("parallel", "parallel")),
        name="causal_conv",
    )(*args)


def _proj_ln_kernel(y_ref, w_ref, h_ref, g_ref, b_ref, o_ref, *, nk):
    rs = TM // PROJ_ROW_SPLIT

    def slab_dot(rows):
        return jnp.dot(y_ref[rows, :], w_ref[...], preferred_element_type=F32)

    def finish(rows, acc):
        v = ALPHA * h_ref[rows, :] + acc
        o_ref[rows, :] = _layer_norm(v, g_ref[...], b_ref[...])

    slabs = [pl.ds(s * rs, rs) for s in range(PROJ_ROW_SPLIT)]
    if nk == 1:
        for rows in slabs:
            finish(rows, slab_dot(rows))
        return

    k = pl.program_id(1)

    @pl.when(k == 0)
    def _():
        for rows in slabs:
            o_ref[rows, :] = slab_dot(rows)

    @pl.when(k == 1)
    def _():
        for rows in slabs:
            finish(rows, o_ref[rows, :] + slab_dot(rows))


def _proj_ln(y, w, j, h, lng, lnb, i, k):
    kdim = w.shape[1]
    tk = 2048
    nk = kdim // tk
    assert nk in (1, 2)
    return pl.pallas_call(
        functools.partial(_proj_ln_kernel, nk=nk),
        grid=(T_ALL // TM, nk),
        in_specs=[
            pl.BlockSpec((TM, tk), lambda m, kk: (m, kk)),
            pl.BlockSpec((None, tk, D_MODEL), lambda m, kk: (j, kk, 0)),
            pl.BlockSpec((TM, D_MODEL), lambda m, kk: (m, 0)),
            pl.BlockSpec((None, None, 1, D_MODEL), lambda m, kk: (i, k, 0, 0)),
            pl.BlockSpec((None, None, 1, D_MODEL), lambda m, kk: (i, k, 0, 0)),
        ],
        out_specs=pl.BlockSpec((TM, D_MODEL), lambda m, kk: (m, 0)),
        out_shape=jax.ShapeDtypeStruct((T_ALL, D_MODEL), F32),
        compiler_params=_cparams(("parallel", "arbitrary")),
        name="out_proj_ln",
    )(y, w, h, lng, lnb)


_NT = (((1,), (1,)), ((), ()))
_TN = (((0,), (0,)), ((), ()))
SPLIT_K = 2 * LANES
CONV_HEAD = SUBLANES


def _bf16_pieces(v):
    hi = v.astype(BF16)
    r1 = v - hi.astype(F32)
    mid = r1.astype(BF16)
    lo = (r1 - mid.astype(F32)).astype(BF16)
    return hi, mid, lo


def _split_pack(v, lane_lo):
    hi, mid, lo = _bf16_pieces(v)
    a = jnp.where(lane_lo, hi.astype(F32), pltpu.roll(mid.astype(F32), HEADS, axis=1))
    b = jnp.where(lane_lo, lo.astype(F32), 0.0)
    return jnp.concatenate([a, b], axis=1).astype(BF16)


def _split_pack2(v, lane_lo):
    hi = v.astype(BF16)
    mid = (v - hi.astype(F32)).astype(BF16)
    return jnp.where(lane_lo, hi.astype(F32), pltpu.roll(mid.astype(F32), HEADS, axis=1)).astype(BF16)


def _masked_cumsum(mask_bf, da):
    hi, mid, lo = _bf16_pieces(da)
    p = jnp.dot(mask_bf, jnp.concatenate([hi, mid, lo], axis=1), preferred_element_type=F32)
    return p[:, :LANES] + p[:, LANES:2 * LANES] + p[:, 2 * LANES:]


def _conv_silu(buf_ref, base, n, cw, cbias):
    taps = SSD_CONV_WIDTH
    acc = cw[taps - 1:taps, :] * buf_ref[base:base + n, :]
    for d in range(1, taps):
        acc = acc + cw[taps - 1 - d:taps - d, :] * buf_ref[base - d:base - d + n, :]
    return _silu(acc + cbias)


def _intra_chunk(xbc, dt, acs, te, arow, causal, lane_lo, r_ref, e_ref):
    xs = xbc[:, :GROUP_W]
    bmb = xbc[:, GROUP_W:GROUP_W + STATE].astype(BF16)
    cmb = xbc[:, GROUP_W + STATE:].astype(BF16)
    colmat = jnp.dot(_split_pack(acs, lane_lo), r_ref[...], preferred_element_type=F32)
    packed = jnp.concatenate([_split_pack2(dt, lane_lo), _split_pack2(te, lane_lo),
                              _split_pack2(jnp.exp(acs), lane_lo)], axis=0)
    ex = jnp.dot(packed, e_ref[...], preferred_element_type=F32)
    dtx, tex, eax = ex[:CHUNK], ex[CHUNK:2 * CHUNK], ex[2 * CHUNK:]
    xdt = xs * dtx
    cb = lax.dot_general(cmb, bmb, _NT, preferred_element_type=F32)
    ys = []
    for q in range(HEADS_PER_GROUP // 2):
        scs = []
        for r in (2 * q, 2 * q + 1):
            seg = colmat[:, r * CHUNK:(r + 1) * CHUNK] - arow[r:r + 1, :]
            scs.append((cb * jnp.exp(jnp.where(causal, seg, -jnp.inf))).astype(BF16))
        xp = xdt[:, q * LANES:(q + 1) * LANES]
        rhs = jnp.concatenate([jnp.where(lane_lo, xp, 0.0).astype(BF16),
                               jnp.where(lane_lo, 0.0, xp).astype(BF16)], axis=0)
        ys.append(jnp.dot(jnp.concatenate(scs, axis=1), rhs, preferred_element_type=F32))
    y_intra = jnp.concatenate(ys, axis=1)
    xw = (xdt * tex).astype(BF16)
    return xs, bmb, cmb, y_intra, eax, xw


def _gate_norm(y, z, nw):
    y = y * _silu(z)
    ms = jnp.mean(y * y, axis=-1, keepdims=True)
    return (y * lax.rsqrt(ms + RMS_EPS) * nw).astype(BF16)


def _chunk_iotas():
    li = lax.broadcasted_iota(jnp.int32, (CHUNK, CHUNK), 0)
    si = lax.broadcasted_iota(jnp.int32, (CHUNK, CHUNK), 1)
    return li, si


def _ssd_main_kernel(z_ref, xr_ref, br_ref, cr_ref, dt_ref, cw_ref, cbias_ref, dtb_ref, al_ref, dx_ref, nw_ref,
                     r_ref, e_ref, y_ref, sfin_ref, buf_ref, st_ref, acst_ref):
    g = pl.program_id(0)
    rb = pl.program_id(1)

    @pl.when(rb == 0)
    def _():
        st_ref[...] = jnp.zeros_like(st_ref)
        buf_ref[0:CONV_HEAD, :] = jnp.zeros((CONV_HEAD, XBC_W), F32)

    @pl.when(rb > 0)
    def _():
        buf_ref[0:CONV_HEAD, :] = buf_ref[SSD_RB:SSD_RB + CONV_HEAD, :]

    buf_ref[CONV_HEAD:, 0:GROUP_W] = xr_ref[...]
    buf_ref[CONV_HEAD:, GROUP_W:GROUP_W + STATE] = br_ref[...]
    buf_ref[CONV_HEAD:, GROUP_W + STATE:] = cr_ref[...]

    li, si = _chunk_iotas()
    causal = si <= li
    causal_bf = causal.astype(F32).astype(BF16)
    lane_lo = si < HEADS
    row_iota = lax.broadcasted_iota(jnp.int32, (CHUNK, 1), 0)
    g8 = pl.multiple_of(g * HEADS_PER_GROUP, SUBLANES)
    cw = cw_ref[...]
    cbias = cbias_ref[...]
    a_row = -jnp.exp(al_ref[...])

    for k in range(SSD_NB):
        rows = pl.ds(k * CHUNK, CHUNK)
        xbc = _conv_silu(buf_ref, CONV_HEAD + k * CHUNK, CHUNK, cw, cbias)
        valid = row_iota + (rb * SSD_RB + k * CHUNK) < T_PROMPT
        dt = jnp.where(jnp.logical_and(valid, lane_lo), _softplus(dt_ref[rows, :] + dtb_ref[...]), 0.0)
        acs = _masked_cumsum(causal_bf, dt * a_row)
        te = jnp.exp(acs[CHUNK - 1:CHUNK, :] - acs)
        acst_ref[...] = acs.T
        arow = acst_ref[pl.ds(g8, HEADS_PER_GROUP), :]
        xs, bmb, cmb, y_intra, eax, xw = _intra_chunk(xbc, dt, acs, te, arow, causal, lane_lo, r_ref, e_ref)
        st = st_ref[...]
        y = y_intra + jnp.dot(cmb, st.astype(BF16), preferred_element_type=F32) * eax + dx_ref[...] * xs
        y_ref[rows, :] = _gate_norm(y, z_ref[rows, :], nw_ref[...])
        st_ref[...] = st * eax[CHUNK - 1:CHUNK, :] + lax.dot_general(bmb, xw, _TN, preferred_element_type=F32)

    @pl.when(rb == T_MAIN // SSD_RB - 1)
    def _():
        sfin_ref[...] = st_ref[...]


def _ssd_samp_kernel(z_ref, xr_ref, br_ref, cr_ref, dt_ref, cw_ref, cbias_ref, dtb_ref, al_ref, dx_ref, nw_ref,
                     r_ref, e_ref, prev_ref, s0_ref, yprev_ref, y_ref, s1_ref,
                     buf_ref, xbc_ref, yint_ref, acst_ref, tott_ref):
    del yprev_ref
    g = pl.program_id(0)
    slot = CONV_HEAD + DEC_SEQ
    cw = cw_ref[...]
    cbias = cbias_ref[...]
    for q in range(SEQ_PER_CHUNK):
        r0 = q * DEC_SEQ
        top = q * slot + CONV_HEAD
        buf_ref[top - (SSD_CONV_WIDTH - 1):top, :] = prev_ref[q]
        buf_ref[top:top + DEC_SEQ, 0:GROUP_W] = xr_ref[r0:r0 + DEC_SEQ, :]
        buf_ref[top:top + DEC_SEQ, GROUP_W:GROUP_W + STATE] = br_ref[r0:r0 + DEC_SEQ, :]
        buf_ref[top:top + DEC_SEQ, GROUP_W + STATE:] = cr_ref[r0:r0 + DEC_SEQ, :]
    for q in range(SEQ_PER_CHUNK):
        r0 = q * DEC_SEQ
        xbc_ref[r0:r0 + DEC_SEQ, :] = _conv_silu(buf_ref, q * slot + CONV_HEAD, DEC_SEQ, cw, cbias)

    li, si = _chunk_iotas()
    same = (li // DEC_SEQ) == (si // DEC_SEQ)
    causal = jnp.logical_and(si <= li, same)
    lane_lo = si < HEADS
    g8 = pl.multiple_of(g * HEADS_PER_GROUP, SUBLANES)

    dt = jnp.where(lane_lo, _softplus(dt_ref[...] + dtb_ref[...]), 0.0)
    da = dt * (-jnp.exp(al_ref[...]))
    acs = _masked_cumsum(causal.astype(F32).astype(BF16), da)
    tot = _masked_cumsum(same.astype(F32).astype(BF16), da)
    te = jnp.exp(tot - acs)
    acst_ref[...] = acs.T
    tott_ref[...] = tot.T
    arow = acst_ref[pl.ds(g8, HEADS_PER_GROUP), :]
    trow = tott_ref[pl.ds(g8, HEADS_PER_GROUP), :]
    xs, bmb, cmb, y_intra, eax, xw = _intra_chunk(xbc_ref[...], dt, acs, te, arow, causal, lane_lo, r_ref, e_ref)
    for q in range(SEQ_PER_CHUNK):
        r0 = q * DEC_SEQ
        yint_ref[r0:r0 + DEC_SEQ, :] = lax.dot_general(cmb[r0:r0 + DEC_SEQ, :], s0_ref[q].astype(BF16), _NT,
                                                       preferred_element_type=F32)
    y = y_intra + yint_ref[...] * eax + dx_ref[...] * xs
    y_ref[...] = _gate_norm(y, z_ref[...], nw_ref[...])
    for q in range(SEQ_PER_CHUNK):
        r0 = q * DEC_SEQ
        upd = lax.dot_general(xw[r0:r0 + DEC_SEQ, :], bmb[r0:r0 + DEC_SEQ, :], _TN, preferred_element_type=F32)
        for r in range(HEADS_PER_GROUP):
            lo, hi = r * HEADDIM, (r + 1) * HEADDIM
            s1_ref[q, lo:hi, :] = s0_ref[q, lo:hi, :] * jnp.exp(trow[r:r + 1, r0:r0 + 1]) + upd[lo:hi, :]


def _ssd_specs(rows, row0):
    xcol = D_INNER // GROUP_W
    bcol = (2 * D_INNER) // STATE
    return [
        pl.BlockSpec((rows, GROUP_W), lambda g, c: (row0 + c, g)),
        pl.BlockSpec((rows, GROUP_W), lambda g, c: (row0 + c, xcol + g)),
        pl.BlockSpec((rows, STATE), lambda g, c: (row0 + c, bcol + g)),
        pl.BlockSpec((rows, STATE), lambda g, c: (row0 + c, bcol + GROUPS + g)),
        pl.BlockSpec((rows, LANES), lambda g, c: (row0 + c, DT_COL // LANES)),
        pl.BlockSpec((None, SSD_CONV_WIDTH, XBC_W), lambda g, c: (g, 0, 0)),
        pl.BlockSpec((None, 1, XBC_W), lambda g, c: (g, 0, 0)),
        pl.BlockSpec((1, LANES), lambda g, c: (0, 0)),
        pl.BlockSpec((1, LANES), lambda g, c: (0, 0)),
        pl.BlockSpec((1, GROUP_W), lambda g, c: (0, g)),
        pl.BlockSpec((1, GROUP_W), lambda g, c: (0, g)),
        pl.BlockSpec((None, SPLIT_K, HEADS_PER_GROUP * CHUNK), lambda g, c: (g, 0, 0)),
        pl.BlockSpec((None, LANES, GROUP_W), lambda g, c: (g, 0, 0)),
    ]


def _spread_matrices():
    k = np.arange(SPLIT_K)
    head = k % HEADS
    used = (k // HEADS) < 3
    g = np.arange(GROUPS)[:, None, None]
    rj = np.arange(HEADS_PER_GROUP * CHUNK)[None, None, :] // CHUNK
    ej = np.arange(GROUP_W)[None, None, :] // HEADDIM
    hk = head[None, :, None]
    uk = used[None, :, None]
    r = (uk & (hk == g * HEADS_PER_GROUP + rj)).astype(np.float32)
    e = (hk == g * HEADS_PER_GROUP + ej)[:, :LANES].astype(np.float32)
    return jnp.asarray(r, BF16), jnp.asarray(e, BF16)


def _per_group(a):
    lead = a.shape[:-1]
    x = a[..., :D_INNER].reshape(*lead, GROUPS, GROUP_W)
    b = a[..., D_INNER:D_INNER + GROUPS * STATE].reshape(*lead, GROUPS, STATE)
    c = a[..., D_INNER + GROUPS * STATE:].reshape(*lead, GROUPS, STATE)
    return jnp.moveaxis(jnp.concatenate([x, b, c], axis=-1), -2, 0)


def _ssd(zxd, conv_w, conv_b, dt_bias, a_log, d_skip, norm_w, conv_prev, state0):
    r_mat, e_mat = _spread_matrices()
    pad = ((0, 0), (0, LANES - HEADS))
    params = (_per_group(conv_w), _per_group(conv_b.reshape(1, CONV_DIM)),
              jnp.pad(dt_bias.reshape(1, HEADS).astype(F32), pad),
              jnp.pad(a_log.reshape(1, HEADS).astype(F32), pad),
              jnp.repeat(d_skip.astype(F32), HEADDIM).reshape(1, D_INNER),
              norm_w.reshape(1, D_INNER).astype(F32), r_mat, e_mat)
    common = (zxd,) * 5 + params
    y, s_fin = pl.pallas_call(
        _ssd_main_kernel,
        grid=(GROUPS, T_MAIN // SSD_RB),
        in_specs=_ssd_specs(SSD_RB, 0),
        out_specs=[
            pl.BlockSpec((SSD_RB, GROUP_W), lambda g, c: (c, g)),
            pl.BlockSpec((None, STATE, GROUP_W), lambda g, c: (g, 0, 0)),
        ],
        out_shape=[
            jax.ShapeDtypeStruct((T_ALL, D_INNER), BF16),
            jax.ShapeDtypeStruct((GROUPS, STATE, GROUP_W), F32),
        ],
        scratch_shapes=[
            pltpu.VMEM((CONV_HEAD + SSD_RB, XBC_W), F32),
            pltpu.VMEM((STATE, GROUP_W), F32),
            pltpu.VMEM((LANES, CHUNK), F32),
        ],
        compiler_params=_cparams(("parallel", "arbitrary")),
        name="ssd_prompt",
    )(*common)

    n_in = len(common)
    y, s_new = pl.pallas_call(
        _ssd_samp_kernel,
        grid=(GROUPS, N_SAMP_CHUNKS),
        in_specs=_ssd_specs(CHUNK, N_MAIN_CHUNKS) + [
            pl.BlockSpec((None, SEQ_PER_CHUNK, SSD_CONV_WIDTH - 1, XBC_W), lambda g, c: (g, c, 0, 0)),
            pl.BlockSpec((SEQ_PER_CHUNK, None, GROUP_W, STATE), lambda g, c: (c, g, 0, 0)),
            pl.BlockSpec(memory_space=pl.ANY),
        ],
        out_specs=[
            pl.BlockSpec((CHUNK, GROUP_W), lambda g, c: (N_MAIN_CHUNKS + c, g)),
            pl.BlockSpec((SEQ_PER_CHUNK, None, GROUP_W, STATE), lambda g, c: (c, g, 0, 0)),
        ],
        out_shape=[
            jax.ShapeDtypeStruct((T_ALL, D_INNER), BF16),
            jax.ShapeDtypeStruct((DEC_BATCH, GROUPS, GROUP_W, STATE), F32),
        ],
        scratch_shapes=[
            pltpu.VMEM((SEQ_PER_CHUNK * (CONV_HEAD + DEC_SEQ), XBC_W), F32),
            pltpu.VMEM((CHUNK, XBC_W), F32),
            pltpu.VMEM((CHUNK, GROUP_W), F32),
            pltpu.VMEM((LANES, CHUNK), F32),
            pltpu.VMEM((LANES, CHUNK), F32),
        ],
        input_output_aliases={n_in + 2: 0},
        compiler_params=_cparams(("parallel", "arbitrary")),
        name="ssd_sample",
    )(*common, _per_group(conv_prev), state0, y)
    return y, s_fin, s_new


def _tail_rows(a, n):
    c = a.shape[-1]
    p = a[T_PROMPT - n:T_PROMPT].reshape(1, 1, n, c)
    s = a[T_MAIN:].reshape(DEC_BATCH, DEC_SEQ, c)[:, DEC_SEQ - n:].reshape(1, DEC_BATCH, n, c)
    return p, s


def kernel(x_prompt, x_sample, cache_sc_conv, state_ssd_conv, state_ssd, meta_tokens, ln_g, ln_b,
           ffn_w1, ffn_w3, ffn_w2, sc_w_in, sc_w_conv, sc_w_out,
           ssd_w_in, ssd_w_conv, ssd_b_conv, ssd_dt_bias, ssd_a_log, ssd_d, ssd_norm_w, ssd_w_out):
    w1, w3, w2 = ffn_w1, ffn_w3, ffn_w2
    sc_in_w = sc_w_in.astype(BF16)
    sc_out_w = sc_w_out.astype(BF16)
    ssd_out_w = ssd_w_out.astype(BF16)
    lng = ln_g.reshape(DEPTH, 3, 1, D_MODEL)
    lnb = ln_b.reshape(DEPTH, 3, 1, D_MODEL)

    h = jnp.concatenate([meta_tokens.astype(F32), x_prompt[0], jnp.zeros((N_PAD, D_MODEL), F32),
                         x_sample.reshape(N_SAMP, D_MODEL)], axis=0)

    h = _ffn(h, w1, w3, w2, lng, lnb, 0, 0, 0)
    bgate, u = _sc_in(h, sc_in_w, 0)
    v = _conv(u, 0, D_MODEL, cache_sc_conv[0], sc_w_conv[0], None, bgate, False, BF16)
    new_sc_p, new_sc_s = _tail_rows(u, SC_WIDTH - 1)
    h = _proj_ln(v, sc_out_w, 0, h, lng, lnb, 0, 1)
    h = _ffn(h, w1, w3, w2, lng, lnb, 0, 1, 2)

    h, hb = _ffn(h, w1, w3, w2, lng, lnb, 1, 0, 0, emit_bf16=True)
    zxd = _mm(hb, jnp.swapaxes(ssd_w_in[0], 0, 1), ZXD_W, ZXD_TN)
    new_conv_p, new_conv_s = _tail_rows(zxd[:, D_INNER:DT_COL], SSD_CONV_WIDTH - 1)
    state0 = state_ssd[0].reshape(DEC_BATCH, GROUPS, GROUP_W, STATE)
    y, s_fin, s_new = _ssd(zxd, ssd_w_conv[0], ssd_b_conv[0], ssd_dt_bias[0], ssd_a_log[0], ssd_d[0],
                           ssd_norm_w[0], state_ssd_conv[0], state0)
    h = _proj_ln(y, ssd_out_w, 0, h, lng, lnb, 1, 1)
    h = _ffn(h, w1, w3, w2, lng, lnb, 1, 1, 2)

    y_prompt = h[N_META:T_PROMPT].reshape(1, SEQ, D_MODEL)
    y_sample = h[T_MAIN:].reshape(DEC_BATCH, DEC_SEQ, D_MODEL)
    new_ssd_p = jnp.swapaxes(s_fin, 1, 2).reshape(1, 1, HEADS, HEADDIM, STATE).astype(state_ssd.dtype)
    new_ssd_s = s_new.reshape(1, DEC_BATCH, HEADS, HEADDIM, STATE).astype(state_ssd.dtype)
    return (y_prompt, y_sample, new_sc_p, new_sc_s, new_conv_p, new_conv_s, new_ssd_p, new_ssd_s)
```

```python
import functools

import numpy as np

import jax
import jax.numpy as jnp
from jax import lax
from jax.experimental import pallas as pl
from jax.experimental.pallas import tpu as pltpu

F32 = jnp.float32
BF16 = jnp.bfloat16

D_MODEL = 2048
SEQ = 8192
DEPTH = 2
DEC_BATCH = 32
DEC_SEQ = 16
N_META = 16
D_FF = 5632
SC_WIDTH = 3
D_INNER = 4096
HEADDIM = 64
HEADS = 64
GROUPS = 8
HEADS_PER_GROUP = 8
STATE = 128
SSD_CONV_WIDTH = 4
CONV_DIM = D_INNER + 2 * GROUPS * STATE
GROUP_W = HEADS_PER_GROUP * HEADDIM
XBC_W = GROUP_W + 2 * STATE
ALPHA = (2.0 * DEPTH) ** 0.25
LN_EPS = 1e-5
RMS_EPS = 1e-5

CHUNK = 128
T_PROMPT = N_META + SEQ
N_MAIN_CHUNKS = -(-T_PROMPT // CHUNK)
T_MAIN = N_MAIN_CHUNKS * CHUNK
N_PAD = T_MAIN - T_PROMPT
N_SAMP = DEC_BATCH * DEC_SEQ
N_SAMP_CHUNKS = N_SAMP // CHUNK
SEQ_PER_CHUNK = CHUNK // DEC_SEQ
T_ALL = T_MAIN + N_SAMP

TM = 736
TM_IN = 2 * TM
TF = 512
TF_HEAD = 256
FFN_ROW_SPLIT = 2
PROJ_ROW_SPLIT = 2
ZXD_W = 10368
ZXD_TN = 1152
DT_COL = D_INNER + CONV_DIM
SSD_NB = 5
SSD_RB = SSD_NB * CHUNK
LANES = 128
SUBLANES = 8
CONV_HEAD = SUBLANES

VMEM_LIMIT_BYTES = 56 * 1024 * 1024


def _cparams(sem):
    return pltpu.CompilerParams(dimension_semantics=sem, vmem_limit_bytes=VMEM_LIMIT_BYTES)


def _layer_norm(v, g, b):
    mu = jnp.mean(v, axis=-1, keepdims=True)
    c = v - mu
    var = jnp.mean(c * c, axis=-1, keepdims=True)
    return c * lax.rsqrt(var + LN_EPS) * g + b


def _silu(x):
    return x * jax.nn.sigmoid(x)


def _softplus(x):
    return jnp.maximum(x, 0.0) + jnp.log(1.0 + jnp.exp(-jnp.abs(x)))


def _ffn_kernel(*refs, nf, n_alias, emit_bf16, convert):
    x_ref, w1_ref, w3_ref, w2_ref, g_ref, b_ref = refs[:6]
    outs = refs[6 + n_alias:]
    o_ref = outs[0]
    pos = 1
    ob_ref = None
    if emit_bf16:
        ob_ref = outs[pos]
        pos += 1
    if convert:
        wb_refs = outs[pos:pos + 3]
        for src, dst in zip((w1_ref, w3_ref, w2_ref), wb_refs):
            dst[...] = src[...].astype(BF16)
        w1_ref, w3_ref, w2_ref = wb_refs
        pos += 3
    xb_ref = outs[pos]
    f = pl.program_id(1)

    @pl.when(f == 0)
    def _():
        xb_ref[...] = x_ref[...].astype(BF16)
        o_ref[...] = jnp.zeros_like(o_ref)

    rs = TM // FFN_ROW_SPLIT

    def slab(s, last):
        rows = pl.ds(s * rs, rs)
        xb = xb_ref[rows, :]
        h1 = jnp.dot(xb, w1_ref[...], preferred_element_type=F32)
        h3 = jnp.dot(xb, w3_ref[...], preferred_element_type=F32)
        gate = (_silu(h1) * h3).astype(BF16)
        acc = o_ref[rows, :] + jnp.dot(gate, w2_ref[...], preferred_element_type=F32)
        if last:
            res = _layer_norm(ALPHA * x_ref[rows, :] + 0.5 * acc, g_ref[...], b_ref[...])
            o_ref[rows, :] = res
            if ob_ref is not None:
                ob_ref[rows, :] = res.astype(BF16)
        else:
            o_ref[rows, :] = acc

    @pl.when(f < nf - 1)
    def _():
        for s in range(FFN_ROW_SPLIT):
            slab(s, False)

    @pl.when(f == nf - 1)
    def _():
        for s in range(FFN_ROW_SPLIT):
            slab(s, True)


def _ffn(h, w1, w3, w2, lng, lnb, i, j, k, emit_bf16=False):
    ln_spec = pl.BlockSpec((None, None, 1, D_MODEL), lambda m, f: (i, k, 0, 0))
    h_shapes = [jax.ShapeDtypeStruct((T_ALL, D_MODEL), F32)]
    if emit_bf16:
        h_shapes.append(jax.ShapeDtypeStruct((T_ALL, D_MODEL), BF16))
    n_h = len(h_shapes)
    scratch = [pltpu.VMEM((TM, D_MODEL), BF16)]

    head_row = pl.BlockSpec((TM, D_MODEL), lambda m, f: (0, 0))
    nf_head = D_FF // TF_HEAD
    head = pl.pallas_call(
        functools.partial(_ffn_kernel, nf=nf_head, n_alias=0, emit_bf16=emit_bf16, convert=True),
        grid=(1, nf_head),
        in_specs=[
            head_row,
            pl.BlockSpec((None, None, D_MODEL, TF_HEAD), lambda m, f: (i, j, 0, f)),
            pl.BlockSpec((None, None, D_MODEL, TF_HEAD), lambda m, f: (i, j, 0, f)),
            pl.BlockSpec((None, None, TF_HEAD, D_MODEL), lambda m, f: (i, j, f, 0)),
            ln_spec, ln_spec,
        ],
        out_specs=[head_row] * n_h + [
            pl.BlockSpec((D_MODEL, TF_HEAD), lambda m, f: (0, f)),
            pl.BlockSpec((D_MODEL, TF_HEAD), lambda m, f: (0, f)),
            pl.BlockSpec((TF_HEAD, D_MODEL), lambda m, f: (f, 0)),
        ],
        out_shape=h_shapes + [
            jax.ShapeDtypeStruct((D_MODEL, D_FF), BF16),
            jax.ShapeDtypeStruct((D_MODEL, D_FF), BF16),
            jax.ShapeDtypeStruct((D_FF, D_MODEL), BF16),
        ],
        scratch_shapes=scratch,
        compiler_params=_cparams(("parallel", "arbitrary")),
        name="ffn_ln_head",
    )(h, w1, w3, w2, lng, lnb)
    h_parts, (w1b, w3b, w2b) = head[:n_h], head[n_h:]

    nf = D_FF // TF
    row = pl.BlockSpec((TM, D_MODEL), lambda m, f: (m + 1, 0))
    out = pl.pallas_call(
        functools.partial(_ffn_kernel, nf=nf, n_alias=n_h, emit_bf16=emit_bf16, convert=False),
        grid=(T_ALL // TM - 1, nf),
        in_specs=[
            row,
            pl.BlockSpec((D_MODEL, TF), lambda m, f: (0, f)),
            pl.BlockSpec((D_MODEL, TF), lambda m, f: (0, f)),
            pl.BlockSpec((TF, D_MODEL), lambda m, f: (f, 0)),
            ln_spec, ln_spec,
        ] + [pl.BlockSpec(memory_space=pl.ANY)] * n_h,
        out_specs=[row] * n_h,
        out_shape=h_shapes,
        scratch_shapes=scratch,
        input_output_aliases={6 + a: a for a in range(n_h)},
        compiler_params=_cparams(("parallel", "arbitrary")),
        name="ffn_ln",
    )(h, w1b, w3b, w2b, lng, lnb, *h_parts)
    return tuple(out) if emit_bf16 else out[0]


def _mm_kernel(x_ref, wt_ref, o_ref, wb_ref):
    @pl.when(pl.program_id(1) == 0)
    def _():
        wb_ref[...] = wt_ref[...].astype(BF16)

    o_ref[...] = lax.dot_general(x_ref[...], wb_ref[...], (((1,), (1,)), ((), ())),
                                 preferred_element_type=F32)


def _mm(xb, wt, n_out, tn):
    k = wt.shape[1]
    return pl.pallas_call(
        _mm_kernel,
        grid=(n_out // tn, T_ALL // TM_IN),
        in_specs=[
            pl.BlockSpec((TM_IN, k), lambda j, m: (m, 0)),
            pl.BlockSpec((tn, k), lambda j, m: (j, 0)),
        ],
        out_specs=pl.BlockSpec((TM_IN, tn), lambda j, m: (m, j)),
        out_shape=jax.ShapeDtypeStruct((T_ALL, n_out), F32),
        scratch_shapes=[pltpu.VMEM((tn, k), BF16)],
        compiler_params=_cparams(("parallel", "arbitrary")),
        name="in_proj",
    )(xb, wt)


def _sc_in_kernel(x_ref, wb_ref, wc_ref, wh_ref, bg_ref, u_ref, wbb_ref, wcb_ref, whb_ref):
    @pl.when(pl.program_id(1) == 0)
    def _():
        wbb_ref[...] = wb_ref[...].astype(BF16)
        wcb_ref[...] = wc_ref[...].astype(BF16)
        whb_ref[...] = wh_ref[...].astype(BF16)

    xb = x_ref[...]
    bg_ref[...] = jnp.dot(xb, wbb_ref[...], preferred_element_type=F32)
    c = jnp.dot(xb, wcb_ref[...], preferred_element_type=F32)
    hh = jnp.dot(xb, whb_ref[...], preferred_element_type=F32)
    u_ref[...] = c * hh


def _sc_in(xb, w_in, j):
    tn = 512
    nb = D_MODEL // tn
    row = pl.BlockSpec((TM, tn), lambda n, m: (m, n))
    wscratch = pltpu.VMEM((D_MODEL, tn), BF16)
    return pl.pallas_call(
        _sc_in_kernel,
        grid=(nb, T_ALL // TM),
        in_specs=[
            pl.BlockSpec((TM, D_MODEL), lambda n, m: (m, 0)),
            pl.BlockSpec((None, D_MODEL, tn), lambda n, m: (j, 0, n)),
            pl.BlockSpec((None, D_MODEL, tn), lambda n, m: (j, 0, n + nb)),
            pl.BlockSpec((None, D_MODEL, tn), lambda n, m: (j, 0, n + 2 * nb)),
        ],
        out_specs=[row, row],
        out_shape=[jax.ShapeDtypeStruct((T_ALL, D_MODEL), F32)] * 2,
        scratch_shapes=[wscratch, wscratch, wscratch],
        compiler_params=_cparams(("parallel", "arbitrary")),
        name="sc_in_proj",
    )(xb, w_in, w_in, w_in)


CONV_TILE = 128
CONV_CB = 2048
HALO = 8


def _conv_kernel(*refs, taps, has_bias, has_gate, act):
    u_ref, halo_ref = refs[0], refs[1]
    ov_refs = refs[2:2 + taps - 1]
    w_ref = refs[2 + taps - 1]
    pos = 3 + taps - 1
    bias_ref = gate_ref = None
    if has_bias:
        bias_ref = refs[pos]
        pos += 1
    if has_gate:
        gate_ref = refs[pos]
        pos += 1
    o_ref = refs[pos]

    i = pl.program_id(0)
    first_samp_tile = T_MAIN // CONV_TILE

    def body(in_samples):
        u = u_ref[...]
        halo = jnp.where(i > 0, halo_ref[...], 0.0)
        lrow = lax.broadcasted_iota(jnp.int32, (CONV_TILE, 1), 0)
        spos = jnp.bitwise_and(lrow, DEC_SEQ - 1)
        w = w_ref[...]
        acc = w[taps - 1:taps, :] * u
        for d in range(1, taps):
            ud = pltpu.roll(u, d, axis=0)
            if in_samples:
                ud = jnp.where(spos < d, ov_refs[d - 1][...], ud)
            else:
                for r in range(d):
                    ud = jnp.where(lrow == r, halo[HALO - d + r:HALO - d + r + 1, :], ud)
            acc = acc + w[taps - 1 - d:taps - d, :] * ud
        if has_bias:
            acc = acc + bias_ref[...]
        if act:
            acc = _silu(acc)
        if has_gate:
            acc = acc * gate_ref[...]
        o_ref[...] = acc.astype(o_ref.dtype)

    @pl.when(i < first_samp_tile)
    def _():
        body(False)

    @pl.when(i >= first_samp_tile)
    def _():
        body(True)


def _conv_overrides(prev, taps):
    c = prev.shape[-1]
    out = []
    for d in range(1, taps):
        o = jnp.zeros((DEC_BATCH, DEC_SEQ, c), F32)
        for p in range(d):
            o = o.at[:, p].set(prev[:, taps - 1 + p - d])
        out.append(o.reshape(N_SAMP, c))
    return out


def _conv(u, col_off, n_ch, prev, w, bias, gate, act, out_dtype):
    taps = w.shape[0]
    ncb = n_ch // CONV_CB
    cb0 = col_off // CONV_CB
    tiles_per_halo = CONV_TILE // HALO
    samp_tile0 = T_MAIN // CONV_TILE
    ovs = _conv_overrides(prev, taps)
    in_specs = [
        pl.BlockSpec((CONV_TILE, CONV_CB), lambda i, c: (i, cb0 + c)),
        pl.BlockSpec((HALO, CONV_CB), lambda i, c: (jnp.maximum(i * tiles_per_halo - 1, 0), cb0 + c)),
    ]
    args = [u, u]
    for o in ovs:
        in_specs.append(pl.BlockSpec((CONV_TILE, CONV_CB), lambda i, c: (jnp.maximum(i - samp_tile0, 0), c)))
        args.append(o)
    in_specs.append(pl.BlockSpec((taps, CONV_CB), lambda i, c: (0, c)))
    args.append(w)
    if bias is not None:
        in_specs.append(pl.BlockSpec((1, CONV_CB), lambda i, c: (0, c)))
        args.append(bias.reshape(1, n_ch))
    if gate is not None:
        in_specs.append(pl.BlockSpec((CONV_TILE, CONV_CB), lambda i, c: (i, c)))
        args.append(gate)
    return pl.pallas_call(
        functools.partial(_conv_kernel, taps=taps, has_bias=bias is not None,
                          has_gate=gate is not None, act=act),
        grid=(T_ALL // CONV_TILE, ncb),
        in_specs=in_specs,
        out_specs=pl.BlockSpec((CONV_TILE, CONV_CB), lambda i, c: (i, c)),
        out_shape=jax.ShapeDtypeStruct((T_ALL, n_ch), out_dtype),
        compiler_params=_cparams(("parallel", "parallel")),
        name="causal_conv",
    )(*args)


def _proj_ln_kernel(y_ref, w_ref, h_ref, g_ref, b_ref, o_ref, *, nk):
    rs = TM // PROJ_ROW_SPLIT

    def slab_dot(rows):
        return jnp.dot(y_ref[rows, :], w_ref[...], preferred_element_type=F32)

    def finish(rows, acc):
        v = ALPHA * h_ref[rows, :] + acc
        o_ref[rows, :] = _layer_norm(v, g_ref[...], b_ref[...])

    slabs = [pl.ds(s * rs, rs) for s in range(PROJ_ROW_SPLIT)]
    if nk == 1:
        for rows in slabs:
            finish(rows, slab_dot(rows))
        return

    k = pl.program_id(1)

    @pl.when(k == 0)
    def _():
        for rows in slabs:
            o_ref[rows, :] = slab_dot(rows)

    @pl.when(k == 1)
    def _():
        for rows in slabs:
            finish(rows, o_ref[rows, :] + slab_dot(rows))


def _proj_ln(y, w, j, h, lng, lnb, i, k):
    kdim = w.shape[1]
    tk = 2048
    nk = kdim // tk
    assert nk in (1, 2)
    return pl.pallas_call(
        functools.partial(_proj_ln_kernel, nk=nk),
        grid=(T_ALL // TM, nk),
        in_specs=[
            pl.BlockSpec((TM, tk), lambda m, kk: (m, kk)),
            pl.BlockSpec((None, tk, D_MODEL), lambda m, kk: (j, kk, 0)),
            pl.BlockSpec((TM, D_MODEL), lambda m, kk: (m, 0)),
            pl.BlockSpec((None, None, 1, D_MODEL), lambda m, kk: (i, k, 0, 0)),
            pl.BlockSpec((None, None, 1, D_MODEL), lambda m, kk: (i, k, 0, 0)),
        ],
        out_specs=pl.BlockSpec((TM, D_MODEL), lambda m, kk: (m, 0)),
        out_shape=jax.ShapeDtypeStruct((T_ALL, D_MODEL), F32),
        compiler_params=_cparams(("parallel", "arbitrary")),
        name="out_proj_ln",
    )(y, w, h, lng, lnb)


_NT = (((1,), (1,)), ((), ()))
_TN = (((0,), (0,)), ((), ()))
SPLIT_K = 2 * LANES


def _bf16_pieces(v):
    hi = v.astype(BF16)
    r1 = v - hi.astype(F32)
    mid = r1.astype(BF16)
    lo = (r1 - mid.astype(F32)).astype(BF16)
    return hi, mid, lo


def _split_pack(v, lane_lo):
    hi, mid, lo = _bf16_pieces(v)
    a = jnp.where(lane_lo, hi.astype(F32), pltpu.roll(mid.astype(F32), HEADS, axis=1))
    b = jnp.where(lane_lo, lo.astype(F32), 0.0)
    return jnp.concatenate([a, b], axis=1).astype(BF16)


def _split_pack2(v, lane_lo):
    hi = v.astype(BF16)
    mid = (v - hi.astype(F32)).astype(BF16)
    return jnp.where(lane_lo, hi.astype(F32), pltpu.roll(mid.astype(F32), HEADS, axis=1)).astype(BF16)


def _masked_cumsum(mask_bf, da):
    hi, mid, lo = _bf16_pieces(da)
    p = jnp.dot(mask_bf, jnp.concatenate([hi, mid, lo], axis=1), preferred_element_type=F32)
    return p[:, :LANES] + p[:, LANES:2 * LANES] + p[:, 2 * LANES:]


def _conv_silu(buf_ref, base, n, cw, cbias):
    taps = SSD_CONV_WIDTH
    acc = cw[taps - 1:taps, :] * buf_ref[base:base + n, :]
    for d in range(1, taps):
        acc = acc + cw[taps - 1 - d:taps - d, :] * buf_ref[base - d:base - d + n, :]
    return _silu(acc + cbias)


def _intra_chunk(xbc, dt, acs, te, arow, causal, lane_lo, r_ref, e_ref):
    xs = xbc[:, :GROUP_W]
    bmb = xbc[:, GROUP_W:GROUP_W + STATE].astype(BF16)
    cmb = xbc[:, GROUP_W + STATE:].astype(BF16)
    colmat = jnp.dot(_split_pack(acs, lane_lo), r_ref[...], preferred_element_type=F32)
    packed = jnp.concatenate([_split_pack2(dt, lane_lo), _split_pack2(te, lane_lo),
                              _split_pack2(jnp.exp(acs), lane_lo)], axis=0)
    ex = jnp.dot(packed, e_ref[...], preferred_element_type=F32)
    dtx, tex, eax = ex[:CHUNK], ex[CHUNK:2 * CHUNK], ex[2 * CHUNK:]
    xdt = xs * dtx
    cb = lax.dot_general(cmb, bmb, _NT, preferred_element_type=F32)
    ys = []
    for q in range(HEADS_PER_GROUP // 2):
        scs = []
        for r in (2 * q, 2 * q + 1):
            seg = colmat[:, r * CHUNK:(r + 1) * CHUNK] - arow[r:r + 1, :]
            scs.append((cb * jnp.exp(jnp.where(causal, seg, -jnp.inf))).astype(BF16))
        xp = xdt[:, q * LANES:(q + 1) * LANES]
        rhs = jnp.concatenate([jnp.where(lane_lo, xp, 0.0).astype(BF16),
                               jnp.where(lane_lo, 0.0, xp).astype(BF16)], axis=0)
        ys.append(jnp.dot(jnp.concatenate(scs, axis=1), rhs, preferred_element_type=F32))
    y_intra = jnp.concatenate(ys, axis=1)
    xw = (xdt * tex).astype(BF16)
    return xs, bmb, cmb, y_intra, eax, xw


def _gate_norm(y, z, nw):
    y = y * _silu(z)
    ms = jnp.mean(y * y, axis=-1, keepdims=True)
    return (y * lax.rsqrt(ms + RMS_EPS) * nw).astype(BF16)


def _chunk_iotas():
    li = lax.broadcasted_iota(jnp.int32, (CHUNK, CHUNK), 0)
    si = lax.broadcasted_iota(jnp.int32, (CHUNK, CHUNK), 1)
    return li, si


def _ssd_main_kernel(z_ref, xr_ref, br_ref, cr_ref, dt_ref, cw_ref, cbias_ref, dtb_ref, al_ref, dx_ref, nw_ref,
                     r_ref, e_ref, y_ref, sfin_ref, buf_ref, st_ref, acst_ref):
    g = pl.program_id(0)
    rb = pl.program_id(1)

    @pl.when(rb == 0)
    def _():
        st_ref[...] = jnp.zeros_like(st_ref)
        buf_ref[0:CONV_HEAD, :] = jnp.zeros((CONV_HEAD, XBC_W), F32)

    @pl.when(rb > 0)
    def _():
        buf_ref[0:CONV_HEAD, :] = buf_ref[SSD_RB:SSD_RB + CONV_HEAD, :]

    buf_ref[CONV_HEAD:, 0:GROUP_W] = xr_ref[...]
    buf_ref[CONV_HEAD:, GROUP_W:GROUP_W + STATE] = br_ref[...]
    buf_ref[CONV_HEAD:, GROUP_W + STATE:] = cr_ref[...]

    li, si = _chunk_iotas()
    causal = si <= li
    causal_bf = causal.astype(F32).astype(BF16)
    lane_lo = si < HEADS
    row_iota = lax.broadcasted_iota(jnp.int32, (CHUNK, 1), 0)
    g8 = pl.multiple_of(g * HEADS_PER_GROUP, SUBLANES)
    cw = cw_ref[...]
    cbias = cbias_ref[...]
    a_row = -jnp.exp(al_ref[...])

    for k in range(SSD_NB):
        rows = pl.ds(k * CHUNK, CHUNK)
        xbc = _conv_silu(buf_ref, CONV_HEAD + k * CHUNK, CHUNK, cw, cbias)
        valid = row_iota + (rb * SSD_RB + k * CHUNK) < T_PROMPT
        dt = jnp.where(jnp.logical_and(valid, lane_lo), _softplus(dt_ref[rows, :] + dtb_ref[...]), 0.0)
        acs = _masked_cumsum(causal_bf, dt * a_row)
        te = jnp.exp(acs[CHUNK - 1:CHUNK, :] - acs)
        acst_ref[...] = acs.T
        arow = acst_ref[pl.ds(g8, HEADS_PER_GROUP), :]
        xs, bmb, cmb, y_intra, eax, xw = _intra_chunk(xbc, dt, acs, te, arow, causal, lane_lo, r_ref, e_ref)
        st = st_ref[...]
        y = y_intra + jnp.dot(cmb, st.astype(BF16), preferred_element_type=F32) * eax + dx_ref[...] * xs
        y_ref[rows, :] = _gate_norm(y, z_ref[rows, :], nw_ref[...])
        st_ref[...] = st * eax[CHUNK - 1:CHUNK, :] + lax.dot_general(bmb, xw, _TN, preferred_element_type=F32)

    @pl.when(rb == T_MAIN // SSD_RB - 1)
    def _():
        sfin_ref[...] = st_ref[...]


def _ssd_samp_kernel(z_ref, xr_ref, br_ref, cr_ref, dt_ref, cw_ref, cbias_ref, dtb_ref, al_ref, dx_ref, nw_ref,
                     r_ref, e_ref, prev_ref, s0_ref, yprev_ref, y_ref, s1_ref,
                     buf_ref, xbc_ref, yint_ref, acst_ref, tott_ref):
    del yprev_ref
    g = pl.program_id(0)
    slot = CONV_HEAD + DEC_SEQ
    cw = cw_ref[...]
    cbias = cbias_ref[...]
    for q in range(SEQ_PER_CHUNK):
        r0 = q * DEC_SEQ
        top = q * slot + CONV_HEAD
        buf_ref[top - (SSD_CONV_WIDTH - 1):top, :] = prev_ref[q]
        buf_ref[top:top + DEC_SEQ, 0:GROUP_W] = xr_ref[r0:r0 + DEC_SEQ, :]
        buf_ref[top:top + DEC_SEQ, GROUP_W:GROUP_W + STATE] = br_ref[r0:r0 + DEC_SEQ, :]
        buf_ref[top:top + DEC_SEQ, GROUP_W + STATE:] = cr_ref[r0:r0 + DEC_SEQ, :]
    for q in range(SEQ_PER_CHUNK):
        r0 = q * DEC_SEQ
        xbc_ref[r0:r0 + DEC_SEQ, :] = _conv_silu(buf_ref, q * slot + CONV_HEAD, DEC_SEQ, cw, cbias)

    li, si = _chunk_iotas()
    same = (li // DEC_SEQ) == (si // DEC_SEQ)
    causal = jnp.logical_and(si <= li, same)
    lane_lo = si < HEADS
    g8 = pl.multiple_of(g * HEADS_PER_GROUP, SUBLANES)

    dt = jnp.where(lane_lo, _softplus(dt_ref[...] + dtb_ref[...]), 0.0)
    da = dt * (-jnp.exp(al_ref[...]))
    acs = _masked_cumsum(causal.astype(F32).astype(BF16), da)
    tot = _masked_cumsum(same.astype(F32).astype(BF16), da)
    te = jnp.exp(tot - acs)
    acst_ref[...] = acs.T
    tott_ref[...] = tot.T
    arow = acst_ref[pl.ds(g8, HEADS_PER_GROUP), :]
    trow = tott_ref[pl.ds(g8, HEADS_PER_GROUP), :]
    xs, bmb, cmb, y_intra, eax, xw = _intra_chunk(xbc_ref[...], dt, acs, te, arow, causal, lane_lo, r_ref, e_ref)
    for q in range(SEQ_PER_CHUNK):
        r0 = q * DEC_SEQ
        yint_ref[r0:r0 + DEC_SEQ, :] = lax.dot_general(cmb[r0:r0 + DEC_SEQ, :], s0_ref[q].astype(BF16), _NT,
                                                       preferred_element_type=F32)
    y = y_intra + yint_ref[...] * eax + dx_ref[...] * xs
    y_ref[...] = _gate_norm(y, z_ref[...], nw_ref[...])
    for q in range(SEQ_PER_CHUNK):
        r0 = q * DEC_SEQ
        upd = lax.dot_general(xw[r0:r0 + DEC_SEQ, :], bmb[r0:r0 + DEC_SEQ, :], _TN, preferred_element_type=F32)
        for r in range(HEADS_PER_GROUP):
            lo, hi = r * HEADDIM, (r + 1) * HEADDIM
            s1_ref[q, lo:hi, :] = s0_ref[q, lo:hi, :] * jnp.exp(trow[r:r + 1, r0:r0 + 1]) + upd[lo:hi, :]


def _ssd_specs(rows, row0):
    xcol = D_INNER // GROUP_W
    bcol = (2 * D_INNER) // STATE
    return [
        pl.BlockSpec((rows, GROUP_W), lambda g, c: (row0 + c, g)),
        pl.BlockSpec((rows, GROUP_W), lambda g, c: (row0 + c, xcol + g)),
        pl.BlockSpec((rows, STATE), lambda g, c: (row0 + c, bcol + g)),
        pl.BlockSpec((rows, STATE), lambda g, c: (row0 + c, bcol + GROUPS + g)),
        pl.BlockSpec((rows, LANES), lambda g, c: (row0 + c, DT_COL // LANES)),
        pl.BlockSpec((None, SSD_CONV_WIDTH, XBC_W), lambda g, c: (g, 0, 0)),
        pl.BlockSpec((None, 1, XBC_W), lambda g, c: (g, 0, 0)),
        pl.BlockSpec((1, LANES), lambda g, c: (0, 0)),
        pl.BlockSpec((1, LANES), lambda g, c: (0, 0)),
        pl.BlockSpec((1, GROUP_W), lambda g, c: (0, g)),
        pl.BlockSpec((1, GROUP_W), lambda g, c: (0, g)),
        pl.BlockSpec((None, SPLIT_K, HEADS_PER_GROUP * CHUNK), lambda g, c: (g, 0, 0)),
        pl.BlockSpec((None, LANES, GROUP_W), lambda g, c: (g, 0, 0)),
    ]


def _spread_matrices():
    k = np.arange(SPLIT_K)
    head = k % HEADS
    used = (k // HEADS) < 3
    g = np.arange(GROUPS)[:, None, None]
    rj = np.arange(HEADS_PER_GROUP * CHUNK)[None, None, :] // CHUNK
    ej = np.arange(GROUP_W)[None, None, :] // HEADDIM
    hk = head[None, :, None]
    uk = used[None, :, None]
    r = (uk & (hk == g * HEADS_PER_GROUP + rj)).astype(np.float32)
    e = (hk == g * HEADS_PER_GROUP + ej)[:, :LANES].astype(np.float32)
    return jnp.asarray(r, BF16), jnp.asarray(e, BF16)


def _per_group(a):
    lead = a.shape[:-1]
    x = a[..., :D_INNER].reshape(*lead, GROUPS, GROUP_W)
    b = a[..., D_INNER:D_INNER + GROUPS * STATE].reshape(*lead, GROUPS, STATE)
    c = a[..., D_INNER + GROUPS * STATE:].reshape(*lead, GROUPS, STATE)
    return jnp.moveaxis(jnp.concatenate([x, b, c], axis=-1), -2, 0)


def _ssd(zxd, conv_w, conv_b, dt_bias, a_log, d_skip, norm_w, conv_prev, state0):
    r_mat, e_mat = _spread_matrices()
    pad = ((0, 0), (0, LANES - HEADS))
    params = (_per_group(conv_w), _per_group(conv_b.reshape(1, CONV_DIM)),
              jnp.pad(dt_bias.reshape(1, HEADS).astype(F32), pad),
              jnp.pad(a_log.reshape(1, HEADS).astype(F32), pad),
              jnp.repeat(d_skip.astype(F32), HEADDIM).reshape(1, D_INNER),
              norm_w.reshape(1, D_INNER).astype(F32), r_mat, e_mat)
    common = (zxd,) * 5 + params
    y, s_fin = pl.pallas_call(
        _ssd_main_kernel,
        grid=(GROUPS, T_MAIN // SSD_RB),
        in_specs=_ssd_specs(SSD_RB, 0),
        out_specs=[
            pl.BlockSpec((SSD_RB, GROUP_W), lambda g, c: (c, g)),
            pl.BlockSpec((None, STATE, GROUP_W), lambda g, c: (g, 0, 0)),
        ],
        out_shape=[
            jax.ShapeDtypeStruct((T_ALL, D_INNER), BF16),
            jax.ShapeDtypeStruct((GROUPS, STATE, GROUP_W), F32),
        ],
        scratch_shapes=[
            pltpu.VMEM((CONV_HEAD + SSD_RB, XBC_W), F32),
            pltpu.VMEM((STATE, GROUP_W), F32),
            pltpu.VMEM((LANES, CHUNK), F32),
        ],
        compiler_params=_cparams(("parallel", "arbitrary")),
        name="ssd_prompt",
    )(*common)

    n_in = len(common)
    y, s_new = pl.pallas_call(
        _ssd_samp_kernel,
        grid=(GROUPS, N_SAMP_CHUNKS),
        in_specs=_ssd_specs(CHUNK, N_MAIN_CHUNKS) + [
            pl.BlockSpec((None, SEQ_PER_CHUNK, SSD_CONV_WIDTH - 1, XBC_W), lambda g, c: (g, c, 0, 0)),
            pl.BlockSpec((SEQ_PER_CHUNK, None, GROUP_W, STATE), lambda g, c: (c, g, 0, 0)),
            pl.BlockSpec(memory_space=pl.ANY),
        ],
        out_specs=[
            pl.BlockSpec((CHUNK, GROUP_W), lambda g, c: (N_MAIN_CHUNKS + c, g)),
            pl.BlockSpec((SEQ_PER_CHUNK, None, GROUP_W, STATE), lambda g, c: (c, g, 0, 0)),
        ],
        out_shape=[
            jax.ShapeDtypeStruct((T_ALL, D_INNER), BF16),
            jax.ShapeDtypeStruct((DEC_BATCH, GROUPS, GROUP_W, STATE), F32),
        ],
        scratch_shapes=[
            pltpu.VMEM((SEQ_PER_CHUNK * (CONV_HEAD + DEC_SEQ), XBC_W), F32),
            pltpu.VMEM((CHUNK, XBC_W), F32),
            pltpu.VMEM((CHUNK, GROUP_W), F32),
            pltpu.VMEM((LANES, CHUNK), F32),
            pltpu.VMEM((LANES, CHUNK), F32),
        ],
        input_output_aliases={n_in + 2: 0},
        compiler_params=_cparams(("parallel", "arbitrary")),
        name="ssd_sample",
    )(*common, _per_group(conv_prev), state0, y)
    return y, s_fin, s_new


def _tail_rows(a, n):
    c = a.shape[-1]
    p = a[T_PROMPT - n:T_PROMPT].reshape(1, 1, n, c)
    s = a[T_MAIN:].reshape(DEC_BATCH, DEC_SEQ, c)[:, DEC_SEQ - n:].reshape(1, DEC_BATCH, n, c)
    return p, s


def kernel(x_prompt, x_sample, cache_sc_conv, state_ssd_conv, state_ssd, meta_tokens, ln_g, ln_b,
           ffn_w1, ffn_w3, ffn_w2, sc_w_in, sc_w_conv, sc_w_out,
           ssd_w_in, ssd_w_conv, ssd_b_conv, ssd_dt_bias, ssd_a_log, ssd_d, ssd_norm_w, ssd_w_out):
    w1, w3, w2 = ffn_w1, ffn_w3, ffn_w2
    sc_out_w = sc_w_out.astype(BF16)
    ssd_out_w = ssd_w_out.astype(BF16)
    lng = ln_g.reshape(DEPTH, 3, 1, D_MODEL)
    lnb = ln_b.reshape(DEPTH, 3, 1, D_MODEL)

    h = jnp.concatenate([meta_tokens.astype(F32), x_prompt[0], jnp.zeros((N_PAD, D_MODEL), F32),
                         x_sample.reshape(N_SAMP, D_MODEL)], axis=0)

    h, hb = _ffn(h, w1, w3, w2, lng, lnb, 0, 0, 0, emit_bf16=True)
    bgate, u = _sc_in(hb, sc_w_in, 0)
    v = _conv(u, 0, D_MODEL, cache_sc_conv[0], sc_w_conv[0], None, bgate, False, BF16)
    new_sc_p, new_sc_s = _tail_rows(u, SC_WIDTH - 1)
    h = _proj_ln(v, sc_out_w, 0, h, lng, lnb, 0, 1)
    h = _ffn(h, w1, w3, w2, lng, lnb, 0, 1, 2)

    h, hb = _ffn(h, w1, w3, w2, lng, lnb, 1, 0, 0, emit_bf16=True)
    zxd = _mm(hb, jnp.swapaxes(ssd_w_in[0], 0, 1), ZXD_W, ZXD_TN)
    new_conv_p, new_conv_s = _tail_rows(zxd[:, D_INNER:DT_COL], SSD_CONV_WIDTH - 1)
    state0 = state_ssd[0].reshape(DEC_BATCH, GROUPS, GROUP_W, STATE)
    y, s_fin, s_new = _ssd(zxd, ssd_w_conv[0], ssd_b_conv[0], ssd_dt_bias[0], ssd_a_log[0], ssd_d[0],
                           ssd_norm_w[0], state_ssd_conv[0], state0)
    h = _proj_ln(y, ssd_out_w, 0, h, lng, lnb, 1, 1)
    h = _ffn(h, w1, w3, w2, lng, lnb, 1, 1, 2)

    y_prompt = h[N_META:T_PROMPT].reshape(1, SEQ, D_MODEL)
    y_sample = h[T_MAIN:].reshape(DEC_BATCH, DEC_SEQ, D_MODEL)
    new_ssd_p = jnp.swapaxes(s_fin, 1, 2).reshape(1, 1, HEADS, HEADDIM, STATE).astype(state_ssd.dtype)
    new_ssd_s = s_new.reshape(1, DEC_BATCH, HEADS, HEADDIM, STATE).astype(state_ssd.dtype)
    return (y_prompt, y_sample, new_sc_p, new_sc_s, new_conv_p, new_conv_s, new_ssd_p, new_ssd_s)
```

```python
import functools

import numpy as np

import jax
import jax.numpy as jnp
from jax import lax
from jax.experimental import pallas as pl
from jax.experimental.pallas import tpu as pltpu

F32 = jnp.float32
BF16 = jnp.bfloat16

D_MODEL = 2048
SEQ = 8192
DEPTH = 2
DEC_BATCH = 32
DEC_SEQ = 16
N_META = 16
D_FF = 5632
SC_WIDTH = 3
D_INNER = 4096
HEADDIM = 64
HEADS = 64
GROUPS = 8
HEADS_PER_GROUP = 8
STATE = 128
SSD_CONV_WIDTH = 4
CONV_DIM = D_INNER + 2 * GROUPS * STATE
GROUP_W = HEADS_PER_GROUP * HEADDIM
XBC_W = GROUP_W + 2 * STATE
ALPHA = (2.0 * DEPTH) ** 0.25
LN_EPS = 1e-5
RMS_EPS = 1e-5

CHUNK = 128
T_PROMPT = N_META + SEQ
N_MAIN_CHUNKS = -(-T_PROMPT // CHUNK)
T_MAIN = N_MAIN_CHUNKS * CHUNK
N_PAD = T_MAIN - T_PROMPT
N_SAMP = DEC_BATCH * DEC_SEQ
N_SAMP_CHUNKS = N_SAMP // CHUNK
SEQ_PER_CHUNK = CHUNK // DEC_SEQ
T_ALL = T_MAIN + N_SAMP

TM = 736
TM_IN = 2 * TM
TF = 512
TF_HEAD = 256
FFN_ROW_SPLIT = 2
PROJ_ROW_SPLIT = 2
ZXD_W = 10368
ZXD_TN = 1152
DT_COL = D_INNER + CONV_DIM
SSD_NB = 5
SSD_RB = SSD_NB * CHUNK
LANES = 128
SUBLANES = 8
CONV_HEAD = SUBLANES

VMEM_LIMIT_BYTES = 56 * 1024 * 1024


def _cparams(sem):
    return pltpu.CompilerParams(dimension_semantics=sem, vmem_limit_bytes=VMEM_LIMIT_BYTES)


def _layer_norm(v, g, b):
    mu = jnp.mean(v, axis=-1, keepdims=True)
    c = v - mu
    var = jnp.mean(c * c, axis=-1, keepdims=True)
    return c * lax.rsqrt(var + LN_EPS) * g + b


def _silu(x):
    return x * jax.nn.sigmoid(x)


def _softplus(x):
    return jnp.maximum(x, 0.0) + jnp.log(1.0 + jnp.exp(-jnp.abs(x)))


def _ffn_kernel(*refs, nf, n_alias, emit_bf16, convert):
    x_ref, w1_ref, w3_ref, w2_ref, g_ref, b_ref = refs[:6]
    outs = refs[6 + n_alias:]
    o_ref = outs[0]
    pos = 1
    ob_ref = None
    if emit_bf16:
        ob_ref = outs[pos]
        pos += 1
    if convert:
        wb_refs = outs[pos:pos + 3]
        for src, dst in zip((w1_ref, w3_ref, w2_ref), wb_refs):
            dst[...] = src[...].astype(BF16)
        w1_ref, w3_ref, w2_ref = wb_refs
        pos += 3
    xb_ref = outs[pos]
    f = pl.program_id(1)

    @pl.when(f == 0)
    def _():
        xb_ref[...] = x_ref[...].astype(BF16)
        o_ref[...] = jnp.zeros_like(o_ref)

    rs = TM // FFN_ROW_SPLIT

    def slab(s, last):
        rows = pl.ds(s * rs, rs)
        xb = xb_ref[rows, :]
        h1 = jnp.dot(xb, w1_ref[...], preferred_element_type=F32)
        h3 = jnp.dot(xb, w3_ref[...], preferred_element_type=F32)
        gate = (_silu(h1) * h3).astype(BF16)
        acc = o_ref[rows, :] + jnp.dot(gate, w2_ref[...], preferred_element_type=F32)
        if last:
            res = _layer_norm(ALPHA * x_ref[rows, :] + 0.5 * acc, g_ref[...], b_ref[...])
            o_ref[rows, :] = res
            if ob_ref is not None:
                ob_ref[rows, :] = res.astype(BF16)
        else:
            o_ref[rows, :] = acc

    @pl.when(f < nf - 1)
    def _():
        for s in range(FFN_ROW_SPLIT):
            slab(s, False)

    @pl.when(f == nf - 1)
    def _():
        for s in range(FFN_ROW_SPLIT):
            slab(s, True)


def _ffn(h, w1, w3, w2, lng, lnb, i, j, k, emit_bf16=False):
    ln_spec = pl.BlockSpec((None, None, 1, D_MODEL), lambda m, f: (i, k, 0, 0))
    h_shapes = [jax.ShapeDtypeStruct((T_ALL, D_MODEL), F32)]
    if emit_bf16:
        h_shapes.append(jax.ShapeDtypeStruct((T_ALL, D_MODEL), BF16))
    n_h = len(h_shapes)
    scratch = [pltpu.VMEM((TM, D_MODEL), BF16)]

    head_row = pl.BlockSpec((TM, D_MODEL), lambda m, f: (0, 0))
    nf = D_FF // TF
    nf_head = D_FF // TF_HEAD
    per = TF // TF_HEAD
    head = pl.pallas_call(
        functools.partial(_ffn_kernel, nf=nf_head, n_alias=0, emit_bf16=emit_bf16, convert=True),
        grid=(1, nf_head),
        in_specs=[
            head_row,
            pl.BlockSpec((None, None, D_MODEL, TF_HEAD), lambda m, f: (i, j, 0, f)),
            pl.BlockSpec((None, None, D_MODEL, TF_HEAD), lambda m, f: (i, j, 0, f)),
            pl.BlockSpec((None, None, TF_HEAD, D_MODEL), lambda m, f: (i, j, f, 0)),
            ln_spec, ln_spec,
        ],
        out_specs=[head_row] * n_h + [
            pl.BlockSpec((None, D_MODEL, TF_HEAD), lambda m, f: (f // per, 0, f % per)),
            pl.BlockSpec((None, D_MODEL, TF_HEAD), lambda m, f: (f // per, 0, f % per)),
            pl.BlockSpec((TF_HEAD, D_MODEL), lambda m, f: (f, 0)),
        ],
        out_shape=h_shapes + [
            jax.ShapeDtypeStruct((nf, D_MODEL, TF), BF16),
            jax.ShapeDtypeStruct((nf, D_MODEL, TF), BF16),
            jax.ShapeDtypeStruct((D_FF, D_MODEL), BF16),
        ],
        scratch_shapes=scratch,
        compiler_params=_cparams(("parallel", "arbitrary")),
        name="ffn_ln_head",
    )(h, w1, w3, w2, lng, lnb)
    h_parts, (w1b, w3b, w2b) = head[:n_h], head[n_h:]

    row = pl.BlockSpec((TM, D_MODEL), lambda m, f: (m + 1, 0))
    out = pl.pallas_call(
        functools.partial(_ffn_kernel, nf=nf, n_alias=n_h, emit_bf16=emit_bf16, convert=False),
        grid=(T_ALL // TM - 1, nf),
        in_specs=[
            row,
            pl.BlockSpec((None, D_MODEL, TF), lambda m, f: (f, 0, 0)),
            pl.BlockSpec((None, D_MODEL, TF), lambda m, f: (f, 0, 0)),
            pl.BlockSpec((TF, D_MODEL), lambda m, f: (f, 0)),
            ln_spec, ln_spec,
        ] + [pl.BlockSpec(memory_space=pl.ANY)] * n_h,
        out_specs=[row] * n_h,
        out_shape=h_shapes,
        scratch_shapes=scratch,
        input_output_aliases={6 + a: a for a in range(n_h)},
        compiler_params=_cparams(("parallel", "arbitrary")),
        name="ffn_ln",
    )(h, w1b, w3b, w2b, lng, lnb, *h_parts)
    return tuple(out) if emit_bf16 else out[0]


def _mm_kernel(x_ref, wt_ref, o_ref, wb_ref):
    @pl.when(pl.program_id(1) == 0)
    def _():
        wb_ref[...] = wt_ref[...].astype(BF16)

    o_ref[...] = lax.dot_general(x_ref[...], wb_ref[...], (((1,), (1,)), ((), ())),
                                 preferred_element_type=F32)


def _mm(xb, wt, n_out, tn):
    k = wt.shape[1]
    return pl.pallas_call(
        _mm_kernel,
        grid=(n_out // tn, T_ALL // TM_IN),
        in_specs=[
            pl.BlockSpec((TM_IN, k), lambda j, m: (m, 0)),
            pl.BlockSpec((tn, k), lambda j, m: (j, 0)),
        ],
        out_specs=pl.BlockSpec((TM_IN, tn), lambda j, m: (m, j)),
        out_shape=jax.ShapeDtypeStruct((T_ALL, n_out), F32),
        scratch_shapes=[pltpu.VMEM((tn, k), BF16)],
        compiler_params=_cparams(("parallel", "arbitrary")),
        name="in_proj",
    )(xb, wt)


def _sc_in_kernel(x_ref, wb_ref, wc_ref, wh_ref, bg_ref, u_ref, wbb_ref, wcb_ref, whb_ref):
    @pl.when(pl.program_id(1) == 0)
    def _():
        wbb_ref[...] = wb_ref[...].astype(BF16)
        wcb_ref[...] = wc_ref[...].astype(BF16)
        whb_ref[...] = wh_ref[...].astype(BF16)

    xb = x_ref[...]
    bg_ref[...] = jnp.dot(xb, wbb_ref[...], preferred_element_type=F32)
    c = jnp.dot(xb, wcb_ref[...], preferred_element_type=F32)
    hh = jnp.dot(xb, whb_ref[...], preferred_element_type=F32)
    u_ref[...] = c * hh


def _sc_in(xb, w_in, j):
    tn = 512
    nb = D_MODEL // tn
    row = pl.BlockSpec((TM, tn), lambda n, m: (m, n))
    wscratch = pltpu.VMEM((D_MODEL, tn), BF16)
    return pl.pallas_call(
        _sc_in_kernel,
        grid=(nb, T_ALL // TM),
        in_specs=[
            pl.BlockSpec((TM, D_MODEL), lambda n, m: (m, 0)),
            pl.BlockSpec((None, D_MODEL, tn), lambda n, m: (j, 0, n)),
            pl.BlockSpec((None, D_MODEL, tn), lambda n, m: (j, 0, n + nb)),
            pl.BlockSpec((None, D_MODEL, tn), lambda n, m: (j, 0, n + 2 * nb)),
        ],
        out_specs=[row, row],
        out_shape=[jax.ShapeDtypeStruct((T_ALL, D_MODEL), F32)] * 2,
        scratch_shapes=[wscratch, wscratch, wscratch],
        compiler_params=_cparams(("parallel", "arbitrary")),
        name="sc_in_proj",
    )(xb, w_in, w_in, w_in)


CONV_TILE = 128
CONV_CB = 2048
HALO = 8


def _conv_kernel(*refs, taps, has_bias, has_gate, act):
    u_ref, halo_ref = refs[0], refs[1]
    ov_refs = refs[2:2 + taps - 1]
    w_ref = refs[2 + taps - 1]
    pos = 3 + taps - 1
    bias_ref = gate_ref = None
    if has_bias:
        bias_ref = refs[pos]
        pos += 1
    if has_gate:
        gate_ref = refs[pos]
        pos += 1
    o_ref = refs[pos]

    i = pl.program_id(0)
    first_samp_tile = T_MAIN // CONV_TILE

    def body(in_samples):
        u = u_ref[...]
        halo = jnp.where(i > 0, halo_ref[...], 0.0)
        lrow = lax.broadcasted_iota(jnp.int32, (CONV_TILE, 1), 0)
        spos = jnp.bitwise_and(lrow, DEC_SEQ - 1)
        w = w_ref[...]
        acc = w[taps - 1:taps, :] * u
        for d in range(1, taps):
            ud = pltpu.roll(u, d, axis=0)
            if in_samples:
                ud = jnp.where(spos < d, ov_refs[d - 1][...], ud)
            else:
                for r in range(d):
                    ud = jnp.where(lrow == r, halo[HALO - d + r:HALO - d + r + 1, :], ud)
            acc = acc + w[taps - 1 - d:taps - d, :] * ud
        if has_bias:
            acc = acc + bias_ref[...]
        if act:
            acc = _silu(acc)
        if has_gate:
            acc = acc * gate_ref[...]
        o_ref[...] = acc.astype(o_ref.dtype)

    @pl.when(i < first_samp_tile)
    def _():
        body(False)

    @pl.when(i >= first_samp_tile)
    def _():
        body(True)


def _conv_overrides(prev, taps):
    c = prev.shape[-1]
    out = []
    for d in range(1, taps):
        o = jnp.zeros((DEC_BATCH, DEC_SEQ, c), F32)
        for p in range(d):
            o = o.at[:, p].set(prev[:, taps - 1 + p - d])
        out.append(o.reshape(N_SAMP, c))
    return out


def _conv(u, col_off, n_ch, prev, w, bias, gate, act, out_dtype):
    taps = w.shape[0]
    ncb = n_ch // CONV_CB
    cb0 = col_off // CONV_CB
    tiles_per_halo = CONV_TILE // HALO
    samp_tile0 = T_MAIN // CONV_TILE
    ovs = _conv_overrides(prev, taps)
    in_specs = [
        pl.BlockSpec((CONV_TILE, CONV_CB), lambda i, c: (i, cb0 + c)),
        pl.BlockSpec((HALO, CONV_CB), lambda i, c: (jnp.maximum(i * tiles_per_halo - 1, 0), cb0 + c)),
    ]
    args = [u, u]
    for o in ovs:
        in_specs.append(pl.BlockSpec((CONV_TILE, CONV_CB), lambda i, c: (jnp.maximum(i - samp_tile0, 0), c)))
        args.append(o)
    in_specs.append(pl.BlockSpec((taps, CONV_CB), lambda i, c: (0, c)))
    args.append(w)
    if bias is not None:
        in_specs.append(pl.BlockSpec((1, CONV_CB), lambda i, c: (0, c)))
        args.append(bias.reshape(1, n_ch))
    if gate is not None:
        in_specs.append(pl.BlockSpec((CONV_TILE, CONV_CB), lambda i, c: (i, c)))
        args.append(gate)
    return pl.pallas_call(
        functools.partial(_conv_kernel, taps=taps, has_bias=bias is not None,
                          has_gate=gate is not None, act=act),
        grid=(T_ALL // CONV_TILE, ncb),
        in_specs=in_specs,
        out_specs=pl.BlockSpec((CONV_TILE, CONV_CB), lambda i, c: (i, c)),
        out_shape=jax.ShapeDtypeStruct((T_ALL, n_ch), out_dtype),
        compiler_params=_cparams(("parallel", "parallel")),
        name="causal_conv",
    )(*args)


def _proj_ln_kernel(y_ref, w_ref, h_ref, g_ref, b_ref, o_ref, *, nk):
    rs = TM // PROJ_ROW_SPLIT

    def slab_dot(rows):
        return jnp.dot(y_ref[rows, :], w_ref[...], preferred_element_type=F32)

    def finish(rows, acc):
        v = ALPHA * h_ref[rows, :] + acc
        o_ref[rows, :] = _layer_norm(v, g_ref[...], b_ref[...])

    slabs = [pl.ds(s * rs, rs) for s in range(PROJ_ROW_SPLIT)]
    if nk == 1:
        for rows in slabs:
            finish(rows, slab_dot(rows))
        return

    k = pl.program_id(1)

    @pl.when(k == 0)
    def _():
        for rows in slabs:
            o_ref[rows, :] = slab_dot(rows)

    @pl.when(k == 1)
    def _():
        for rows in slabs:
            finish(rows, o_ref[rows, :] + slab_dot(rows))


def _proj_ln(y, w, j, h, lng, lnb, i, k):
    kdim = w.shape[1]
    tk = 2048
    nk = kdim // tk
    assert nk in (1, 2)
    return pl.pallas_call(
        functools.partial(_proj_ln_kernel, nk=nk),
        grid=(T_ALL // TM, nk),
        in_specs=[
            pl.BlockSpec((TM, tk), lambda m, kk: (m, kk)),
            pl.BlockSpec((None, tk, D_MODEL), lambda m, kk: (j, kk, 0)),
            pl.BlockSpec((TM, D_MODEL), lambda m, kk: (m, 0)),
            pl.BlockSpec((None, None, 1, D_MODEL), lambda m, kk: (i, k, 0, 0)),
            pl.BlockSpec((None, None, 1, D_MODEL), lambda m, kk: (i, k, 0, 0)),
        ],
        out_specs=pl.BlockSpec((TM, D_MODEL), lambda m, kk: (m, 0)),
        out_shape=jax.ShapeDtypeStruct((T_ALL, D_MODEL), F32),
        compiler_params=_cparams(("parallel", "arbitrary")),
        name="out_proj_ln",
    )(y, w, h, lng, lnb)


_NT = (((1,), (1,)), ((), ()))
_TN = (((0,), (0,)), ((), ()))
SPLIT_K = 2 * LANES


def _bf16_pieces(v):
    hi = v.astype(BF16)
    r1 = v - hi.astype(F32)
    mid = r1.astype(BF16)
    lo = (r1 - mid.astype(F32)).astype(BF16)
    return hi, mid, lo


def _split_pack(v, lane_lo):
    hi, mid, lo = _bf16_pieces(v)
    a = jnp.where(lane_lo, hi.astype(F32), pltpu.roll(mid.astype(F32), HEADS, axis=1))
    b = jnp.where(lane_lo, lo.astype(F32), 0.0)
    return jnp.concatenate([a, b], axis=1).astype(BF16)


def _split_pack2(v, lane_lo):
    hi = v.astype(BF16)
    mid = (v - hi.astype(F32)).astype(BF16)
    return jnp.where(lane_lo, hi.astype(F32), pltpu.roll(mid.astype(F32), HEADS, axis=1)).astype(BF16)


def _masked_cumsum(mask_bf, da):
    hi, mid, lo = _bf16_pieces(da)
    p = jnp.dot(mask_bf, jnp.concatenate([hi, mid, lo], axis=1), preferred_element_type=F32)
    return p[:, :LANES] + p[:, LANES:2 * LANES] + p[:, 2 * LANES:]


def _conv_silu(buf_ref, base, n, cw, cbias):
    taps = SSD_CONV_WIDTH
    acc = cw[taps - 1:taps, :] * buf_ref[base:base + n, :]
    for d in range(1, taps):
        acc = acc + cw[taps - 1 - d:taps - d, :] * buf_ref[base - d:base - d + n, :]
    return _silu(acc + cbias)


def _intra_chunk(xbc, dt, acs, te, arow, causal, lane_lo, r_ref, e_ref):
    xs = xbc[:, :GROUP_W]
    bmb = xbc[:, GROUP_W:GROUP_W + STATE].astype(BF16)
    cmb = xbc[:, GROUP_W + STATE:].astype(BF16)
    colmat = jnp.dot(_split_pack(acs, lane_lo), r_ref[...], preferred_element_type=F32)
    packed = jnp.concatenate([_split_pack2(dt, lane_lo), _split_pack2(te, lane_lo),
                              _split_pack2(jnp.exp(acs), lane_lo)], axis=0)
    ex = jnp.dot(packed, e_ref[...], preferred_element_type=F32)
    dtx, tex, eax = ex[:CHUNK], ex[CHUNK:2 * CHUNK], ex[2 * CHUNK:]
    xdt = xs * dtx
    cb = lax.dot_general(cmb, bmb, _NT, preferred_element_type=F32)
    ys = []
    for q in range(HEADS_PER_GROUP // 2):
        scs = []
        for r in (2 * q, 2 * q + 1):
            seg = colmat[:, r * CHUNK:(r + 1) * CHUNK] - arow[r:r + 1, :]
            scs.append((cb * jnp.exp(jnp.where(causal, seg, -jnp.inf))).astype(BF16))
        xp = xdt[:, q * LANES:(q + 1) * LANES]
        rhs = jnp.concatenate([jnp.where(lane_lo, xp, 0.0).astype(BF16),
                               jnp.where(lane_lo, 0.0, xp).astype(BF16)], axis=0)
        ys.append(jnp.dot(jnp.concatenate(scs, axis=1), rhs, preferred_element_type=F32))
    y_intra = jnp.concatenate(ys, axis=1)
    xw = (xdt * tex).astype(BF16)
    return xs, bmb, cmb, y_intra, eax, xw


def _gate_norm(y, z, nw):
    y = y * _silu(z)
    ms = jnp.mean(y * y, axis=-1, keepdims=True)
    return (y * lax.rsqrt(ms + RMS_EPS) * nw).astype(BF16)


def _chunk_iotas():
    li = lax.broadcasted_iota(jnp.int32, (CHUNK, CHUNK), 0)
    si = lax.broadcasted_iota(jnp.int32, (CHUNK, CHUNK), 1)
    return li, si


def _ssd_main_kernel(z_ref, xr_ref, br_ref, cr_ref, dt_ref, cw_ref, cbias_ref, dtb_ref, al_ref, dx_ref, nw_ref,
                     r_ref, e_ref, y_ref, sfin_ref, buf_ref, st_ref, acst_ref):
    g = pl.program_id(0)
    rb = pl.program_id(1)

    @pl.when(rb == 0)
    def _():
        st_ref[...] = jnp.zeros_like(st_ref)
        buf_ref[0:CONV_HEAD, :] = jnp.zeros((CONV_HEAD, XBC_W), F32)

    @pl.when(rb > 0)
    def _():
        buf_ref[0:CONV_HEAD, :] = buf_ref[SSD_RB:SSD_RB + CONV_HEAD, :]

    buf_ref[CONV_HEAD:, 0:GROUP_W] = xr_ref[...]
    buf_ref[CONV_HEAD:, GROUP_W:GROUP_W + STATE] = br_ref[...]
    buf_ref[CONV_HEAD:, GROUP_W + STATE:] = cr_ref[...]

    li, si = _chunk_iotas()
    causal = si <= li
    causal_bf = causal.astype(F32).astype(BF16)
    lane_lo = si < HEADS
    row_iota = lax.broadcasted_iota(jnp.int32, (CHUNK, 1), 0)
    g8 = pl.multiple_of(g * HEADS_PER_GROUP, SUBLANES)
    cw = cw_ref[...]
    cbias = cbias_ref[...]
    a_row = -jnp.exp(al_ref[...])

    for k in range(SSD_NB):
        rows = pl.ds(k * CHUNK, CHUNK)
        xbc = _conv_silu(buf_ref, CONV_HEAD + k * CHUNK, CHUNK, cw, cbias)
        valid = row_iota + (rb * SSD_RB + k * CHUNK) < T_PROMPT
        dt = jnp.where(jnp.logical_and(valid, lane_lo), _softplus(dt_ref[rows, :] + dtb_ref[...]), 0.0)
        acs = _masked_cumsum(causal_bf, dt * a_row)
        te = jnp.exp(acs[CHUNK - 1:CHUNK, :] - acs)
        acst_ref[...] = acs.T
        arow = acst_ref[pl.ds(g8, HEADS_PER_GROUP), :]
        xs, bmb, cmb, y_intra, eax, xw = _intra_chunk(xbc, dt, acs, te, arow, causal, lane_lo, r_ref, e_ref)
        st = st_ref[...]
        y = y_intra + jnp.dot(cmb, st.astype(BF16), preferred_element_type=F32) * eax + dx_ref[...] * xs
        y_ref[rows, :] = _gate_norm(y, z_ref[rows, :], nw_ref[...])
        st_ref[...] = st * eax[CHUNK - 1:CHUNK, :] + lax.dot_general(bmb, xw, _TN, preferred_element_type=F32)

    @pl.when(rb == T_MAIN // SSD_RB - 1)
    def _():
        sfin_ref[...] = st_ref[...]


def _ssd_samp_kernel(z_ref, xr_ref, br_ref, cr_ref, dt_ref, cw_ref, cbias_ref, dtb_ref, al_ref, dx_ref, nw_ref,
                     r_ref, e_ref, prev_ref, s0_ref, yprev_ref, y_ref, s1_ref,
                     buf_ref, xbc_ref, yint_ref, acst_ref, tott_ref):
    del yprev_ref
    g = pl.program_id(0)
    slot = CONV_HEAD + DEC_SEQ
    cw = cw_ref[...]
    cbias = cbias_ref[...]
    for q in range(SEQ_PER_CHUNK):
        r0 = q * DEC_SEQ
        top = q * slot + CONV_HEAD
        buf_ref[top - (SSD_CONV_WIDTH - 1):top, :] = prev_ref[q]
        buf_ref[top:top + DEC_SEQ, 0:GROUP_W] = xr_ref[r0:r0 + DEC_SEQ, :]
        buf_ref[top:top + DEC_SEQ, GROUP_W:GROUP_W + STATE] = br_ref[r0:r0 + DEC_SEQ, :]
        buf_ref[top:top + DEC_SEQ, GROUP_W + STATE:] = cr_ref[r0:r0 + DEC_SEQ, :]
    for q in range(SEQ_PER_CHUNK):
        r0 = q * DEC_SEQ
        xbc_ref[r0:r0 + DEC_SEQ, :] = _conv_silu(buf_ref, q * slot + CONV_HEAD, DEC_SEQ, cw, cbias)

    li, si = _chunk_iotas()
    same = (li // DEC_SEQ) == (si // DEC_SEQ)
    causal = jnp.logical_and(si <= li, same)
    lane_lo = si < HEADS
    g8 = pl.multiple_of(g * HEADS_PER_GROUP, SUBLANES)

    dt = jnp.where(lane_lo, _softplus(dt_ref[...] + dtb_ref[...]), 0.0)
    da = dt * (-jnp.exp(al_ref[...]))
    acs = _masked_cumsum(causal.astype(F32).astype(BF16), da)
    tot = _masked_cumsum(same.astype(F32).astype(BF16), da)
    te = jnp.exp(tot - acs)
    acst_ref[...] = acs.T
    tott_ref[...] = tot.T
    arow = acst_ref[pl.ds(g8, HEADS_PER_GROUP), :]
    trow = tott_ref[pl.ds(g8, HEADS_PER_GROUP), :]
    xs, bmb, cmb, y_intra, eax, xw = _intra_chunk(xbc_ref[...], dt, acs, te, arow, causal, lane_lo, r_ref, e_ref)
    for q in range(SEQ_PER_CHUNK):
        r0 = q * DEC_SEQ
        yint_ref[r0:r0 + DEC_SEQ, :] = lax.dot_general(cmb[r0:r0 + DEC_SEQ, :], s0_ref[q].astype(BF16), _NT,
                                                       preferred_element_type=F32)
    y = y_intra + yint_ref[...] * eax + dx_ref[...] * xs
    y_ref[...] = _gate_norm(y, z_ref[...], nw_ref[...])
    for q in range(SEQ_PER_CHUNK):
        r0 = q * DEC_SEQ
        upd = lax.dot_general(xw[r0:r0 + DEC_SEQ, :], bmb[r0:r0 + DEC_SEQ, :], _TN, preferred_element_type=F32)
        for r in range(HEADS_PER_GROUP):
            lo, hi = r * HEADDIM, (r + 1) * HEADDIM
            s1_ref[q, lo:hi, :] = s0_ref[q, lo:hi, :] * jnp.exp(trow[r:r + 1, r0:r0 + 1]) + upd[lo:hi, :]


def _ssd_specs(rows, row0):
    xcol = D_INNER // GROUP_W
    bcol = (2 * D_INNER) // STATE
    return [
        pl.BlockSpec((rows, GROUP_W), lambda g, c: (row0 + c, g)),
        pl.BlockSpec((rows, GROUP_W), lambda g, c: (row0 + c, xcol + g)),
        pl.BlockSpec((rows, STATE), lambda g, c: (row0 + c, bcol + g)),
        pl.BlockSpec((rows, STATE), lambda g, c: (row0 + c, bcol + GROUPS + g)),
        pl.BlockSpec((rows, LANES), lambda g, c: (row0 + c, DT_COL // LANES)),
        pl.BlockSpec((None, SSD_CONV_WIDTH, XBC_W), lambda g, c: (g, 0, 0)),
        pl.BlockSpec((None, 1, XBC_W), lambda g, c: (g, 0, 0)),
        pl.BlockSpec((1, LANES), lambda g, c: (0, 0)),
        pl.BlockSpec((1, LANES), lambda g, c: (0, 0)),
        pl.BlockSpec((1, GROUP_W), lambda g, c: (0, g)),
        pl.BlockSpec((1, GROUP_W), lambda g, c: (0, g)),
        pl.BlockSpec((None, SPLIT_K, HEADS_PER_GROUP * CHUNK), lambda g, c: (g, 0, 0)),
        pl.BlockSpec((None, LANES, GROUP_W), lambda g, c: (g, 0, 0)),
    ]


def _spread_matrices():
    k = np.arange(SPLIT_K)
    head = k % HEADS
    used = (k // HEADS) < 3
    g = np.arange(GROUPS)[:, None, None]
    rj = np.arange(HEADS_PER_GROUP * CHUNK)[None, None, :] // CHUNK
    ej = np.arange(GROUP_W)[None, None, :] // HEADDIM
    hk = head[None, :, None]
    uk = used[None, :, None]
    r = (uk & (hk == g * HEADS_PER_GROUP + rj)).astype(np.float32)
    e = (hk == g * HEADS_PER_GROUP + ej)[:, :LANES].astype(np.float32)
    return jnp.asarray(r, BF16), jnp.asarray(e, BF16)


def _per_group(a):
    lead = a.shape[:-1]
    x = a[..., :D_INNER].reshape(*lead, GROUPS, GROUP_W)
    b = a[..., D_INNER:D_INNER + GROUPS * STATE].reshape(*lead, GROUPS, STATE)
    c = a[..., D_INNER + GROUPS * STATE:].reshape(*lead, GROUPS, STATE)
    return jnp.moveaxis(jnp.concatenate([x, b, c], axis=-1), -2, 0)


def _ssd(zxd, conv_w, conv_b, dt_bias, a_log, d_skip, norm_w, conv_prev, state0):
    r_mat, e_mat = _spread_matrices()
    pad = ((0, 0), (0, LANES - HEADS))
    params = (_per_group(conv_w), _per_group(conv_b.reshape(1, CONV_DIM)),
              jnp.pad(dt_bias.reshape(1, HEADS).astype(F32), pad),
              jnp.pad(a_log.reshape(1, HEADS).astype(F32), pad),
              jnp.repeat(d_skip.astype(F32), HEADDIM).reshape(1, D_INNER),
              norm_w.reshape(1, D_INNER).astype(F32), r_mat, e_mat)
    common = (zxd,) * 5 + params
    y, s_fin = pl.pallas_call(
        _ssd_main_kernel,
        grid=(GROUPS, T_MAIN // SSD_RB),
        in_specs=_ssd_specs(SSD_RB, 0),
        out_specs=[
            pl.BlockSpec((SSD_RB, GROUP_W), lambda g, c: (c, g)),
            pl.BlockSpec((None, STATE, GROUP_W), lambda g, c: (g, 0, 0)),
        ],
        out_shape=[
            jax.ShapeDtypeStruct((T_ALL, D_INNER), BF16),
            jax.ShapeDtypeStruct((GROUPS, STATE, GROUP_W), F32),
        ],
        scratch_shapes=[
            pltpu.VMEM((CONV_HEAD + SSD_RB, XBC_W), F32),
            pltpu.VMEM((STATE, GROUP_W), F32),
            pltpu.VMEM((LANES, CHUNK), F32),
        ],
        compiler_params=_cparams(("parallel", "arbitrary")),
        name="ssd_prompt",
    )(*common)

    n_in = len(common)
    y, s_new = pl.pallas_call(
        _ssd_samp_kernel,
        grid=(GROUPS, N_SAMP_CHUNKS),
        in_specs=_ssd_specs(CHUNK, N_MAIN_CHUNKS) + [
            pl.BlockSpec((None, SEQ_PER_CHUNK, SSD_CONV_WIDTH - 1, XBC_W), lambda g, c: (g, c, 0, 0)),
            pl.BlockSpec((SEQ_PER_CHUNK, None, GROUP_W, STATE), lambda g, c: (c, g, 0, 0)),
            pl.BlockSpec(memory_space=pl.ANY),
        ],
        out_specs=[
            pl.BlockSpec((CHUNK, GROUP_W), lambda g, c: (N_MAIN_CHUNKS + c, g)),
            pl.BlockSpec((SEQ_PER_CHUNK, None, GROUP_W, STATE), lambda g, c: (c, g, 0, 0)),
        ],
        out_shape=[
            jax.ShapeDtypeStruct((T_ALL, D_INNER), BF16),
            jax.ShapeDtypeStruct((DEC_BATCH, GROUPS, GROUP_W, STATE), F32),
        ],
        scratch_shapes=[
            pltpu.VMEM((SEQ_PER_CHUNK * (CONV_HEAD + DEC_SEQ), XBC_W), F32),
            pltpu.VMEM((CHUNK, XBC_W), F32),
            pltpu.VMEM((CHUNK, GROUP_W), F32),
            pltpu.VMEM((LANES, CHUNK), F32),
            pltpu.VMEM((LANES, CHUNK), F32),
        ],
        input_output_aliases={n_in + 2: 0},
        compiler_params=_cparams(("parallel", "arbitrary")),
        name="ssd_sample",
    )(*common, _per_group(conv_prev), state0, y)
    return y, s_fin, s_new


def _tail_rows(a, n):
    c = a.shape[-1]
    p = a[T_PROMPT - n:T_PROMPT].reshape(1, 1, n, c)
    s = a[T_MAIN:].reshape(DEC_BATCH, DEC_SEQ, c)[:, DEC_SEQ - n:].reshape(1, DEC_BATCH, n, c)
    return p, s


def kernel(x_prompt, x_sample, cache_sc_conv, state_ssd_conv, state_ssd, meta_tokens, ln_g, ln_b,
           ffn_w1, ffn_w3, ffn_w2, sc_w_in, sc_w_conv, sc_w_out,
           ssd_w_in, ssd_w_conv, ssd_b_conv, ssd_dt_bias, ssd_a_log, ssd_d, ssd_norm_w, ssd_w_out):
    w1, w3, w2 = ffn_w1, ffn_w3, ffn_w2
    sc_out_w = sc_w_out.astype(BF16)
    ssd_out_w = ssd_w_out.astype(BF16)
    lng = ln_g.reshape(DEPTH, 3, 1, D_MODEL)
    lnb = ln_b.reshape(DEPTH, 3, 1, D_MODEL)

    h = jnp.concatenate([meta_tokens.astype(F32), x_prompt[0], jnp.zeros((N_PAD, D_MODEL), F32),
                         x_sample.reshape(N_SAMP, D_MODEL)], axis=0)

    h, hb = _ffn(h, w1, w3, w2, lng, lnb, 0, 0, 0, emit_bf16=True)
    bgate, u = _sc_in(hb, sc_w_in, 0)
    v = _conv(u, 0, D_MODEL, cache_sc_conv[0], sc_w_conv[0], None, bgate, False, BF16)
    new_sc_p, new_sc_s = _tail_rows(u, SC_WIDTH - 1)
    h = _proj_ln(v, sc_out_w, 0, h, lng, lnb, 0, 1)
    h = _ffn(h, w1, w3, w2, lng, lnb, 0, 1, 2)

    h, hb = _ffn(h, w1, w3, w2, lng, lnb, 1, 0, 0, emit_bf16=True)
    zxd = _mm(hb, jnp.swapaxes(ssd_w_in[0], 0, 1), ZXD_W, ZXD_TN)
    new_conv_p, new_conv_s = _tail_rows(zxd[:, D_INNER:DT_COL], SSD_CONV_WIDTH - 1)
    state0 = state_ssd[0].reshape(DEC_BATCH, GROUPS, GROUP_W, STATE)
    y, s_fin, s_new = _ssd(zxd, ssd_w_conv[0], ssd_b_conv[0], ssd_dt_bias[0], ssd_a_log[0], ssd_d[0],
                           ssd_norm_w[0], state_ssd_conv[0], state0)
    h = _proj_ln(y, ssd_out_w, 0, h, lng, lnb, 1, 1)
    h = _ffn(h, w1, w3, w2, lng, lnb, 1, 1, 2)

    y_prompt = h[N_META:T_PROMPT].reshape(1, SEQ, D_MODEL)
    y_sample = h[T_MAIN:].reshape(DEC_BATCH, DEC_SEQ, D_MODEL)
    new_ssd_p = jnp.swapaxes(s_fin, 1, 2).reshape(1, 1, HEADS, HEADDIM, STATE).astype(state_ssd.dtype)
    new_ssd_s = s_new.reshape(1, DEC_BATCH, HEADS, HEADDIM, STATE).astype(state_ssd.dtype)
    return (y_prompt, y_sample, new_sc_p, new_sc_s, new_conv_p, new_conv_s, new_ssd_p, new_ssd_s)
```

```python
import functools

import numpy as np

import jax
import jax.numpy as jnp
from jax import lax
from jax.experimental import pallas as pl
from jax.experimental.pallas import tpu as pltpu

F32 = jnp.float32
BF16 = jnp.bfloat16

D_MODEL = 2048
SEQ = 8192
DEPTH = 2
DEC_BATCH = 32
DEC_SEQ = 16
N_META = 16
D_FF = 5632
SC_WIDTH = 3
D_INNER = 4096
HEADDIM = 64
HEADS = 64
GROUPS = 8
HEADS_PER_GROUP = 8
STATE = 128
SSD_CONV_WIDTH = 4
CONV_DIM = D_INNER + 2 * GROUPS * STATE
GROUP_W = HEADS_PER_GROUP * HEADDIM
XBC_W = GROUP_W + 2 * STATE
ALPHA = (2.0 * DEPTH) ** 0.25
LN_EPS = 1e-5
RMS_EPS = 1e-5

CHUNK = 128
T_PROMPT = N_META + SEQ
N_MAIN_CHUNKS = -(-T_PROMPT // CHUNK)
T_MAIN = N_MAIN_CHUNKS * CHUNK
N_PAD = T_MAIN - T_PROMPT
N_SAMP = DEC_BATCH * DEC_SEQ
N_SAMP_CHUNKS = N_SAMP // CHUNK
SEQ_PER_CHUNK = CHUNK // DEC_SEQ
T_ALL = T_MAIN + N_SAMP

TM = 736
TM_IN = 2 * TM
TF = 512
TF_HEAD = 256
FFN_ROW_SPLIT = 2
PROJ_ROW_SPLIT = 2
ZXD_W = 10368
ZXD_TN = 1152
DT_COL = D_INNER + CONV_DIM
SSD_NB = 5
SSD_RB = SSD_NB * CHUNK
LANES = 128
SUBLANES = 8
CONV_HEAD = SUBLANES

VMEM_LIMIT_BYTES = 56 * 1024 * 1024


def _cparams(sem):
    return pltpu.CompilerParams(dimension_semantics=sem, vmem_limit_bytes=VMEM_LIMIT_BYTES)


def _layer_norm(v, g, b):
    mu = jnp.mean(v, axis=-1, keepdims=True)
    c = v - mu
    var = jnp.mean(c * c, axis=-1, keepdims=True)
    return c * lax.rsqrt(var + LN_EPS) * g + b


def _silu(x):
    return x * jax.nn.sigmoid(x)


def _softplus(x):
    return jnp.maximum(x, 0.0) + jnp.log(1.0 + jnp.exp(-jnp.abs(x)))


def _ffn_kernel(*refs, nf, n_alias, emit_bf16, convert):
    x_ref, w1_ref, w3_ref, w2_ref, g_ref, b_ref = refs[:6]
    outs = refs[6 + n_alias:]
    o_ref = outs[0]
    pos = 1
    ob_ref = None
    if emit_bf16:
        ob_ref = outs[pos]
        pos += 1
    if convert:
        wb_refs = outs[pos:pos + 3]
        for src, dst in zip((w1_ref, w3_ref, w2_ref), wb_refs):
            dst[...] = src[...].astype(BF16)
        w1_ref, w3_ref, w2_ref = wb_refs
        pos += 3
    xb_ref = outs[pos]
    f = pl.program_id(1)

    @pl.when(f == 0)
    def _():
        xb_ref[...] = x_ref[...].astype(BF16)
        o_ref[...] = jnp.zeros_like(o_ref)

    rs = TM // FFN_ROW_SPLIT

    def slab(s, last):
        rows = pl.ds(s * rs, rs)
        xb = xb_ref[rows, :]
        h1 = jnp.dot(xb, w1_ref[...], preferred_element_type=F32)
        h3 = jnp.dot(xb, w3_ref[...], preferred_element_type=F32)
        gate = (_silu(h1) * h3).astype(BF16)
        acc = o_ref[rows, :] + jnp.dot(gate, w2_ref[...], preferred_element_type=F32)
        if last:
            res = _layer_norm(ALPHA * x_ref[rows, :] + 0.5 * acc, g_ref[...], b_ref[...])
            o_ref[rows, :] = res
            if ob_ref is not None:
                ob_ref[rows, :] = res.astype(BF16)
        else:
            o_ref[rows, :] = acc

    @pl.when(f < nf - 1)
    def _():
        for s in range(FFN_ROW_SPLIT):
            slab(s, False)

    @pl.when(f == nf - 1)
    def _():
        for s in range(FFN_ROW_SPLIT):
            slab(s, True)


def _ffn(h, w1, w3, w2, lng, lnb, i, j, k, emit_bf16=False):
    ln_spec = pl.BlockSpec((None, None, 1, D_MODEL), lambda m, f: (i, k, 0, 0))
    h_shapes = [jax.ShapeDtypeStruct((T_ALL, D_MODEL), F32)]
    if emit_bf16:
        h_shapes.append(jax.ShapeDtypeStruct((T_ALL, D_MODEL), BF16))
    n_h = len(h_shapes)
    scratch = [pltpu.VMEM((TM, D_MODEL), BF16)]

    head_row = pl.BlockSpec((TM, D_MODEL), lambda m, f: (0, 0))
    nf = D_FF // TF
    nf_head = D_FF // TF_HEAD
    per = TF // TF_HEAD
    head = pl.pallas_call(
        functools.partial(_ffn_kernel, nf=nf_head, n_alias=0, emit_bf16=emit_bf16, convert=True),
        grid=(1, nf_head),
        in_specs=[
            head_row,
            pl.BlockSpec((None, None, D_MODEL, TF_HEAD), lambda m, f: (i, j, 0, f)),
            pl.BlockSpec((None, None, D_MODEL, TF_HEAD), lambda m, f: (i, j, 0, f)),
            pl.BlockSpec((None, None, TF_HEAD, D_MODEL), lambda m, f: (i, j, f, 0)),
            ln_spec, ln_spec,
        ],
        out_specs=[head_row] * n_h + [
            pl.BlockSpec((None, D_MODEL, TF_HEAD), lambda m, f: (f // per, 0, f % per)),
            pl.BlockSpec((None, D_MODEL, TF_HEAD), lambda m, f: (f // per, 0, f % per)),
            pl.BlockSpec((TF_HEAD, D_MODEL), lambda m, f: (f, 0)),
        ],
        out_shape=h_shapes + [
            jax.ShapeDtypeStruct((nf, D_MODEL, TF), BF16),
            jax.ShapeDtypeStruct((nf, D_MODEL, TF), BF16),
            jax.ShapeDtypeStruct((D_FF, D_MODEL), BF16),
        ],
        scratch_shapes=scratch,
        compiler_params=_cparams(("parallel", "arbitrary")),
        name="ffn_ln_head",
    )(h, w1, w3, w2, lng, lnb)
    h_parts, (w1b, w3b, w2b) = head[:n_h], head[n_h:]

    row = pl.BlockSpec((TM, D_MODEL), lambda m, f: (m + 1, 0))
    out = pl.pallas_call(
        functools.partial(_ffn_kernel, nf=nf, n_alias=n_h, emit_bf16=emit_bf16, convert=False),
        grid=(T_ALL // TM - 1, nf),
        in_specs=[
            row,
            pl.BlockSpec((None, D_MODEL, TF), lambda m, f: (f, 0, 0)),
            pl.BlockSpec((None, D_MODEL, TF), lambda m, f: (f, 0, 0)),
            pl.BlockSpec((TF, D_MODEL), lambda m, f: (f, 0)),
            ln_spec, ln_spec,
        ] + [pl.BlockSpec(memory_space=pl.ANY)] * n_h,
        out_specs=[row] * n_h,
        out_shape=h_shapes,
        scratch_shapes=scratch,
        input_output_aliases={6 + a: a for a in range(n_h)},
        compiler_params=_cparams(("parallel", "arbitrary")),
        name="ffn_ln",
    )(h, w1b, w3b, w2b, lng, lnb, *h_parts)
    return tuple(out) if emit_bf16 else out[0]


def _mm_kernel(x_ref, wt_ref, o_ref, wb_ref, *, n_valid):
    j = pl.program_id(0)
    tn = wt_ref.shape[0]

    @pl.when(pl.program_id(1) == 0)
    def _():
        row = lax.broadcasted_iota(jnp.int32, (tn, 1), 0) + j * tn
        wb_ref[...] = jnp.where(row < n_valid, wt_ref[...], 0.0).astype(BF16)

    o_ref[...] = lax.dot_general(x_ref[...], wb_ref[...], (((1,), (1,)), ((), ())),
                                 preferred_element_type=F32)


def _mm(xb, wt, n_out, tn):
    n, k = wt.shape
    return pl.pallas_call(
        functools.partial(_mm_kernel, n_valid=n),
        grid=(n_out // tn, T_ALL // TM_IN),
        in_specs=[
            pl.BlockSpec((TM_IN, k), lambda j, m: (m, 0)),
            pl.BlockSpec((tn, k), lambda j, m: (j, 0)),
        ],
        out_specs=pl.BlockSpec((TM_IN, tn), lambda j, m: (m, j)),
        out_shape=jax.ShapeDtypeStruct((T_ALL, n_out), F32),
        scratch_shapes=[pltpu.VMEM((tn, k), BF16)],
        compiler_params=_cparams(("parallel", "arbitrary")),
        name="in_proj",
    )(xb, wt)


def _sc_in_kernel(x_ref, wb_ref, wc_ref, wh_ref, bg_ref, u_ref, wbb_ref, wcb_ref, whb_ref):
    @pl.when(pl.program_id(1) == 0)
    def _():
        wbb_ref[...] = wb_ref[...].astype(BF16)
        wcb_ref[...] = wc_ref[...].astype(BF16)
        whb_ref[...] = wh_ref[...].astype(BF16)

    xb = x_ref[...]
    bg_ref[...] = jnp.dot(xb, wbb_ref[...], preferred_element_type=F32)
    c = jnp.dot(xb, wcb_ref[...], preferred_element_type=F32)
    hh = jnp.dot(xb, whb_ref[...], preferred_element_type=F32)
    u_ref[...] = c * hh


def _sc_in(xb, w_in, j):
    tn = 512
    nb = D_MODEL // tn
    row = pl.BlockSpec((TM, tn), lambda n, m: (m, n))
    wscratch = pltpu.VMEM((D_MODEL, tn), BF16)
    return pl.pallas_call(
        _sc_in_kernel,
        grid=(nb, T_ALL // TM),
        in_specs=[
            pl.BlockSpec((TM, D_MODEL), lambda n, m: (m, 0)),
            pl.BlockSpec((None, D_MODEL, tn), lambda n, m: (j, 0, n)),
            pl.BlockSpec((None, D_MODEL, tn), lambda n, m: (j, 0, n + nb)),
            pl.BlockSpec((None, D_MODEL, tn), lambda n, m: (j, 0, n + 2 * nb)),
        ],
        out_specs=[row, row],
        out_shape=[jax.ShapeDtypeStruct((T_ALL, D_MODEL), F32)] * 2,
        scratch_shapes=[wscratch, wscratch, wscratch],
        compiler_params=_cparams(("parallel", "arbitrary")),
        name="sc_in_proj",
    )(xb, w_in, w_in, w_in)


CONV_TILE = 128
CONV_CB = 2048
HALO = 8


def _conv_kernel(*refs, taps, has_bias, has_gate, act):
    u_ref, halo_ref = refs[0], refs[1]
    ov_refs = refs[2:2 + taps - 1]
    w_ref = refs[2 + taps - 1]
    pos = 3 + taps - 1
    bias_ref = gate_ref = None
    if has_bias:
        bias_ref = refs[pos]
        pos += 1
    if has_gate:
        gate_ref = refs[pos]
        pos += 1
    o_ref = refs[pos]

    i = pl.program_id(0)
    first_samp_tile = T_MAIN // CONV_TILE

    def body(in_samples):
        u = u_ref[...]
        halo = jnp.where(i > 0, halo_ref[...], 0.0)
        lrow = lax.broadcasted_iota(jnp.int32, (CONV_TILE, 1), 0)
        spos = jnp.bitwise_and(lrow, DEC_SEQ - 1)
        w = w_ref[...]
        acc = w[taps - 1:taps, :] * u
        for d in range(1, taps):
            ud = pltpu.roll(u, d, axis=0)
            if in_samples:
                ud = jnp.where(spos < d, ov_refs[d - 1][...], ud)
            else:
                for r in range(d):
                    ud = jnp.where(lrow == r, halo[HALO - d + r:HALO - d + r + 1, :], ud)
            acc = acc + w[taps - 1 - d:taps - d, :] * ud
        if has_bias:
            acc = acc + bias_ref[...]
        if act:
            acc = _silu(acc)
        if has_gate:
            acc = acc * gate_ref[...]
        o_ref[...] = acc.astype(o_ref.dtype)

    @pl.when(i < first_samp_tile)
    def _():
        body(False)

    @pl.when(i >= first_samp_tile)
    def _():
        body(True)


def _conv_overrides(prev, taps):
    c = prev.shape[-1]
    out = []
    for d in range(1, taps):
        o = jnp.zeros((DEC_BATCH, DEC_SEQ, c), F32)
        for p in range(d):
            o = o.at[:, p].set(prev[:, taps - 1 + p - d])
        out.append(o.reshape(N_SAMP, c))
    return out


def _conv(u, col_off, n_ch, prev, w, bias, gate, act, out_dtype):
    taps = w.shape[0]
    ncb = n_ch // CONV_CB
    cb0 = col_off // CONV_CB
    tiles_per_halo = CONV_TILE // HALO
    samp_tile0 = T_MAIN // CONV_TILE
    ovs = _conv_overrides(prev, taps)
    in_specs = [
        pl.BlockSpec((CONV_TILE, CONV_CB), lambda i, c: (i, cb0 + c)),
        pl.BlockSpec((HALO, CONV_CB), lambda i, c: (jnp.maximum(i * tiles_per_halo - 1, 0), cb0 + c)),
    ]
    args = [u, u]
    for o in ovs:
        in_specs.append(pl.BlockSpec((CONV_TILE, CONV_CB), lambda i, c: (jnp.maximum(i - samp_tile0, 0), c)))
        args.append(o)
    in_specs.append(pl.BlockSpec((taps, CONV_CB), lambda i, c: (0, c)))
    args.append(w)
    if bias is not None:
        in_specs.append(pl.BlockSpec((1, CONV_CB), lambda i, c: (0, c)))
        args.append(bias.reshape(1, n_ch))
    if gate is not None:
        in_specs.append(pl.BlockSpec((CONV_TILE, CONV_CB), lambda i, c: (i, c)))
        args.append(gate)
    return pl.pallas_call(
        functools.partial(_conv_kernel, taps=taps, has_bias=bias is not None,
                          has_gate=gate is not None, act=act),
        grid=(T_ALL // CONV_TILE, ncb),
        in_specs=in_specs,
        out_specs=pl.BlockSpec((CONV_TILE, CONV_CB), lambda i, c: (i, c)),
        out_shape=jax.ShapeDtypeStruct((T_ALL, n_ch), out_dtype),
        compiler_params=_cparams(("parallel", "parallel")),
        name="causal_conv",
    )(*args)


def _proj_ln_kernel(y_ref, w_ref, h_ref, g_ref, b_ref, o_ref, *, nk):
    rs = TM // PROJ_ROW_SPLIT

    def slab_dot(rows):
        return jnp.dot(y_ref[rows, :], w_ref[...], preferred_element_type=F32)

    def finish(rows, acc):
        v = ALPHA * h_ref[rows, :] + acc
        o_ref[rows, :] = _layer_norm(v, g_ref[...], b_ref[...])

    slabs = [pl.ds(s * rs, rs) for s in range(PROJ_ROW_SPLIT)]
    if nk == 1:
        for rows in slabs:
            finish(rows, slab_dot(rows))
        return

    k = pl.program_id(1)

    @pl.when(k == 0)
    def _():
        for rows in slabs:
            o_ref[rows, :] = slab_dot(rows)

    @pl.when(k == 1)
    def _():
        for rows in slabs:
            finish(rows, o_ref[rows, :] + slab_dot(rows))


def _proj_ln(y, w, j, h, lng, lnb, i, k):
    kdim = w.shape[1]
    tk = 2048
    nk = kdim // tk
    assert nk in (1, 2)
    return pl.pallas_call(
        functools.partial(_proj_ln_kernel, nk=nk),
        grid=(T_ALL // TM, nk),
        in_specs=[
            pl.BlockSpec((TM, tk), lambda m, kk: (m, kk)),
            pl.BlockSpec((None, tk, D_MODEL), lambda m, kk: (j, kk, 0)),
            pl.BlockSpec((TM, D_MODEL), lambda m, kk: (m, 0)),
            pl.BlockSpec((None, None, 1, D_MODEL), lambda m, kk: (i, k, 0, 0)),
            pl.BlockSpec((None, None, 1, D_MODEL), lambda m, kk: (i, k, 0, 0)),
        ],
        out_specs=pl.BlockSpec((TM, D_MODEL), lambda m, kk: (m, 0)),
        out_shape=jax.ShapeDtypeStruct((T_ALL, D_MODEL), F32),
        compiler_params=_cparams(("parallel", "arbitrary")),
        name="out_proj_ln",
    )(y, w, h, lng, lnb)


_NT = (((1,), (1,)), ((), ()))
_TN = (((0,), (0,)), ((), ()))
SPLIT_K = 2 * LANES


def _bf16_pieces(v):
    hi = v.astype(BF16)
    r1 = v - hi.astype(F32)
    mid = r1.astype(BF16)
    lo = (r1 - mid.astype(F32)).astype(BF16)
    return hi, mid, lo


def _split_pack(v, lane_lo):
    hi, mid, lo = _bf16_pieces(v)
    a = jnp.where(lane_lo, hi.astype(F32), pltpu.roll(mid.astype(F32), HEADS, axis=1))
    b = jnp.where(lane_lo, lo.astype(F32), 0.0)
    return jnp.concatenate([a, b], axis=1).astype(BF16)


def _split_pack2(v, lane_lo):
    hi = v.astype(BF16)
    mid = (v - hi.astype(F32)).astype(BF16)
    return jnp.where(lane_lo, hi.astype(F32), pltpu.roll(mid.astype(F32), HEADS, axis=1)).astype(BF16)


def _masked_cumsum(mask_bf, da):
    hi, mid, lo = _bf16_pieces(da)
    p = jnp.dot(mask_bf, jnp.concatenate([hi, mid, lo], axis=1), preferred_element_type=F32)
    return p[:, :LANES] + p[:, LANES:2 * LANES] + p[:, 2 * LANES:]


def _conv_silu(buf_ref, base, n, cw, cbias):
    taps = SSD_CONV_WIDTH
    acc = cw[taps - 1:taps, :] * buf_ref[base:base + n, :]
    for d in range(1, taps):
        acc = acc + cw[taps - 1 - d:taps - d, :] * buf_ref[base - d:base - d + n, :]
    return _silu(acc + cbias)


def _ssd_prep_kernel(dt_ref, dtb_ref, al_ref, pk_acs_ref, pk3_ref, acst_ref, tott_ref):
    c = pl.program_id(0)
    li, si = _chunk_iotas()
    lane_lo = si < HEADS
    is_prompt = c < N_MAIN_CHUNKS
    same = jnp.logical_or(is_prompt, (li // DEC_SEQ) == (si // DEC_SEQ))
    causal = jnp.logical_and(si <= li, same)
    row = lax.broadcasted_iota(jnp.int32, (CHUNK, 1), 0) + c * CHUNK
    valid = jnp.logical_or(row < T_PROMPT, row >= T_MAIN)
    dt = jnp.where(jnp.logical_and(valid, lane_lo), _softplus(dt_ref[...] + dtb_ref[...]), 0.0)
    da = dt * (-jnp.exp(al_ref[...]))
    acs = _masked_cumsum(causal.astype(F32).astype(BF16), da)
    tot = _masked_cumsum(same.astype(F32).astype(BF16), da)
    pk_acs_ref[...] = _split_pack(acs, lane_lo)
    pk3_ref[...] = jnp.concatenate([_split_pack2(dt, lane_lo), _split_pack2(jnp.exp(tot - acs), lane_lo),
                                    _split_pack2(jnp.exp(acs), lane_lo)], axis=1)
    acst_ref[...] = acs.T
    tott_ref[...] = tot.T


def _ssd_prep(zxd, dt_bias, a_log):
    n_chunks = T_ALL // CHUNK
    tr = pl.BlockSpec((None, LANES, CHUNK), lambda c: (c, 0, 0))
    return pl.pallas_call(
        _ssd_prep_kernel,
        grid=(n_chunks,),
        in_specs=[
            pl.BlockSpec((CHUNK, LANES), lambda c: (c, DT_COL // LANES)),
            pl.BlockSpec((1, LANES), lambda c: (0, 0)),
            pl.BlockSpec((1, LANES), lambda c: (0, 0)),
        ],
        out_specs=[
            pl.BlockSpec((CHUNK, SPLIT_K), lambda c: (c, 0)),
            pl.BlockSpec((CHUNK, 3 * LANES), lambda c: (c, 0)),
            tr, tr,
        ],
        out_shape=[
            jax.ShapeDtypeStruct((T_ALL, SPLIT_K), BF16),
            jax.ShapeDtypeStruct((T_ALL, 3 * LANES), BF16),
            jax.ShapeDtypeStruct((n_chunks, LANES, CHUNK), F32),
            jax.ShapeDtypeStruct((n_chunks, LANES, CHUNK), F32),
        ],
        compiler_params=_cparams(("parallel",)),
        name="ssd_prep",
    )(zxd, dt_bias, a_log)


def _intra_chunk(xbc, pk_acs, pk3, arow, causal, lane_lo, r_ref, e_ref):
    xs = xbc[:, :GROUP_W]
    bmb = xbc[:, GROUP_W:GROUP_W + STATE].astype(BF16)
    cmb = xbc[:, GROUP_W + STATE:].astype(BF16)
    colmat = jnp.dot(pk_acs, r_ref[...], preferred_element_type=F32)
    packed = jnp.concatenate([pk3[:, :LANES], pk3[:, LANES:2 * LANES], pk3[:, 2 * LANES:]], axis=0)
    ex = jnp.dot(packed, e_ref[...], preferred_element_type=F32)
    dtx, tex, eax = ex[:CHUNK], ex[CHUNK:2 * CHUNK], ex[2 * CHUNK:]
    xdt = xs * dtx
    cb = lax.dot_general(cmb, bmb, _NT, preferred_element_type=F32)
    ys = []
    for q in range(HEADS_PER_GROUP // 2):
        scs = []
        for r in (2 * q, 2 * q + 1):
            seg = colmat[:, r * CHUNK:(r + 1) * CHUNK] - arow[r:r + 1, :]
            scs.append((cb * jnp.exp(jnp.where(causal, seg, -jnp.inf))).astype(BF16))
        xp = xdt[:, q * LANES:(q + 1) * LANES]
        rhs = jnp.concatenate([jnp.where(lane_lo, xp, 0.0).astype(BF16),
                               jnp.where(lane_lo, 0.0, xp).astype(BF16)], axis=0)
        ys.append(jnp.dot(jnp.concatenate(scs, axis=1), rhs, preferred_element_type=F32))
    y_intra = jnp.concatenate(ys, axis=1)
    xw = (xdt * tex).astype(BF16)
    return xs, bmb, cmb, y_intra, eax, xw


def _gate_norm(y, z, nw):
    y = y * _silu(z)
    ms = jnp.mean(y * y, axis=-1, keepdims=True)
    return (y * lax.rsqrt(ms + RMS_EPS) * nw).astype(BF16)


def _chunk_iotas():
    li = lax.broadcasted_iota(jnp.int32, (CHUNK, CHUNK), 0)
    si = lax.broadcasted_iota(jnp.int32, (CHUNK, CHUNK), 1)
    return li, si


def _ssd_main_kernel(z_ref, xr_ref, br_ref, cr_ref, pk_acs_ref, pk3_ref, acst_ref, tott_ref, cw_ref, cbias_ref,
                     dx_ref, nw_ref, r_ref, e_ref, y_ref, sfin_ref, buf_ref, st_ref):
    del tott_ref
    g = pl.program_id(0)
    rb = pl.program_id(1)

    @pl.when(rb == 0)
    def _():
        st_ref[...] = jnp.zeros_like(st_ref)
        buf_ref[0:CONV_HEAD, :] = jnp.zeros((CONV_HEAD, XBC_W), F32)

    @pl.when(rb > 0)
    def _():
        buf_ref[0:CONV_HEAD, :] = buf_ref[SSD_RB:SSD_RB + CONV_HEAD, :]

    buf_ref[CONV_HEAD:, 0:GROUP_W] = xr_ref[...]
    buf_ref[CONV_HEAD:, GROUP_W:GROUP_W + STATE] = br_ref[...]
    buf_ref[CONV_HEAD:, GROUP_W + STATE:] = cr_ref[...]

    li, si = _chunk_iotas()
    causal = si <= li
    lane_lo = si < HEADS
    g8 = pl.multiple_of(g * HEADS_PER_GROUP, SUBLANES)
    cw = cw_ref[...]
    cbias = cbias_ref[...]

    for k in range(SSD_NB):
        rows = pl.ds(k * CHUNK, CHUNK)
        xbc = _conv_silu(buf_ref, CONV_HEAD + k * CHUNK, CHUNK, cw, cbias)
        arow = acst_ref[k, pl.ds(g8, HEADS_PER_GROUP), :]
        xs, bmb, cmb, y_intra, eax, xw = _intra_chunk(xbc, pk_acs_ref[rows, :], pk3_ref[rows, :], arow,
                                                      causal, lane_lo, r_ref, e_ref)
        st = st_ref[...]
        y = y_intra + jnp.dot(cmb, st.astype(BF16), preferred_element_type=F32) * eax + dx_ref[...] * xs
        y_ref[rows, :] = _gate_norm(y, z_ref[rows, :], nw_ref[...])
        st_ref[...] = st * eax[CHUNK - 1:CHUNK, :] + lax.dot_general(bmb, xw, _TN, preferred_element_type=F32)

    @pl.when(rb == T_MAIN // SSD_RB - 1)
    def _():
        sfin_ref[...] = st_ref[...]


def _ssd_samp_kernel(z_ref, xr_ref, br_ref, cr_ref, pk_acs_ref, pk3_ref, acst_ref, tott_ref, cw_ref, cbias_ref,
                     dx_ref, nw_ref, r_ref, e_ref, prev_ref, s0_ref, yprev_ref, y_ref, s1_ref,
                     buf_ref, xbc_ref, yint_ref):
    del yprev_ref
    g = pl.program_id(0)
    slot = CONV_HEAD + DEC_SEQ
    cw = cw_ref[...]
    cbias = cbias_ref[...]
    for q in range(SEQ_PER_CHUNK):
        r0 = q * DEC_SEQ
        top = q * slot + CONV_HEAD
        buf_ref[top - (SSD_CONV_WIDTH - 1):top, :] = prev_ref[q]
        buf_ref[top:top + DEC_SEQ, 0:GROUP_W] = xr_ref[r0:r0 + DEC_SEQ, :]
        buf_ref[top:top + DEC_SEQ, GROUP_W:GROUP_W + STATE] = br_ref[r0:r0 + DEC_SEQ, :]
        buf_ref[top:top + DEC_SEQ, GROUP_W + STATE:] = cr_ref[r0:r0 + DEC_SEQ, :]
    for q in range(SEQ_PER_CHUNK):
        r0 = q * DEC_SEQ
        xbc_ref[r0:r0 + DEC_SEQ, :] = _conv_silu(buf_ref, q * slot + CONV_HEAD, DEC_SEQ, cw, cbias)

    li, si = _chunk_iotas()
    same = (li // DEC_SEQ) == (si // DEC_SEQ)
    causal = jnp.logical_and(si <= li, same)
    lane_lo = si < HEADS
    g8 = pl.multiple_of(g * HEADS_PER_GROUP, SUBLANES)

    arow = acst_ref[0, pl.ds(g8, HEADS_PER_GROUP), :]
    trow = tott_ref[0, pl.ds(g8, HEADS_PER_GROUP), :]
    xs, bmb, cmb, y_intra, eax, xw = _intra_chunk(xbc_ref[...], pk_acs_ref[...], pk3_ref[...], arow,
                                                  causal, lane_lo, r_ref, e_ref)
    for q in range(SEQ_PER_CHUNK):
        r0 = q * DEC_SEQ
        yint_ref[r0:r0 + DEC_SEQ, :] = lax.dot_general(cmb[r0:r0 + DEC_SEQ, :], s0_ref[q].astype(BF16), _NT,
                                                       preferred_element_type=F32)
    y = y_intra + yint_ref[...] * eax + dx_ref[...] * xs
    y_ref[...] = _gate_norm(y, z_ref[...], nw_ref[...])
    for q in range(SEQ_PER_CHUNK):
        r0 = q * DEC_SEQ
        upd = lax.dot_general(xw[r0:r0 + DEC_SEQ, :], bmb[r0:r0 + DEC_SEQ, :], _TN, preferred_element_type=F32)
        for r in range(HEADS_PER_GROUP):
            lo, hi = r * HEADDIM, (r + 1) * HEADDIM
            s1_ref[q, lo:hi, :] = s0_ref[q, lo:hi, :] * jnp.exp(trow[r:r + 1, r0:r0 + 1]) + upd[lo:hi, :]


def _ssd_specs(rows, row0):
    xcol = D_INNER // GROUP_W
    bcol = (2 * D_INNER) // STATE
    return [
        pl.BlockSpec((rows, GROUP_W), lambda g, c: (row0 + c, g)),
        pl.BlockSpec((rows, GROUP_W), lambda g, c: (row0 + c, xcol + g)),
        pl.BlockSpec((rows, STATE), lambda g, c: (row0 + c, bcol + g)),
        pl.BlockSpec((rows, STATE), lambda g, c: (row0 + c, bcol + GROUPS + g)),
        pl.BlockSpec((rows, SPLIT_K), lambda g, c: (row0 + c, 0)),
        pl.BlockSpec((rows, 3 * LANES), lambda g, c: (row0 + c, 0)),
        pl.BlockSpec((rows // CHUNK, LANES, CHUNK), lambda g, c: (row0 + c, 0, 0)),
        pl.BlockSpec((rows // CHUNK, LANES, CHUNK), lambda g, c: (row0 + c, 0, 0)),
        pl.BlockSpec((None, SSD_CONV_WIDTH, XBC_W), lambda g, c: (g, 0, 0)),
        pl.BlockSpec((None, 1, XBC_W), lambda g, c: (g, 0, 0)),
        pl.BlockSpec((1, GROUP_W), lambda g, c: (0, g)),
        pl.BlockSpec((1, GROUP_W), lambda g, c: (0, g)),
        pl.BlockSpec((None, SPLIT_K, HEADS_PER_GROUP * CHUNK), lambda g, c: (g, 0, 0)),
        pl.BlockSpec((None, LANES, GROUP_W), lambda g, c: (g, 0, 0)),
    ]


def _spread_matrices():
    k = np.arange(SPLIT_K)
    head = k % HEADS
    used = (k // HEADS) < 3
    g = np.arange(GROUPS)[:, None, None]
    rj = np.arange(HEADS_PER_GROUP * CHUNK)[None, None, :] // CHUNK
    ej = np.arange(GROUP_W)[None, None, :] // HEADDIM
    hk = head[None, :, None]
    uk = used[None, :, None]
    r = (uk & (hk == g * HEADS_PER_GROUP + rj)).astype(np.float32)
    e = (hk == g * HEADS_PER_GROUP + ej)[:, :LANES].astype(np.float32)
    return jnp.asarray(r, BF16), jnp.asarray(e, BF16)


def _per_group(a):
    lead = a.shape[:-1]
    x = a[..., :D_INNER].reshape(*lead, GROUPS, GROUP_W)
    b = a[..., D_INNER:D_INNER + GROUPS * STATE].reshape(*lead, GROUPS, STATE)
    c = a[..., D_INNER + GROUPS * STATE:].reshape(*lead, GROUPS, STATE)
    return jnp.moveaxis(jnp.concatenate([x, b, c], axis=-1), -2, 0)


def _ssd(zxd, conv_w, conv_b, dt_bias, a_log, d_skip, norm_w, conv_prev, state0):
    r_mat, e_mat = _spread_matrices()
    pad = ((0, 0), (0, LANES - HEADS))
    prep = _ssd_prep(zxd, jnp.pad(dt_bias.reshape(1, HEADS).astype(F32), pad),
                     jnp.pad(a_log.reshape(1, HEADS).astype(F32), pad))
    params = (_per_group(conv_w), _per_group(conv_b.reshape(1, CONV_DIM)),
              jnp.repeat(d_skip.astype(F32), HEADDIM).reshape(1, D_INNER),
              norm_w.reshape(1, D_INNER).astype(F32), r_mat, e_mat)
    common = (zxd,) * 4 + tuple(prep) + params
    y, s_fin = pl.pallas_call(
        _ssd_main_kernel,
        grid=(GROUPS, T_MAIN // SSD_RB),
        in_specs=_ssd_specs(SSD_RB, 0),
        out_specs=[
            pl.BlockSpec((SSD_RB, GROUP_W), lambda g, c: (c, g)),
            pl.BlockSpec((None, STATE, GROUP_W), lambda g, c: (g, 0, 0)),
        ],
        out_shape=[
            jax.ShapeDtypeStruct((T_ALL, D_INNER), BF16),
            jax.ShapeDtypeStruct((GROUPS, STATE, GROUP_W), F32),
        ],
        scratch_shapes=[
            pltpu.VMEM((CONV_HEAD + SSD_RB, XBC_W), F32),
            pltpu.VMEM((STATE, GROUP_W), F32),
        ],
        compiler_params=_cparams(("parallel", "arbitrary")),
        name="ssd_prompt",
    )(*common)

    n_in = len(common)
    y, s_new = pl.pallas_call(
        _ssd_samp_kernel,
        grid=(GROUPS, N_SAMP_CHUNKS),
        in_specs=_ssd_specs(CHUNK, N_MAIN_CHUNKS) + [
            pl.BlockSpec((None, SEQ_PER_CHUNK, SSD_CONV_WIDTH - 1, XBC_W), lambda g, c: (g, c, 0, 0)),
            pl.BlockSpec((SEQ_PER_CHUNK, None, GROUP_W, STATE), lambda g, c: (c, g, 0, 0)),
            pl.BlockSpec(memory_space=pl.ANY),
        ],
        out_specs=[
            pl.BlockSpec((CHUNK, GROUP_W), lambda g, c: (N_MAIN_CHUNKS + c, g)),
            pl.BlockSpec((SEQ_PER_CHUNK, None, GROUP_W, STATE), lambda g, c: (c, g, 0, 0)),
        ],
        out_shape=[
            jax.ShapeDtypeStruct((T_ALL, D_INNER), BF16),
            jax.ShapeDtypeStruct((DEC_BATCH, GROUPS, GROUP_W, STATE), F32),
        ],
        scratch_shapes=[
            pltpu.VMEM((SEQ_PER_CHUNK * (CONV_HEAD + DEC_SEQ), XBC_W), F32),
            pltpu.VMEM((CHUNK, XBC_W), F32),
            pltpu.VMEM((CHUNK, GROUP_W), F32),
        ],
        input_output_aliases={n_in + 2: 0},
        compiler_params=_cparams(("parallel", "arbitrary")),
        name="ssd_sample",
    )(*common, _per_group(conv_prev), state0, y)
    return y, s_fin, s_new


def _tail_rows(a, n):
    c = a.shape[-1]
    p = a[T_PROMPT - n:T_PROMPT].reshape(1, 1, n, c)
    s = a[T_MAIN:].reshape(DEC_BATCH, DEC_SEQ, c)[:, DEC_SEQ - n:].reshape(1, DEC_BATCH, n, c)
    return p, s


def kernel(x_prompt, x_sample, cache_sc_conv, state_ssd_conv, state_ssd, meta_tokens, ln_g, ln_b,
           ffn_w1, ffn_w3, ffn_w2, sc_w_in, sc_w_conv, sc_w_out,
           ssd_w_in, ssd_w_conv, ssd_b_conv, ssd_dt_bias, ssd_a_log, ssd_d, ssd_norm_w, ssd_w_out):
    w1, w3, w2 = ffn_w1, ffn_w3, ffn_w2
    sc_out_w = sc_w_out.astype(BF16)
    ssd_out_w = ssd_w_out.astype(BF16)
    lng = ln_g.reshape(DEPTH, 3, 1, D_MODEL)
    lnb = ln_b.reshape(DEPTH, 3, 1, D_MODEL)

    h = jnp.concatenate([meta_tokens.astype(F32), x_prompt[0], jnp.zeros((N_PAD, D_MODEL), F32),
                         x_sample.reshape(N_SAMP, D_MODEL)], axis=0)

    h, hb = _ffn(h, w1, w3, w2, lng, lnb, 0, 0, 0, emit_bf16=True)
    bgate, u = _sc_in(hb, sc_w_in, 0)
    v = _conv(u, 0, D_MODEL, cache_sc_conv[0], sc_w_conv[0], None, bgate, False, BF16)
    new_sc_p, new_sc_s = _tail_rows(u, SC_WIDTH - 1)
    h = _proj_ln(v, sc_out_w, 0, h, lng, lnb, 0, 1)
    h = _ffn(h, w1, w3, w2, lng, lnb, 0, 1, 2)

    h, hb = _ffn(h, w1, w3, w2, lng, lnb, 1, 0, 0, emit_bf16=True)
    zxd = _mm(hb, jnp.swapaxes(ssd_w_in[0], 0, 1), ZXD_W, ZXD_TN)
    new_conv_p, new_conv_s = _tail_rows(zxd[:, D_INNER:DT_COL], SSD_CONV_WIDTH - 1)
    state0 = state_ssd[0].reshape(DEC_BATCH, GROUPS, GROUP_W, STATE)
    y, s_fin, s_new = _ssd(zxd, ssd_w_conv[0], ssd_b_conv[0], ssd_dt_bias[0], ssd_a_log[0], ssd_d[0],
                           ssd_norm_w[0], state_ssd_conv[0], state0)
    h = _proj_ln(y, ssd_out_w, 0, h, lng, lnb, 1, 1)
    h = _ffn(h, w1, w3, w2, lng, lnb, 1, 1, 2)

    y_prompt = h[N_META:T_PROMPT].reshape(1, SEQ, D_MODEL)
    y_sample = h[T_MAIN:].reshape(DEC_BATCH, DEC_SEQ, D_MODEL)
    new_ssd_p = jnp.swapaxes(s_fin, 1, 2).reshape(1, 1, HEADS, HEADDIM, STATE).astype(state_ssd.dtype)
    new_ssd_s = s_new.reshape(1, DEC_BATCH, HEADS, HEADDIM, STATE).astype(state_ssd.dtype)
    return (y_prompt, y_sample, new_sc_p, new_sc_s, new_conv_p, new_conv_s, new_ssd_p, new_ssd_s)
```

```python
import functools

import numpy as np

import jax
import jax.numpy as jnp
from jax import lax
from jax.experimental import pallas as pl
from jax.experimental.pallas import tpu as pltpu

F32 = jnp.float32
BF16 = jnp.bfloat16

D_MODEL = 2048
SEQ = 8192
DEPTH = 2
DEC_BATCH = 32
DEC_SEQ = 16
N_META = 16
D_FF = 5632
SC_WIDTH = 3
D_INNER = 4096
HEADDIM = 64
HEADS = 64
GROUPS = 8
HEADS_PER_GROUP = 8
STATE = 128
SSD_CONV_WIDTH = 4
CONV_DIM = D_INNER + 2 * GROUPS * STATE
GROUP_W = HEADS_PER_GROUP * HEADDIM
XBC_W = GROUP_W + 2 * STATE
ALPHA = (2.0 * DEPTH) ** 0.25
LN_EPS = 1e-5
RMS_EPS = 1e-5

CHUNK = 128
T_PROMPT = N_META + SEQ
N_MAIN_CHUNKS = -(-T_PROMPT // CHUNK)
T_MAIN = N_MAIN_CHUNKS * CHUNK
N_PAD = T_MAIN - T_PROMPT
N_SAMP = DEC_BATCH * DEC_SEQ
N_SAMP_CHUNKS = N_SAMP // CHUNK
SEQ_PER_CHUNK = CHUNK // DEC_SEQ
T_ALL = T_MAIN + N_SAMP

TM = 736
TM_IN = 2 * TM
TF = 512
TF_HEAD = 256
FFN_ROW_SPLIT = 2
PROJ_ROW_SPLIT = 2
SC_ROW_SPLIT = 2
ZXD_W = 10368
ZXD_TN = 1152
DT_COL = D_INNER + CONV_DIM
SSD_NB = 5
SSD_RB = SSD_NB * CHUNK
PREP_NB = 3
LANES = 128
SUBLANES = 8
CONV_HEAD = SUBLANES

VMEM_LIMIT_BYTES = 56 * 1024 * 1024


def _cparams(sem):
    return pltpu.CompilerParams(dimension_semantics=sem, vmem_limit_bytes=VMEM_LIMIT_BYTES)


def _layer_norm(v, g, b):
    mu = jnp.mean(v, axis=-1, keepdims=True)
    c = v - mu
    var = jnp.mean(c * c, axis=-1, keepdims=True)
    return c * lax.rsqrt(var + LN_EPS) * g + b


def _silu(x):
    return x * jax.nn.sigmoid(x)


def _softplus(x):
    return jnp.maximum(x, 0.0) + jnp.log(1.0 + jnp.exp(-jnp.abs(x)))


def _ffn_kernel(*refs, nf, n_alias, emit_bf16, convert):
    x_ref, w1_ref, w3_ref, w2_ref, g_ref, b_ref = refs[:6]
    outs = refs[6 + n_alias:]
    o_ref = outs[0]
    pos = 1
    ob_ref = None
    if emit_bf16:
        ob_ref = outs[pos]
        pos += 1
    if convert:
        wb_refs = outs[pos:pos + 3]
        for src, dst in zip((w1_ref, w3_ref, w2_ref), wb_refs):
            dst[...] = src[...].astype(BF16)
        w1_ref, w3_ref, w2_ref = wb_refs
        pos += 3
    xb_ref = outs[pos]
    f = pl.program_id(1)

    @pl.when(f == 0)
    def _():
        xb_ref[...] = x_ref[...].astype(BF16)
        o_ref[...] = jnp.zeros_like(o_ref)

    rs = TM // FFN_ROW_SPLIT

    def slab(s, last):
        rows = pl.ds(s * rs, rs)
        xb = xb_ref[rows, :]
        h1 = jnp.dot(xb, w1_ref[...], preferred_element_type=F32)
        h3 = jnp.dot(xb, w3_ref[...], preferred_element_type=F32)
        gate = (_silu(h1) * h3).astype(BF16)
        acc = o_ref[rows, :] + jnp.dot(gate, w2_ref[...], preferred_element_type=F32)
        if last:
            res = _layer_norm(ALPHA * x_ref[rows, :] + 0.5 * acc, g_ref[...], b_ref[...])
            o_ref[rows, :] = res
            if ob_ref is not None:
                ob_ref[rows, :] = res.astype(BF16)
        else:
            o_ref[rows, :] = acc

    @pl.when(f < nf - 1)
    def _():
        for s in range(FFN_ROW_SPLIT):
            slab(s, False)

    @pl.when(f == nf - 1)
    def _():
        for s in range(FFN_ROW_SPLIT):
            slab(s, True)


def _ffn(h, w1, w3, w2, lng, lnb, i, j, k, emit_bf16=False):
    ln_spec = pl.BlockSpec((None, None, 1, D_MODEL), lambda m, f: (i, k, 0, 0))
    h_shapes = [jax.ShapeDtypeStruct((T_ALL, D_MODEL), F32)]
    if emit_bf16:
        h_shapes.append(jax.ShapeDtypeStruct((T_ALL, D_MODEL), BF16))
    n_h = len(h_shapes)
    scratch = [pltpu.VMEM((TM, D_MODEL), BF16)]

    head_row = pl.BlockSpec((TM, D_MODEL), lambda m, f: (0, 0))
    nf = D_FF // TF
    nf_head = D_FF // TF_HEAD
    per = TF // TF_HEAD
    head = pl.pallas_call(
        functools.partial(_ffn_kernel, nf=nf_head, n_alias=0, emit_bf16=emit_bf16, convert=True),
        grid=(1, nf_head),
        in_specs=[
            head_row,
            pl.BlockSpec((None, None, D_MODEL, TF_HEAD), lambda m, f: (i, j, 0, f)),
            pl.BlockSpec((None, None, D_MODEL, TF_HEAD), lambda m, f: (i, j, 0, f)),
            pl.BlockSpec((None, None, TF_HEAD, D_MODEL), lambda m, f: (i, j, f, 0)),
            ln_spec, ln_spec,
        ],
        out_specs=[head_row] * n_h + [
            pl.BlockSpec((None, D_MODEL, TF_HEAD), lambda m, f: (f // per, 0, f % per)),
            pl.BlockSpec((None, D_MODEL, TF_HEAD), lambda m, f: (f // per, 0, f % per)),
            pl.BlockSpec((TF_HEAD, D_MODEL), lambda m, f: (f, 0)),
        ],
        out_shape=h_shapes + [
            jax.ShapeDtypeStruct((nf, D_MODEL, TF), BF16),
            jax.ShapeDtypeStruct((nf, D_MODEL, TF), BF16),
            jax.ShapeDtypeStruct((D_FF, D_MODEL), BF16),
        ],
        scratch_shapes=scratch,
        compiler_params=_cparams(("parallel", "arbitrary")),
        name="ffn_ln_head",
    )(h, w1, w3, w2, lng, lnb)
    h_parts, (w1b, w3b, w2b) = head[:n_h], head[n_h:]

    row = pl.BlockSpec((TM, D_MODEL), lambda m, f: (m + 1, 0))
    out = pl.pallas_call(
        functools.partial(_ffn_kernel, nf=nf, n_alias=n_h, emit_bf16=emit_bf16, convert=False),
        grid=(T_ALL // TM - 1, nf),
        in_specs=[
            row,
            pl.BlockSpec((None, D_MODEL, TF), lambda m, f: (f, 0, 0)),
            pl.BlockSpec((None, D_MODEL, TF), lambda m, f: (f, 0, 0)),
            pl.BlockSpec((TF, D_MODEL), lambda m, f: (f, 0)),
            ln_spec, ln_spec,
        ] + [pl.BlockSpec(memory_space=pl.ANY)] * n_h,
        out_specs=[row] * n_h,
        out_shape=h_shapes,
        scratch_shapes=scratch,
        input_output_aliases={6 + a: a for a in range(n_h)},
        compiler_params=_cparams(("parallel", "arbitrary")),
        name="ffn_ln",
    )(h, w1b, w3b, w2b, lng, lnb, *h_parts)
    return tuple(out) if emit_bf16 else out[0]


def _mm_kernel(x_ref, wt_ref, o_ref, wb_ref, *, n_valid):
    j = pl.program_id(0)
    tn = wt_ref.shape[0]

    @pl.when(pl.program_id(1) == 0)
    def _():
        row = lax.broadcasted_iota(jnp.int32, (tn, 1), 0) + j * tn
        wb_ref[...] = jnp.where(row < n_valid, wt_ref[...], 0.0).astype(BF16)

    o_ref[...] = lax.dot_general(x_ref[...], wb_ref[...], (((1,), (1,)), ((), ())),
                                 preferred_element_type=F32)


def _mm(xb, wt, n_out, tn):
    n, k = wt.shape
    return pl.pallas_call(
        functools.partial(_mm_kernel, n_valid=n),
        grid=(n_out // tn, T_ALL // TM_IN),
        in_specs=[
            pl.BlockSpec((TM_IN, k), lambda j, m: (m, 0)),
            pl.BlockSpec((tn, k), lambda j, m: (j, 0)),
        ],
        out_specs=pl.BlockSpec((TM_IN, tn), lambda j, m: (m, j)),
        out_shape=jax.ShapeDtypeStruct((T_ALL, n_out), F32),
        scratch_shapes=[pltpu.VMEM((tn, k), BF16)],
        compiler_params=_cparams(("parallel", "arbitrary")),
        name="in_proj",
    )(xb, wt)


SC_TAIL0 = T_ALL - TM
SC_SAMP_OFF = T_MAIN - SC_TAIL0
assert SC_SAMP_OFF % DEC_SEQ == 0 and T_PROMPT - (SC_WIDTH - 1) >= SC_TAIL0


def _sc_mix_kernel(x_ref, wb_ref, wc_ref, wh_ref, cw_ref, o1_ref, o2_ref, v_ref, ut_ref,
                   wbb_ref, wcb_ref, whb_ref, ubuf_ref, *, nm):
    m = pl.program_id(1)
    tn = ubuf_ref.shape[1]

    @pl.when(m == 0)
    def _():
        wbb_ref[...] = wb_ref[...].astype(BF16)
        wcb_ref[...] = wc_ref[...].astype(BF16)
        whb_ref[...] = wh_ref[...].astype(BF16)
        ubuf_ref[0:CONV_HEAD, :] = jnp.zeros((CONV_HEAD, tn), F32)

    cw = cw_ref[...]
    ov_refs = (o1_ref, o2_ref)
    rs = TM // SC_ROW_SPLIT

    def slab(s, last_tile):
        r0 = s * rs
        xb = x_ref[r0:r0 + rs, :]
        bg = jnp.dot(xb, wbb_ref[...], preferred_element_type=F32)
        c = jnp.dot(xb, wcb_ref[...], preferred_element_type=F32)
        hh = jnp.dot(xb, whb_ref[...], preferred_element_type=F32)
        u = c * hh
        ubuf_ref[CONV_HEAD + r0:CONV_HEAD + r0 + rs, :] = u
        acc = cw[SC_WIDTH - 1:SC_WIDTH, :] * u
        if last_tile:
            lrow = lax.broadcasted_iota(jnp.int32, (rs, 1), 0) + r0
            spos = jnp.bitwise_and(lrow, DEC_SEQ - 1)
        for d in range(1, SC_WIDTH):
            ud = ubuf_ref[CONV_HEAD + r0 - d:CONV_HEAD + r0 - d + rs, :]
            if last_tile:
                first = jnp.logical_and(lrow >= SC_SAMP_OFF, spos < d)
                ud = jnp.where(first, ov_refs[d - 1][r0:r0 + rs, :], ud)
            acc = acc + cw[SC_WIDTH - 1 - d:SC_WIDTH - d, :] * ud
        v_ref[r0:r0 + rs, :] = (bg * acc).astype(BF16)
        if last_tile:
            ut_ref[r0:r0 + rs, :] = u

    @pl.when(m < nm - 1)
    def _():
        for s in range(SC_ROW_SPLIT):
            slab(s, False)

    @pl.when(m == nm - 1)
    def _():
        for s in range(SC_ROW_SPLIT):
            slab(s, True)

    ubuf_ref[0:CONV_HEAD, :] = ubuf_ref[TM:TM + CONV_HEAD, :]


def _sc_mix(xb, w_in, j, w_conv, conv_prev):
    tn = 512
    nb = D_MODEL // tn
    nm = T_ALL // TM
    ovs = [jnp.pad(o, ((SC_SAMP_OFF, 0), (0, 0))) for o in _conv_overrides(conv_prev, SC_WIDTH)]
    tail = pl.BlockSpec((TM, tn), lambda n, m: (0, n))
    wscratch = pltpu.VMEM((D_MODEL, tn), BF16)
    return pl.pallas_call(
        functools.partial(_sc_mix_kernel, nm=nm),
        grid=(nb, nm),
        in_specs=[
            pl.BlockSpec((TM, D_MODEL), lambda n, m: (m, 0)),
            pl.BlockSpec((None, D_MODEL, tn), lambda n, m: (j, 0, n)),
            pl.BlockSpec((None, D_MODEL, tn), lambda n, m: (j, 0, n + nb)),
            pl.BlockSpec((None, D_MODEL, tn), lambda n, m: (j, 0, n + 2 * nb)),
            pl.BlockSpec((SC_WIDTH, tn), lambda n, m: (0, n)),
            tail, tail,
        ],
        out_specs=[pl.BlockSpec((TM, tn), lambda n, m: (m, n)), tail],
        out_shape=[jax.ShapeDtypeStruct((T_ALL, D_MODEL), BF16),
                   jax.ShapeDtypeStruct((TM, D_MODEL), F32)],
        scratch_shapes=[wscratch, wscratch, wscratch, pltpu.VMEM((CONV_HEAD + TM, tn), F32)],
        compiler_params=_cparams(("parallel", "arbitrary")),
        name="sc_mix",
    )(xb, w_in, w_in, w_in, w_conv.astype(F32), *ovs)


def _conv_overrides(prev, taps):
    c = prev.shape[-1]
    out = []
    for d in range(1, taps):
        o = jnp.zeros((DEC_BATCH, DEC_SEQ, c), F32)
        for p in range(d):
            o = o.at[:, p].set(prev[:, taps - 1 + p - d])
        out.append(o.reshape(N_SAMP, c))
    return out


def _proj_ln_kernel(y_ref, w_ref, h_ref, g_ref, b_ref, o_ref, *, nk):
    rs = TM // PROJ_ROW_SPLIT

    def slab_dot(rows):
        return jnp.dot(y_ref[rows, :], w_ref[...], preferred_element_type=F32)

    def finish(rows, acc):
        v = ALPHA * h_ref[rows, :] + acc
        o_ref[rows, :] = _layer_norm(v, g_ref[...], b_ref[...])

    slabs = [pl.ds(s * rs, rs) for s in range(PROJ_ROW_SPLIT)]
    if nk == 1:
        for rows in slabs:
            finish(rows, slab_dot(rows))
        return

    k = pl.program_id(1)

    @pl.when(k == 0)
    def _():
        for rows in slabs:
            o_ref[rows, :] = slab_dot(rows)

    @pl.when(k == 1)
    def _():
        for rows in slabs:
            finish(rows, o_ref[rows, :] + slab_dot(rows))


def _proj_ln(y, w, j, h, lng, lnb, i, k):
    kdim = w.shape[1]
    tk = 2048
    nk = kdim // tk
    assert nk in (1, 2)
    return pl.pallas_call(
        functools.partial(_proj_ln_kernel, nk=nk),
        grid=(T_ALL // TM, nk),
        in_specs=[
            pl.BlockSpec((TM, tk), lambda m, kk: (m, kk)),
            pl.BlockSpec((None, tk, D_MODEL), lambda m, kk: (j, kk, 0)),
            pl.BlockSpec((TM, D_MODEL), lambda m, kk: (m, 0)),
            pl.BlockSpec((None, None, 1, D_MODEL), lambda m, kk: (i, k, 0, 0)),
            pl.BlockSpec((None, None, 1, D_MODEL), lambda m, kk: (i, k, 0, 0)),
        ],
        out_specs=pl.BlockSpec((TM, D_MODEL), lambda m, kk: (m, 0)),
        out_shape=jax.ShapeDtypeStruct((T_ALL, D_MODEL), F32),
        compiler_params=_cparams(("parallel", "arbitrary")),
        name="out_proj_ln",
    )(y, w, h, lng, lnb)


_NT = (((1,), (1,)), ((), ()))
_TN = (((0,), (0,)), ((), ()))
SPLIT_K = 2 * LANES


def _bf16_pieces(v):
    hi = v.astype(BF16)
    r1 = v - hi.astype(F32)
    mid = r1.astype(BF16)
    lo = (r1 - mid.astype(F32)).astype(BF16)
    return hi, mid, lo


def _split_pack(v, lane_lo):
    hi, mid, lo = _bf16_pieces(v)
    a = jnp.where(lane_lo, hi.astype(F32), pltpu.roll(mid.astype(F32), HEADS, axis=1))
    b = jnp.where(lane_lo, lo.astype(F32), 0.0)
    return jnp.concatenate([a, b], axis=1).astype(BF16)


def _split_pack2(v, lane_lo):
    hi = v.astype(BF16)
    mid = (v - hi.astype(F32)).astype(BF16)
    return jnp.where(lane_lo, hi.astype(F32), pltpu.roll(mid.astype(F32), HEADS, axis=1)).astype(BF16)


def _masked_cumsum(mask_bf, da):
    hi, mid, lo = _bf16_pieces(da)
    p = jnp.dot(mask_bf, jnp.concatenate([hi, mid, lo], axis=1), preferred_element_type=F32)
    return p[:, :LANES] + p[:, LANES:2 * LANES] + p[:, 2 * LANES:]


def _conv_silu(buf_ref, base, n, cw, cbias):
    taps = SSD_CONV_WIDTH
    acc = cw[taps - 1:taps, :] * buf_ref[base:base + n, :]
    for d in range(1, taps):
        acc = acc + cw[taps - 1 - d:taps - d, :] * buf_ref[base - d:base - d + n, :]
    return _silu(acc + cbias)


def _ssd_prep_kernel(dt_ref, dtb_ref, al_ref, pk_acs_ref, pk3_ref, acst_ref, tott_ref):
    li, si = _chunk_iotas()
    lane_lo = si < HEADS
    same_seq = (li // DEC_SEQ) == (si // DEC_SEQ)
    a_row = -jnp.exp(al_ref[...])
    for k in range(PREP_NB):
        c = pl.program_id(0) * PREP_NB + k
        rows = pl.ds(k * CHUNK, CHUNK)
        same = jnp.logical_or(c < N_MAIN_CHUNKS, same_seq)
        causal = jnp.logical_and(si <= li, same)
        row = lax.broadcasted_iota(jnp.int32, (CHUNK, 1), 0) + c * CHUNK
        valid = jnp.logical_or(row < T_PROMPT, row >= T_MAIN)
        dt = jnp.where(jnp.logical_and(valid, lane_lo), _softplus(dt_ref[rows, :] + dtb_ref[...]), 0.0)
        da = dt * a_row
        acs = _masked_cumsum(causal.astype(F32).astype(BF16), da)
        tot = _masked_cumsum(same.astype(F32).astype(BF16), da)
        pk_acs_ref[rows, :] = _split_pack(acs, lane_lo)
        pk3_ref[rows, :] = jnp.concatenate([_split_pack2(dt, lane_lo), _split_pack2(jnp.exp(tot - acs), lane_lo),
                                            _split_pack2(jnp.exp(acs), lane_lo)], axis=1)
        acst_ref[k] = acs.T
        tott_ref[k] = tot.T


def _ssd_prep(zxd, dt_bias, a_log):
    n_chunks = T_ALL // CHUNK
    rows = PREP_NB * CHUNK
    tr = pl.BlockSpec((PREP_NB, LANES, CHUNK), lambda c: (c, 0, 0))
    return pl.pallas_call(
        _ssd_prep_kernel,
        grid=(n_chunks // PREP_NB,),
        in_specs=[
            pl.BlockSpec((rows, LANES), lambda c: (c, DT_COL // LANES)),
            pl.BlockSpec((1, LANES), lambda c: (0, 0)),
            pl.BlockSpec((1, LANES), lambda c: (0, 0)),
        ],
        out_specs=[
            pl.BlockSpec((rows, SPLIT_K), lambda c: (c, 0)),
            pl.BlockSpec((rows, 3 * LANES), lambda c: (c, 0)),
            tr, tr,
        ],
        out_shape=[
            jax.ShapeDtypeStruct((T_ALL, SPLIT_K), BF16),
            jax.ShapeDtypeStruct((T_ALL, 3 * LANES), BF16),
            jax.ShapeDtypeStruct((n_chunks, LANES, CHUNK), F32),
            jax.ShapeDtypeStruct((n_chunks, LANES, CHUNK), F32),
        ],
        compiler_params=_cparams(("parallel",)),
        name="ssd_prep",
    )(zxd, dt_bias, a_log)


def _intra_chunk(xbc, pk_acs, pk3, arow, causal, lane_lo, r_ref, e_ref):
    xs = xbc[:, :GROUP_W]
    bmb = xbc[:, GROUP_W:GROUP_W + STATE].astype(BF16)
    cmb = xbc[:, GROUP_W + STATE:].astype(BF16)
    colmat = jnp.dot(pk_acs, r_ref[...], preferred_element_type=F32)
    packed = jnp.concatenate([pk3[:, :LANES], pk3[:, LANES:2 * LANES], pk3[:, 2 * LANES:]], axis=0)
    ex = jnp.dot(packed, e_ref[...], preferred_element_type=F32)
    dtx, tex, eax = ex[:CHUNK], ex[CHUNK:2 * CHUNK], ex[2 * CHUNK:]
    xdt = xs * dtx
    cb = lax.dot_general(cmb, bmb, _NT, preferred_element_type=F32)
    ys = []
    for q in range(HEADS_PER_GROUP // 2):
        scs = []
        for r in (2 * q, 2 * q + 1):
            seg = colmat[:, r * CHUNK:(r + 1) * CHUNK] - arow[r:r + 1, :]
            scs.append((cb * jnp.exp(jnp.where(causal, seg, -jnp.inf))).astype(BF16))
        xp = xdt[:, q * LANES:(q + 1) * LANES]
        rhs = jnp.concatenate([jnp.where(lane_lo, xp, 0.0).astype(BF16),
                               jnp.where(lane_lo, 0.0, xp).astype(BF16)], axis=0)
        ys.append(jnp.dot(jnp.concatenate(scs, axis=1), rhs, preferred_element_type=F32))
    y_intra = jnp.concatenate(ys, axis=1)
    xw = (xdt * tex).astype(BF16)
    return xs, bmb, cmb, y_intra, eax, xw


def _gate_norm(y, z, nw):
    y = y * _silu(z)
    ms = jnp.mean(y * y, axis=-1, keepdims=True)
    return (y * lax.rsqrt(ms + RMS_EPS) * nw).astype(BF16)


def _chunk_iotas():
    li = lax.broadcasted_iota(jnp.int32, (CHUNK, CHUNK), 0)
    si = lax.broadcasted_iota(jnp.int32, (CHUNK, CHUNK), 1)
    return li, si


def _ssd_main_kernel(z_ref, xr_ref, br_ref, cr_ref, pk_acs_ref, pk3_ref, acst_ref, tott_ref, cw_ref, cbias_ref,
                     dx_ref, nw_ref, r_ref, e_ref, y_ref, sfin_ref, buf_ref, st_ref):
    del tott_ref
    g = pl.program_id(0)
    rb = pl.program_id(1)

    @pl.when(rb == 0)
    def _():
        st_ref[...] = jnp.zeros_like(st_ref)
        buf_ref[0:CONV_HEAD, :] = jnp.zeros((CONV_HEAD, XBC_W), F32)

    @pl.when(rb > 0)
    def _():
        buf_ref[0:CONV_HEAD, :] = buf_ref[SSD_RB:SSD_RB + CONV_HEAD, :]

    buf_ref[CONV_HEAD:, 0:GROUP_W] = xr_ref[...]
    buf_ref[CONV_HEAD:, GROUP_W:GROUP_W + STATE] = br_ref[...]
    buf_ref[CONV_HEAD:, GROUP_W + STATE:] = cr_ref[...]

    li, si = _chunk_iotas()
    causal = si <= li
    lane_lo = si < HEADS
    g8 = pl.multiple_of(g * HEADS_PER_GROUP, SUBLANES)
    cw = cw_ref[...]
    cbias = cbias_ref[...]

    for k in range(SSD_NB):
        rows = pl.ds(k * CHUNK, CHUNK)
        xbc = _conv_silu(buf_ref, CONV_HEAD + k * CHUNK, CHUNK, cw, cbias)
        arow = acst_ref[k, pl.ds(g8, HEADS_PER_GROUP), :]
        xs, bmb, cmb, y_intra, eax, xw = _intra_chunk(xbc, pk_acs_ref[rows, :], pk3_ref[rows, :], arow,
                                                      causal, lane_lo, r_ref, e_ref)
        st = st_ref[...]
        y = y_intra + jnp.dot(cmb, st.astype(BF16), preferred_element_type=F32) * eax + dx_ref[...] * xs
        y_ref[rows, :] = _gate_norm(y, z_ref[rows, :], nw_ref[...])
        st_ref[...] = st * eax[CHUNK - 1:CHUNK, :] + lax.dot_general(bmb, xw, _TN, preferred_element_type=F32)

    @pl.when(rb == T_MAIN // SSD_RB - 1)
    def _():
        sfin_ref[...] = st_ref[...]


def _ssd_samp_kernel(z_ref, xr_ref, br_ref, cr_ref, pk_acs_ref, pk3_ref, acst_ref, tott_ref, cw_ref, cbias_ref,
                     dx_ref, nw_ref, r_ref, e_ref, prev_ref, s0_ref, yprev_ref, y_ref, s1_ref,
                     buf_ref, xbc_ref, yint_ref):
    del yprev_ref
    g = pl.program_id(0)
    slot = CONV_HEAD + DEC_SEQ
    cw = cw_ref[...]
    cbias = cbias_ref[...]
    for q in range(SEQ_PER_CHUNK):
        r0 = q * DEC_SEQ
        top = q * slot + CONV_HEAD
        buf_ref[top - (SSD_CONV_WIDTH - 1):top, :] = prev_ref[q]
        buf_ref[top:top + DEC_SEQ, 0:GROUP_W] = xr_ref[r0:r0 + DEC_SEQ, :]
        buf_ref[top:top + DEC_SEQ, GROUP_W:GROUP_W + STATE] = br_ref[r0:r0 + DEC_SEQ, :]
        buf_ref[top:top + DEC_SEQ, GROUP_W + STATE:] = cr_ref[r0:r0 + DEC_SEQ, :]
    for q in range(SEQ_PER_CHUNK):
        r0 = q * DEC_SEQ
        xbc_ref[r0:r0 + DEC_SEQ, :] = _conv_silu(buf_ref, q * slot + CONV_HEAD, DEC_SEQ, cw, cbias)

    li, si = _chunk_iotas()
    same = (li // DEC_SEQ) == (si // DEC_SEQ)
    causal = jnp.logical_and(si <= li, same)
    lane_lo = si < HEADS
    g8 = pl.multiple_of(g * HEADS_PER_GROUP, SUBLANES)

    arow = acst_ref[0, pl.ds(g8, HEADS_PER_GROUP), :]
    trow = tott_ref[0, pl.ds(g8, HEADS_PER_GROUP), :]
    xs, bmb, cmb, y_intra, eax, xw = _intra_chunk(xbc_ref[...], pk_acs_ref[...], pk3_ref[...], arow,
                                                  causal, lane_lo, r_ref, e_ref)
    for q in range(SEQ_PER_CHUNK):
        r0 = q * DEC_SEQ
        yint_ref[r0:r0 + DEC_SEQ, :] = lax.dot_general(cmb[r0:r0 + DEC_SEQ, :], s0_ref[q].astype(BF16), _NT,
                                                       preferred_element_type=F32)
    y = y_intra + yint_ref[...] * eax + dx_ref[...] * xs
    y_ref[...] = _gate_norm(y, z_ref[...], nw_ref[...])
    for q in range(SEQ_PER_CHUNK):
        r0 = q * DEC_SEQ
        upd = lax.dot_general(xw[r0:r0 + DEC_SEQ, :], bmb[r0:r0 + DEC_SEQ, :], _TN, preferred_element_type=F32)
        for r in range(HEADS_PER_GROUP):
            lo, hi = r * HEADDIM, (r + 1) * HEADDIM
            s1_ref[q, lo:hi, :] = s0_ref[q, lo:hi, :] * jnp.exp(trow[r:r + 1, r0:r0 + 1]) + upd[lo:hi, :]


def _ssd_specs(rows, row0):
    xcol = D_INNER // GROUP_W
    bcol = (2 * D_INNER) // STATE
    return [
        pl.BlockSpec((rows, GROUP_W), lambda g, c: (row0 + c, g)),
        pl.BlockSpec((rows, GROUP_W), lambda g, c: (row0 + c, xcol + g)),
        pl.BlockSpec((rows, STATE), lambda g, c: (row0 + c, bcol + g)),
        pl.BlockSpec((rows, STATE), lambda g, c: (row0 + c, bcol + GROUPS + g)),
        pl.BlockSpec((rows, SPLIT_K), lambda g, c: (row0 + c, 0)),
        pl.BlockSpec((rows, 3 * LANES), lambda g, c: (row0 + c, 0)),
        pl.BlockSpec((rows // CHUNK, LANES, CHUNK), lambda g, c: (row0 + c, 0, 0)),
        pl.BlockSpec((rows // CHUNK, LANES, CHUNK), lambda g, c: (row0 + c, 0, 0)),
        pl.BlockSpec((None, SSD_CONV_WIDTH, XBC_W), lambda g, c: (g, 0, 0)),
        pl.BlockSpec((None, 1, XBC_W), lambda g, c: (g, 0, 0)),
        pl.BlockSpec((1, GROUP_W), lambda g, c: (0, g)),
        pl.BlockSpec((1, GROUP_W), lambda g, c: (0, g)),
        pl.BlockSpec((None, SPLIT_K, HEADS_PER_GROUP * CHUNK), lambda g, c: (g, 0, 0)),
        pl.BlockSpec((None, LANES, GROUP_W), lambda g, c: (g, 0, 0)),
    ]


def _spread_matrices():
    k = np.arange(SPLIT_K)
    head = k % HEADS
    used = (k // HEADS) < 3
    g = np.arange(GROUPS)[:, None, None]
    rj = np.arange(HEADS_PER_GROUP * CHUNK)[None, None, :] // CHUNK
    ej = np.arange(GROUP_W)[None, None, :] // HEADDIM
    hk = head[None, :, None]
    uk = used[None, :, None]
    r = (uk & (hk == g * HEADS_PER_GROUP + rj)).astype(np.float32)
    e = (hk == g * HEADS_PER_GROUP + ej)[:, :LANES].astype(np.float32)
    return jnp.asarray(r, BF16), jnp.asarray(e, BF16)


def _per_group(a):
    lead = a.shape[:-1]
    x = a[..., :D_INNER].reshape(*lead, GROUPS, GROUP_W)
    b = a[..., D_INNER:D_INNER + GROUPS * STATE].reshape(*lead, GROUPS, STATE)
    c = a[..., D_INNER + GROUPS * STATE:].reshape(*lead, GROUPS, STATE)
    return jnp.moveaxis(jnp.concatenate([x, b, c], axis=-1), -2, 0)


def _ssd(zxd, conv_w, conv_b, dt_bias, a_log, d_skip, norm_w, conv_prev, state0):
    r_mat, e_mat = _spread_matrices()
    pad = ((0, 0), (0, LANES - HEADS))
    prep = _ssd_prep(zxd, jnp.pad(dt_bias.reshape(1, HEADS).astype(F32), pad),
                     jnp.pad(a_log.reshape(1, HEADS).astype(F32), pad))
    params = (_per_group(conv_w), _per_group(conv_b.reshape(1, CONV_DIM)),
              jnp.repeat(d_skip.astype(F32), HEADDIM).reshape(1, D_INNER),
              norm_w.reshape(1, D_INNER).astype(F32), r_mat, e_mat)
    common = (zxd,) * 4 + tuple(prep) + params
    y, s_fin = pl.pallas_call(
        _ssd_main_kernel,
        grid=(GROUPS, T_MAIN // SSD_RB),
        in_specs=_ssd_specs(SSD_RB, 0),
        out_specs=[
            pl.BlockSpec((SSD_RB, GROUP_W), lambda g, c: (c, g)),
            pl.BlockSpec((None, STATE, GROUP_W), lambda g, c: (g, 0, 0)),
        ],
        out_shape=[
            jax.ShapeDtypeStruct((T_ALL, D_INNER), BF16),
            jax.ShapeDtypeStruct((GROUPS, STATE, GROUP_W), F32),
        ],
        scratch_shapes=[
            pltpu.VMEM((CONV_HEAD + SSD_RB, XBC_W), F32),
            pltpu.VMEM((STATE, GROUP_W), F32),
        ],
        compiler_params=_cparams(("parallel", "arbitrary")),
        name="ssd_prompt",
    )(*common)

    n_in = len(common)
    y, s_new = pl.pallas_call(
        _ssd_samp_kernel,
        grid=(GROUPS, N_SAMP_CHUNKS),
        in_specs=_ssd_specs(CHUNK, N_MAIN_CHUNKS) + [
            pl.BlockSpec((None, SEQ_PER_CHUNK, SSD_CONV_WIDTH - 1, XBC_W), lambda g, c: (g, c, 0, 0)),
            pl.BlockSpec((SEQ_PER_CHUNK, None, GROUP_W, STATE), lambda g, c: (c, g, 0, 0)),
            pl.BlockSpec(memory_space=pl.ANY),
        ],
        out_specs=[
            pl.BlockSpec((CHUNK, GROUP_W), lambda g, c: (N_MAIN_CHUNKS + c, g)),
            pl.BlockSpec((SEQ_PER_CHUNK, None, GROUP_W, STATE), lambda g, c: (c, g, 0, 0)),
        ],
        out_shape=[
            jax.ShapeDtypeStruct((T_ALL, D_INNER), BF16),
            jax.ShapeDtypeStruct((DEC_BATCH, GROUPS, GROUP_W, STATE), F32),
        ],
        scratch_shapes=[
            pltpu.VMEM((SEQ_PER_CHUNK * (CONV_HEAD + DEC_SEQ), XBC_W), F32),
            pltpu.VMEM((CHUNK, XBC_W), F32),
            pltpu.VMEM((CHUNK, GROUP_W), F32),
        ],
        input_output_aliases={n_in + 2: 0},
        compiler_params=_cparams(("parallel", "arbitrary")),
        name="ssd_sample",
    )(*common, _per_group(conv_prev), state0, y)
    return y, s_fin, s_new


def _tail_rows(a, n):
    c = a.shape[-1]
    p = a[T_PROMPT - n:T_PROMPT].reshape(1, 1, n, c)
    s = a[T_MAIN:].reshape(DEC_BATCH, DEC_SEQ, c)[:, DEC_SEQ - n:].reshape(1, DEC_BATCH, n, c)
    return p, s


def kernel(x_prompt, x_sample, cache_sc_conv, state_ssd_conv, state_ssd, meta_tokens, ln_g, ln_b,
           ffn_w1, ffn_w3, ffn_w2, sc_w_in, sc_w_conv, sc_w_out,
           ssd_w_in, ssd_w_conv, ssd_b_conv, ssd_dt_bias, ssd_a_log, ssd_d, ssd_norm_w, ssd_w_out):
    w1, w3, w2 = ffn_w1, ffn_w3, ffn_w2
    sc_out_w = sc_w_out.astype(BF16)
    ssd_out_w = ssd_w_out.astype(BF16)
    lng = ln_g.reshape(DEPTH, 3, 1, D_MODEL)
    lnb = ln_b.reshape(DEPTH, 3, 1, D_MODEL)

    h = jnp.concatenate([meta_tokens.astype(F32), x_prompt[0], jnp.zeros((N_PAD, D_MODEL), F32),
                         x_sample.reshape(N_SAMP, D_MODEL)], axis=0)

    h, hb = _ffn(h, w1, w3, w2, lng, lnb, 0, 0, 0, emit_bf16=True)
    v, u_tail = _sc_mix(hb, sc_w_in, 0, sc_w_conv[0], cache_sc_conv[0])
    keep = SC_WIDTH - 1
    new_sc_p = u_tail[T_PROMPT - keep - SC_TAIL0:T_PROMPT - SC_TAIL0].reshape(1, 1, keep, D_MODEL)
    new_sc_s = u_tail[SC_SAMP_OFF:].reshape(DEC_BATCH, DEC_SEQ, D_MODEL)[:, DEC_SEQ - keep:].reshape(
        1, DEC_BATCH, keep, D_MODEL)
    h = _proj_ln(v, sc_out_w, 0, h, lng, lnb, 0, 1)
    h = _ffn(h, w1, w3, w2, lng, lnb, 0, 1, 2)

    h, hb = _ffn(h, w1, w3, w2, lng, lnb, 1, 0, 0, emit_bf16=True)
    zxd = _mm(hb, jnp.swapaxes(ssd_w_in[0], 0, 1), ZXD_W, ZXD_TN)
    new_conv_p, new_conv_s = _tail_rows(zxd[:, D_INNER:DT_COL], SSD_CONV_WIDTH - 1)
    state0 = state_ssd[0].reshape(DEC_BATCH, GROUPS, GROUP_W, STATE)
    y, s_fin, s_new = _ssd(zxd, ssd_w_conv[0], ssd_b_conv[0], ssd_dt_bias[0], ssd_a_log[0], ssd_d[0],
                           ssd_norm_w[0], state_ssd_conv[0], state0)
    h = _proj_ln(y, ssd_out_w, 0, h, lng, lnb, 1, 1)
    h = _ffn(h, w1, w3, w2, lng, lnb, 1, 1, 2)

    y_prompt = h[N_META:T_PROMPT].reshape(1, SEQ, D_MODEL)
    y_sample = h[T_MAIN:].reshape(DEC_BATCH, DEC_SEQ, D_MODEL)
    new_ssd_p = jnp.swapaxes(s_fin, 1, 2).reshape(1, 1, HEADS, HEADDIM, STATE).astype(state_ssd.dtype)
    new_ssd_s = s_new.reshape(1, DEC_BATCH, HEADS, HEADDIM, STATE).astype(state_ssd.dtype)
    return (y_prompt, y_sample, new_sc_p, new_sc_s, new_conv_p, new_conv_s, new_ssd_p, new_ssd_s)
```

```python
import functools

import numpy as np

import jax
import jax.numpy as jnp
from jax import lax
from jax.experimental import pallas as pl
from jax.experimental.pallas import tpu as pltpu

F32 = jnp.float32
BF16 = jnp.bfloat16

D_MODEL = 2048
SEQ = 8192
DEPTH = 2
DEC_BATCH = 32
DEC_SEQ = 16
N_META = 16
D_FF = 5632
SC_WIDTH = 3
D_INNER = 4096
HEADDIM = 64
HEADS = 64
GROUPS = 8
HEADS_PER_GROUP = 8
STATE = 128
SSD_CONV_WIDTH = 4
CONV_DIM = D_INNER + 2 * GROUPS * STATE
GROUP_W = HEADS_PER_GROUP * HEADDIM
XBC_W = GROUP_W + 2 * STATE
ALPHA = (2.0 * DEPTH) ** 0.25
LN_EPS = 1e-5
RMS_EPS = 1e-5

CHUNK = 128
T_PROMPT = N_META + SEQ
N_MAIN_CHUNKS = -(-T_PROMPT // CHUNK)
T_MAIN = N_MAIN_CHUNKS * CHUNK
N_PAD = T_MAIN - T_PROMPT
N_SAMP = DEC_BATCH * DEC_SEQ
N_SAMP_CHUNKS = N_SAMP // CHUNK
SEQ_PER_CHUNK = CHUNK // DEC_SEQ
T_ALL = T_MAIN + N_SAMP

TM = 736
TM_IN = 2 * TM
TF = 512
TF_HEAD = 256
FFN_ROW_SPLIT = 2
PROJ_ROW_SPLIT = 2
SC_ROW_SPLIT = 2
DT_COL = D_INNER + CONV_DIM
ZXD_W = DT_COL
ZXD_TN = 1280
SSD_NB = 13
SSD_RB = SSD_NB * CHUNK
PREP_NB = 3
LANES = 128
SUBLANES = 8
CONV_HEAD = SUBLANES

VMEM_LIMIT_BYTES = 56 * 1024 * 1024
VMEM_LIMIT_IN_PROJ_BYTES = 61 * 1024 * 1024


def _cparams(sem, vmem_limit_bytes=VMEM_LIMIT_BYTES):
    return pltpu.CompilerParams(dimension_semantics=sem, vmem_limit_bytes=vmem_limit_bytes)


def _layer_norm(v, g, b):
    mu = jnp.mean(v, axis=-1, keepdims=True)
    c = v - mu
    var = jnp.mean(c * c, axis=-1, keepdims=True)
    return c * lax.rsqrt(var + LN_EPS) * g + b


def _silu(x):
    return x * jax.nn.sigmoid(x)


def _softplus(x):
    return jnp.maximum(x, 0.0) + jnp.log(1.0 + jnp.exp(-jnp.abs(x)))


def _ffn_kernel(*refs, nf, n_alias, emit_bf16, convert):
    x_ref, w1_ref, w3_ref, w2_ref, g_ref, b_ref = refs[:6]
    outs = refs[6 + n_alias:]
    o_ref = outs[0]
    pos = 1
    ob_ref = None
    if emit_bf16:
        ob_ref = outs[pos]
        pos += 1
    if convert:
        wb_refs = outs[pos:pos + 3]
        for src, dst in zip((w1_ref, w3_ref, w2_ref), wb_refs):
            dst[...] = src[...].astype(BF16)
        w1_ref, w3_ref, w2_ref = wb_refs
        pos += 3
    xb_ref = outs[pos]
    f = pl.program_id(1)

    @pl.when(f == 0)
    def _():
        xb_ref[...] = x_ref[...].astype(BF16)
        o_ref[...] = jnp.zeros_like(o_ref)

    rs = TM // FFN_ROW_SPLIT

    def slab(s, last):
        rows = pl.ds(s * rs, rs)
        xb = xb_ref[rows, :]
        h1 = jnp.dot(xb, w1_ref[...], preferred_element_type=F32)
        h3 = jnp.dot(xb, w3_ref[...], preferred_element_type=F32)
        gate = (_silu(h1) * h3).astype(BF16)
        acc = o_ref[rows, :] + jnp.dot(gate, w2_ref[...], preferred_element_type=F32)
        if last:
            res = _layer_norm(ALPHA * x_ref[rows, :] + 0.5 * acc, g_ref[...], b_ref[...])
            o_ref[rows, :] = res
            if ob_ref is not None:
                ob_ref[rows, :] = res.astype(BF16)
        else:
            o_ref[rows, :] = acc

    @pl.when(f < nf - 1)
    def _():
        for s in range(FFN_ROW_SPLIT):
            slab(s, False)

    @pl.when(f == nf - 1)
    def _():
        for s in range(FFN_ROW_SPLIT):
            slab(s, True)


def _ffn(h, w1, w3, w2, lng, lnb, i, j, k, emit_bf16=False):
    ln_spec = pl.BlockSpec((None, None, 1, D_MODEL), lambda m, f: (i, k, 0, 0))
    h_shapes = [jax.ShapeDtypeStruct((T_ALL, D_MODEL), F32)]
    if emit_bf16:
        h_shapes.append(jax.ShapeDtypeStruct((T_ALL, D_MODEL), BF16))
    n_h = len(h_shapes)
    scratch = [pltpu.VMEM((TM, D_MODEL), BF16)]

    head_row = pl.BlockSpec((TM, D_MODEL), lambda m, f: (0, 0))
    nf = D_FF // TF
    nf_head = D_FF // TF_HEAD
    per = TF // TF_HEAD
    head = pl.pallas_call(
        functools.partial(_ffn_kernel, nf=nf_head, n_alias=0, emit_bf16=emit_bf16, convert=True),
        grid=(1, nf_head),
        in_specs=[
            head_row,
            pl.BlockSpec((None, None, D_MODEL, TF_HEAD), lambda m, f: (i, j, 0, f)),
            pl.BlockSpec((None, None, D_MODEL, TF_HEAD), lambda m, f: (i, j, 0, f)),
            pl.BlockSpec((None, None, TF_HEAD, D_MODEL), lambda m, f: (i, j, f, 0)),
            ln_spec, ln_spec,
        ],
        out_specs=[head_row] * n_h + [
            pl.BlockSpec((None, D_MODEL, TF_HEAD), lambda m, f: (f // per, 0, f % per)),
            pl.BlockSpec((None, D_MODEL, TF_HEAD), lambda m, f: (f // per, 0, f % per)),
            pl.BlockSpec((TF_HEAD, D_MODEL), lambda m, f: (f, 0)),
        ],
        out_shape=h_shapes + [
            jax.ShapeDtypeStruct((nf, D_MODEL, TF), BF16),
            jax.ShapeDtypeStruct((nf, D_MODEL, TF), BF16),
            jax.ShapeDtypeStruct((D_FF, D_MODEL), BF16),
        ],
        scratch_shapes=scratch,
        compiler_params=_cparams(("parallel", "arbitrary")),
        name="ffn_ln_head",
    )(h, w1, w3, w2, lng, lnb)
    h_parts, (w1b, w3b, w2b) = head[:n_h], head[n_h:]

    row = pl.BlockSpec((TM, D_MODEL), lambda m, f: (m + 1, 0))
    out = pl.pallas_call(
        functools.partial(_ffn_kernel, nf=nf, n_alias=n_h, emit_bf16=emit_bf16, convert=False),
        grid=(T_ALL // TM - 1, nf),
        in_specs=[
            row,
            pl.BlockSpec((None, D_MODEL, TF), lambda m, f: (f, 0, 0)),
            pl.BlockSpec((None, D_MODEL, TF), lambda m, f: (f, 0, 0)),
            pl.BlockSpec((TF, D_MODEL), lambda m, f: (f, 0)),
            ln_spec, ln_spec,
        ] + [pl.BlockSpec(memory_space=pl.ANY)] * n_h,
        out_specs=[row] * n_h,
        out_shape=h_shapes,
        scratch_shapes=scratch,
        input_output_aliases={6 + a: a for a in range(n_h)},
        compiler_params=_cparams(("parallel", "arbitrary")),
        name="ffn_ln",
    )(h, w1b, w3b, w2b, lng, lnb, *h_parts)
    return tuple(out) if emit_bf16 else out[0]


def _mm_kernel(x_ref, wt_ref, o_ref, wb_ref):
    @pl.when(pl.program_id(1) == 0)
    def _():
        wb_ref[...] = wt_ref[...].astype(BF16)

    o_ref[...] = lax.dot_general(x_ref[...], wb_ref[...], (((1,), (1,)), ((), ())),
                                 preferred_element_type=F32)


def _mm(xb, wt, n_out, tn):
    n, k = wt.shape
    assert n_out % tn == 0 and n_out <= n
    return pl.pallas_call(
        _mm_kernel,
        grid=(n_out // tn, T_ALL // TM_IN),
        in_specs=[
            pl.BlockSpec((TM_IN, k), lambda j, m: (m, 0)),
            pl.BlockSpec((tn, k), lambda j, m: (j, 0)),
        ],
        out_specs=pl.BlockSpec((TM_IN, tn), lambda j, m: (m, j)),
        out_shape=jax.ShapeDtypeStruct((T_ALL, n_out), F32),
        scratch_shapes=[pltpu.VMEM((tn, k), BF16)],
        compiler_params=_cparams(("parallel", "arbitrary"), VMEM_LIMIT_IN_PROJ_BYTES),
        name="in_proj",
    )(xb, wt)


SC_TAIL0 = T_ALL - TM
SC_SAMP_OFF = T_MAIN - SC_TAIL0
assert SC_SAMP_OFF % DEC_SEQ == 0 and T_PROMPT - (SC_WIDTH - 1) >= SC_TAIL0


def _sc_mix_kernel(x_ref, wb_ref, wc_ref, wh_ref, cw_ref, o1_ref, o2_ref, v_ref, ut_ref,
                   wbb_ref, wcb_ref, whb_ref, ubuf_ref, *, nm):
    m = pl.program_id(1)
    tn = ubuf_ref.shape[1]

    @pl.when(m == 0)
    def _():
        wbb_ref[...] = wb_ref[...].astype(BF16)
        wcb_ref[...] = wc_ref[...].astype(BF16)
        whb_ref[...] = wh_ref[...].astype(BF16)
        ubuf_ref[0:CONV_HEAD, :] = jnp.zeros((CONV_HEAD, tn), F32)

    cw = cw_ref[...]
    ov_refs = (o1_ref, o2_ref)
    rs = TM // SC_ROW_SPLIT

    def slab(s, last_tile):
        r0 = s * rs
        xb = x_ref[r0:r0 + rs, :]
        bg = jnp.dot(xb, wbb_ref[...], preferred_element_type=F32)
        c = jnp.dot(xb, wcb_ref[...], preferred_element_type=F32)
        hh = jnp.dot(xb, whb_ref[...], preferred_element_type=F32)
        u = c * hh
        ubuf_ref[CONV_HEAD + r0:CONV_HEAD + r0 + rs, :] = u
        acc = cw[SC_WIDTH - 1:SC_WIDTH, :] * u
        if last_tile:
            lrow = lax.broadcasted_iota(jnp.int32, (rs, 1), 0) + r0
            spos = jnp.bitwise_and(lrow, DEC_SEQ - 1)
        for d in range(1, SC_WIDTH):
            ud = ubuf_ref[CONV_HEAD + r0 - d:CONV_HEAD + r0 - d + rs, :]
            if last_tile:
                first = jnp.logical_and(lrow >= SC_SAMP_OFF, spos < d)
                ud = jnp.where(first, ov_refs[d - 1][r0:r0 + rs, :], ud)
            acc = acc + cw[SC_WIDTH - 1 - d:SC_WIDTH - d, :] * ud
        v_ref[r0:r0 + rs, :] = (bg * acc).astype(BF16)
        if last_tile:
            ut_ref[r0:r0 + rs, :] = u

    @pl.when(m < nm - 1)
    def _():
        for s in range(SC_ROW_SPLIT):
            slab(s, False)

    @pl.when(m == nm - 1)
    def _():
        for s in range(SC_ROW_SPLIT):
            slab(s, True)

    ubuf_ref[0:CONV_HEAD, :] = ubuf_ref[TM:TM + CONV_HEAD, :]


def _sc_mix(xb, w_in, j, w_conv, conv_prev):
    tn = 512
    nb = D_MODEL // tn
    nm = T_ALL // TM
    ovs = [jnp.pad(o, ((SC_SAMP_OFF, 0), (0, 0))) for o in _conv_overrides(conv_prev, SC_WIDTH)]
    tail = pl.BlockSpec((TM, tn), lambda n, m: (0, n))
    wscratch = pltpu.VMEM((D_MODEL, tn), BF16)
    return pl.pallas_call(
        functools.partial(_sc_mix_kernel, nm=nm),
        grid=(nb, nm),
        in_specs=[
            pl.BlockSpec((TM, D_MODEL), lambda n, m: (m, 0)),
            pl.BlockSpec((None, D_MODEL, tn), lambda n, m: (j, 0, n)),
            pl.BlockSpec((None, D_MODEL, tn), lambda n, m: (j, 0, n + nb)),
            pl.BlockSpec((None, D_MODEL, tn), lambda n, m: (j, 0, n + 2 * nb)),
            pl.BlockSpec((SC_WIDTH, tn), lambda n, m: (0, n)),
            tail, tail,
        ],
        out_specs=[pl.BlockSpec((TM, tn), lambda n, m: (m, n)), tail],
        out_shape=[jax.ShapeDtypeStruct((T_ALL, D_MODEL), BF16),
                   jax.ShapeDtypeStruct((TM, D_MODEL), F32)],
        scratch_shapes=[wscratch, wscratch, wscratch, pltpu.VMEM((CONV_HEAD + TM, tn), F32)],
        compiler_params=_cparams(("parallel", "arbitrary")),
        name="sc_mix",
    )(xb, w_in, w_in, w_in, w_conv.astype(F32), *ovs)


def _conv_overrides(prev, taps):
    c = prev.shape[-1]
    out = []
    for d in range(1, taps):
        o = jnp.zeros((DEC_BATCH, DEC_SEQ, c), F32)
        for p in range(d):
            o = o.at[:, p].set(prev[:, taps - 1 + p - d])
        out.append(o.reshape(N_SAMP, c))
    return out


def _proj_ln_kernel(y_ref, w_ref, h_ref, g_ref, b_ref, o_ref, *, nk):
    rs = TM // PROJ_ROW_SPLIT

    def slab_dot(rows):
        return jnp.dot(y_ref[rows, :], w_ref[...], preferred_element_type=F32)

    def finish(rows, acc):
        v = ALPHA * h_ref[rows, :] + acc
        o_ref[rows, :] = _layer_norm(v, g_ref[...], b_ref[...])

    slabs = [pl.ds(s * rs, rs) for s in range(PROJ_ROW_SPLIT)]
    if nk == 1:
        for rows in slabs:
            finish(rows, slab_dot(rows))
        return

    k = pl.program_id(1)

    @pl.when(k == 0)
    def _():
        for rows in slabs:
            o_ref[rows, :] = slab_dot(rows)

    @pl.when(k == 1)
    def _():
        for rows in slabs:
            finish(rows, o_ref[rows, :] + slab_dot(rows))


def _proj_ln(y, w, j, h, lng, lnb, i, k):
    kdim = w.shape[1]
    tk = 2048
    nk = kdim // tk
    assert nk in (1, 2)
    return pl.pallas_call(
        functools.partial(_proj_ln_kernel, nk=nk),
        grid=(T_ALL // TM, nk),
        in_specs=[
            pl.BlockSpec((TM, tk), lambda m, kk: (m, kk)),
            pl.BlockSpec((None, tk, D_MODEL), lambda m, kk: (j, kk, 0)),
            pl.BlockSpec((TM, D_MODEL), lambda m, kk: (m, 0)),
            pl.BlockSpec((None, None, 1, D_MODEL), lambda m, kk: (i, k, 0, 0)),
            pl.BlockSpec((None, None, 1, D_MODEL), lambda m, kk: (i, k, 0, 0)),
        ],
        out_specs=pl.BlockSpec((TM, D_MODEL), lambda m, kk: (m, 0)),
        out_shape=jax.ShapeDtypeStruct((T_ALL, D_MODEL), F32),
        compiler_params=_cparams(("parallel", "arbitrary")),
        name="out_proj_ln",
    )(y, w, h, lng, lnb)


_NT = (((1,), (1,)), ((), ()))
_TN = (((0,), (0,)), ((), ()))
SPLIT_K = 2 * LANES


def _bf16_pieces(v):
    hi = v.astype(BF16)
    r1 = v - hi.astype(F32)
    mid = r1.astype(BF16)
    lo = (r1 - mid.astype(F32)).astype(BF16)
    return hi, mid, lo


def _split_pack(v, lane_lo):
    hi, mid, lo = _bf16_pieces(v)
    a = jnp.where(lane_lo, hi.astype(F32), pltpu.roll(mid.astype(F32), HEADS, axis=1))
    b = jnp.where(lane_lo, lo.astype(F32), 0.0)
    return jnp.concatenate([a, b], axis=1).astype(BF16)


def _split_pack2(v, lane_lo):
    hi = v.astype(BF16)
    mid = (v - hi.astype(F32)).astype(BF16)
    return jnp.where(lane_lo, hi.astype(F32), pltpu.roll(mid.astype(F32), HEADS, axis=1)).astype(BF16)


def _masked_cumsum(mask_bf, da):
    hi, mid, lo = _bf16_pieces(da)
    p = jnp.dot(mask_bf, jnp.concatenate([hi, mid, lo], axis=1), preferred_element_type=F32)
    return p[:, :LANES] + p[:, LANES:2 * LANES] + p[:, 2 * LANES:]


def _conv_silu(buf_ref, base, n, cw, cbias):
    taps = SSD_CONV_WIDTH
    acc = cw[taps - 1:taps, :] * buf_ref[base:base + n, :]
    for d in range(1, taps):
        acc = acc + cw[taps - 1 - d:taps - d, :] * buf_ref[base - d:base - d + n, :]
    return _silu(acc + cbias)


def _ssd_prep_kernel(x_ref, wdt_ref, dtb_ref, al_ref, pk_acs_ref, pk3_ref, acst_ref, tott_ref):
    li, si = _chunk_iotas()
    lane_lo = si < HEADS
    same_seq = (li // DEC_SEQ) == (si // DEC_SEQ)
    a_row = -jnp.exp(al_ref[...])
    wdt = wdt_ref[...].astype(BF16)
    wdt = jnp.concatenate([wdt, jnp.zeros_like(wdt)], axis=0)
    dt_raw = lax.dot_general(x_ref[...], wdt, _NT, preferred_element_type=F32)
    for k in range(PREP_NB):
        c = pl.program_id(0) * PREP_NB + k
        rows = pl.ds(k * CHUNK, CHUNK)
        same = jnp.logical_or(c < N_MAIN_CHUNKS, same_seq)
        causal = jnp.logical_and(si <= li, same)
        row = lax.broadcasted_iota(jnp.int32, (CHUNK, 1), 0) + c * CHUNK
        valid = jnp.logical_or(row < T_PROMPT, row >= T_MAIN)
        dt = jnp.where(jnp.logical_and(valid, lane_lo),
                       _softplus(dt_raw[k * CHUNK:(k + 1) * CHUNK, :] + dtb_ref[...]), 0.0)
        da = dt * a_row
        acs = _masked_cumsum(causal.astype(F32).astype(BF16), da)
        tot = _masked_cumsum(same.astype(F32).astype(BF16), da)
        pk_acs_ref[rows, :] = _split_pack(acs, lane_lo)
        pk3_ref[rows, :] = jnp.concatenate([_split_pack2(dt, lane_lo), _split_pack2(jnp.exp(tot - acs), lane_lo),
                                            _split_pack2(jnp.exp(acs), lane_lo)], axis=1)
        acst_ref[k] = acs.T
        tott_ref[k] = tot.T


def _ssd_prep(xb, wt, dt_bias, a_log):
    n_chunks = T_ALL // CHUNK
    rows = PREP_NB * CHUNK
    tr = pl.BlockSpec((PREP_NB, LANES, CHUNK), lambda c: (c, 0, 0))
    return pl.pallas_call(
        _ssd_prep_kernel,
        grid=(n_chunks // PREP_NB,),
        in_specs=[
            pl.BlockSpec((rows, D_MODEL), lambda c: (c, 0)),
            pl.BlockSpec((HEADS, D_MODEL), lambda c: (DT_COL // HEADS, 0)),
            pl.BlockSpec((1, LANES), lambda c: (0, 0)),
            pl.BlockSpec((1, LANES), lambda c: (0, 0)),
        ],
        out_specs=[
            pl.BlockSpec((rows, SPLIT_K), lambda c: (c, 0)),
            pl.BlockSpec((rows, 3 * LANES), lambda c: (c, 0)),
            tr, tr,
        ],
        out_shape=[
            jax.ShapeDtypeStruct((T_ALL, SPLIT_K), BF16),
            jax.ShapeDtypeStruct((T_ALL, 3 * LANES), BF16),
            jax.ShapeDtypeStruct((n_chunks, LANES, CHUNK), F32),
            jax.ShapeDtypeStruct((n_chunks, LANES, CHUNK), F32),
        ],
        compiler_params=_cparams(("parallel",)),
        name="ssd_prep",
    )(xb, wt, dt_bias, a_log)


def _intra_chunk(xbc, pk_acs, pk3, arow, causal, lane_lo, r_ref, e_ref):
    xs = xbc[:, :GROUP_W]
    bmb = xbc[:, GROUP_W:GROUP_W + STATE].astype(BF16)
    cmb = xbc[:, GROUP_W + STATE:].astype(BF16)
    colmat = jnp.dot(pk_acs, r_ref[...], preferred_element_type=F32)
    packed = jnp.concatenate([pk3[:, :LANES], pk3[:, LANES:2 * LANES], pk3[:, 2 * LANES:]], axis=0)
    ex = jnp.dot(packed, e_ref[...], preferred_element_type=F32)
    dtx, tex, eax = ex[:CHUNK], ex[CHUNK:2 * CHUNK], ex[2 * CHUNK:]
    xdt = xs * dtx
    cb = lax.dot_general(cmb, bmb, _NT, preferred_element_type=F32)
    ys = []
    for q in range(HEADS_PER_GROUP // 2):
        scs = []
        for r in (2 * q, 2 * q + 1):
            seg = colmat[:, r * CHUNK:(r + 1) * CHUNK] - arow[r:r + 1, :]
            scs.append((cb * jnp.exp(jnp.where(causal, seg, -jnp.inf))).astype(BF16))
        xp = xdt[:, q * LANES:(q + 1) * LANES]
        rhs = jnp.concatenate([jnp.where(lane_lo, xp, 0.0).astype(BF16),
                               jnp.where(lane_lo, 0.0, xp).astype(BF16)], axis=0)
        ys.append(jnp.dot(jnp.concatenate(scs, axis=1), rhs, preferred_element_type=F32))
    y_intra = jnp.concatenate(ys, axis=1)
    xw = (xdt * tex).astype(BF16)
    return xs, bmb, cmb, y_intra, eax, xw


def _gate_norm(y, z, nw):
    y = y * _silu(z)
    ms = jnp.mean(y * y, axis=-1, keepdims=True)
    return (y * lax.rsqrt(ms + RMS_EPS) * nw).astype(BF16)


def _chunk_iotas():
    li = lax.broadcasted_iota(jnp.int32, (CHUNK, CHUNK), 0)
    si = lax.broadcasted_iota(jnp.int32, (CHUNK, CHUNK), 1)
    return li, si


def _ssd_main_kernel(z_ref, xr_ref, br_ref, cr_ref, pk_acs_ref, pk3_ref, acst_ref, tott_ref, cw_ref, cbias_ref,
                     dx_ref, nw_ref, r_ref, e_ref, y_ref, sfin_ref, buf_ref, st_ref):
    del tott_ref
    g = pl.program_id(0)
    rb = pl.program_id(1)

    @pl.when(rb == 0)
    def _():
        st_ref[...] = jnp.zeros_like(st_ref)
        buf_ref[0:CONV_HEAD, :] = jnp.zeros((CONV_HEAD, XBC_W), F32)

    @pl.when(rb > 0)
    def _():
        buf_ref[0:CONV_HEAD, :] = buf_ref[SSD_RB:SSD_RB + CONV_HEAD, :]

    buf_ref[CONV_HEAD:, 0:GROUP_W] = xr_ref[...]
    buf_ref[CONV_HEAD:, GROUP_W:GROUP_W + STATE] = br_ref[...]
    buf_ref[CONV_HEAD:, GROUP_W + STATE:] = cr_ref[...]

    li, si = _chunk_iotas()
    causal = si <= li
    lane_lo = si < HEADS
    g8 = pl.multiple_of(g * HEADS_PER_GROUP, SUBLANES)
    cw = cw_ref[...]
    cbias = cbias_ref[...]

    for k in range(SSD_NB):
        rows = pl.ds(k * CHUNK, CHUNK)
        xbc = _conv_silu(buf_ref, CONV_HEAD + k * CHUNK, CHUNK, cw, cbias)
        arow = acst_ref[k, pl.ds(g8, HEADS_PER_GROUP), :]
        xs, bmb, cmb, y_intra, eax, xw = _intra_chunk(xbc, pk_acs_ref[rows, :], pk3_ref[rows, :], arow,
                                                      causal, lane_lo, r_ref, e_ref)
        st = st_ref[...]
        y = y_intra + jnp.dot(cmb, st.astype(BF16), preferred_element_type=F32) * eax + dx_ref[...] * xs
        y_ref[rows, :] = _gate_norm(y, z_ref[rows, :], nw_ref[...])
        st_ref[...] = st * eax[CHUNK - 1:CHUNK, :] + lax.dot_general(bmb, xw, _TN, preferred_element_type=F32)

    @pl.when(rb == T_MAIN // SSD_RB - 1)
    def _():
        sfin_ref[...] = st_ref[...]


def _ssd_samp_kernel(z_ref, xr_ref, br_ref, cr_ref, pk_acs_ref, pk3_ref, acst_ref, tott_ref, cw_ref, cbias_ref,
                     dx_ref, nw_ref, r_ref, e_ref, prev_ref, s0_ref, yprev_ref, y_ref, s1_ref,
                     buf_ref, xbc_ref, yint_ref):
    del yprev_ref
    g = pl.program_id(0)
    slot = CONV_HEAD + DEC_SEQ
    cw = cw_ref[...]
    cbias = cbias_ref[...]
    for q in range(SEQ_PER_CHUNK):
        r0 = q * DEC_SEQ
        top = q * slot + CONV_HEAD
        buf_ref[top - (SSD_CONV_WIDTH - 1):top, :] = prev_ref[q]
        buf_ref[top:top + DEC_SEQ, 0:GROUP_W] = xr_ref[r0:r0 + DEC_SEQ, :]
        buf_ref[top:top + DEC_SEQ, GROUP_W:GROUP_W + STATE] = br_ref[r0:r0 + DEC_SEQ, :]
        buf_ref[top:top + DEC_SEQ, GROUP_W + STATE:] = cr_ref[r0:r0 + DEC_SEQ, :]
    for q in range(SEQ_PER_CHUNK):
        r0 = q * DEC_SEQ
        xbc_ref[r0:r0 + DEC_SEQ, :] = _conv_silu(buf_ref, q * slot + CONV_HEAD, DEC_SEQ, cw, cbias)

    li, si = _chunk_iotas()
    same = (li // DEC_SEQ) == (si // DEC_SEQ)
    causal = jnp.logical_and(si <= li, same)
    lane_lo = si < HEADS
    g8 = pl.multiple_of(g * HEADS_PER_GROUP, SUBLANES)

    arow = acst_ref[0, pl.ds(g8, HEADS_PER_GROUP), :]
    trow = tott_ref[0, pl.ds(g8, HEADS_PER_GROUP), :]
    xs, bmb, cmb, y_intra, eax, xw = _intra_chunk(xbc_ref[...], pk_acs_ref[...], pk3_ref[...], arow,
                                                  causal, lane_lo, r_ref, e_ref)
    for q in range(SEQ_PER_CHUNK):
        r0 = q * DEC_SEQ
        yint_ref[r0:r0 + DEC_SEQ, :] = lax.dot_general(cmb[r0:r0 + DEC_SEQ, :], s0_ref[q].astype(BF16), _NT,
                                                       preferred_element_type=F32)
    y = y_intra + yint_ref[...] * eax + dx_ref[...] * xs
    y_ref[...] = _gate_norm(y, z_ref[...], nw_ref[...])
    for q in range(SEQ_PER_CHUNK):
        r0 = q * DEC_SEQ
        upd = lax.dot_general(xw[r0:r0 + DEC_SEQ, :], bmb[r0:r0 + DEC_SEQ, :], _TN, preferred_element_type=F32)
        for r in range(HEADS_PER_GROUP):
            lo, hi = r * HEADDIM, (r + 1) * HEADDIM
            s1_ref[q, lo:hi, :] = s0_ref[q, lo:hi, :] * jnp.exp(trow[r:r + 1, r0:r0 + 1]) + upd[lo:hi, :]


def _ssd_specs(rows, row0):
    xcol = D_INNER // GROUP_W
    bcol = (2 * D_INNER) // STATE
    return [
        pl.BlockSpec((rows, GROUP_W), lambda g, c: (row0 + c, g)),
        pl.BlockSpec((rows, GROUP_W), lambda g, c: (row0 + c, xcol + g)),
        pl.BlockSpec((rows, STATE), lambda g, c: (row0 + c, bcol + g)),
        pl.BlockSpec((rows, STATE), lambda g, c: (row0 + c, bcol + GROUPS + g)),
        pl.BlockSpec((rows, SPLIT_K), lambda g, c: (row0 + c, 0)),
        pl.BlockSpec((rows, 3 * LANES), lambda g, c: (row0 + c, 0)),
        pl.BlockSpec((rows // CHUNK, LANES, CHUNK), lambda g, c: (row0 + c, 0, 0)),
        pl.BlockSpec((rows // CHUNK, LANES, CHUNK), lambda g, c: (row0 + c, 0, 0)),
        pl.BlockSpec((None, SSD_CONV_WIDTH, XBC_W), lambda g, c: (g, 0, 0)),
        pl.BlockSpec((None, 1, XBC_W), lambda g, c: (g, 0, 0)),
        pl.BlockSpec((1, GROUP_W), lambda g, c: (0, g)),
        pl.BlockSpec((1, GROUP_W), lambda g, c: (0, g)),
        pl.BlockSpec((None, SPLIT_K, HEADS_PER_GROUP * CHUNK), lambda g, c: (g, 0, 0)),
        pl.BlockSpec((None, LANES, GROUP_W), lambda g, c: (g, 0, 0)),
    ]


def _spread_matrices():
    k = np.arange(SPLIT_K)
    head = k % HEADS
    used = (k // HEADS) < 3
    g = np.arange(GROUPS)[:, None, None]
    rj = np.arange(HEADS_PER_GROUP * CHUNK)[None, None, :] // CHUNK
    ej = np.arange(GROUP_W)[None, None, :] // HEADDIM
    hk = head[None, :, None]
    uk = used[None, :, None]
    r = (uk & (hk == g * HEADS_PER_GROUP + rj)).astype(np.float32)
    e = (hk == g * HEADS_PER_GROUP + ej)[:, :LANES].astype(np.float32)
    return jnp.asarray(r, BF16), jnp.asarray(e, BF16)


def _per_group(a):
    lead = a.shape[:-1]
    x = a[..., :D_INNER].reshape(*lead, GROUPS, GROUP_W)
    b = a[..., D_INNER:D_INNER + GROUPS * STATE].reshape(*lead, GROUPS, STATE)
    c = a[..., D_INNER + GROUPS * STATE:].reshape(*lead, GROUPS, STATE)
    return jnp.moveaxis(jnp.concatenate([x, b, c], axis=-1), -2, 0)


def _ssd(zxd, xb, wt, conv_w, conv_b, dt_bias, a_log, d_skip, norm_w, conv_prev, state0):
    r_mat, e_mat = _spread_matrices()
    pad = ((0, 0), (0, LANES - HEADS))
    prep = _ssd_prep(xb, wt, jnp.pad(dt_bias.reshape(1, HEADS).astype(F32), pad),
                     jnp.pad(a_log.reshape(1, HEADS).astype(F32), pad))
    params = (_per_group(conv_w), _per_group(conv_b.reshape(1, CONV_DIM)),
              jnp.repeat(d_skip.astype(F32), HEADDIM).reshape(1, D_INNER),
              norm_w.reshape(1, D_INNER).astype(F32), r_mat, e_mat)
    common = (zxd,) * 4 + tuple(prep) + params
    y, s_fin = pl.pallas_call(
        _ssd_main_kernel,
        grid=(GROUPS, T_MAIN // SSD_RB),
        in_specs=_ssd_specs(SSD_RB, 0),
        out_specs=[
            pl.BlockSpec((SSD_RB, GROUP_W), lambda g, c: (c, g)),
            pl.BlockSpec((None, STATE, GROUP_W), lambda g, c: (g, 0, 0)),
        ],
        out_shape=[
            jax.ShapeDtypeStruct((T_ALL, D_INNER), BF16),
            jax.ShapeDtypeStruct((GROUPS, STATE, GROUP_W), F32),
        ],
        scratch_shapes=[
            pltpu.VMEM((CONV_HEAD + SSD_RB, XBC_W), F32),
            pltpu.VMEM((STATE, GROUP_W), F32),
        ],
        compiler_params=_cparams(("parallel", "arbitrary")),
        name="ssd_prompt",
    )(*common)

    n_in = len(common)
    y, s_new = pl.pallas_call(
        _ssd_samp_kernel,
        grid=(GROUPS, N_SAMP_CHUNKS),
        in_specs=_ssd_specs(CHUNK, N_MAIN_CHUNKS) + [
            pl.BlockSpec((None, SEQ_PER_CHUNK, SSD_CONV_WIDTH - 1, XBC_W), lambda g, c: (g, c, 0, 0)),
            pl.BlockSpec((SEQ_PER_CHUNK, None, GROUP_W, STATE), lambda g, c: (c, g, 0, 0)),
            pl.BlockSpec(memory_space=pl.ANY),
        ],
        out_specs=[
            pl.BlockSpec((CHUNK, GROUP_W), lambda g, c: (N_MAIN_CHUNKS + c, g)),
            pl.BlockSpec((SEQ_PER_CHUNK, None, GROUP_W, STATE), lambda g, c: (c, g, 0, 0)),
        ],
        out_shape=[
            jax.ShapeDtypeStruct((T_ALL, D_INNER), BF16),
            jax.ShapeDtypeStruct((DEC_BATCH, GROUPS, GROUP_W, STATE), F32),
        ],
        scratch_shapes=[
            pltpu.VMEM((SEQ_PER_CHUNK * (CONV_HEAD + DEC_SEQ), XBC_W), F32),
            pltpu.VMEM((CHUNK, XBC_W), F32),
            pltpu.VMEM((CHUNK, GROUP_W), F32),
        ],
        input_output_aliases={n_in + 2: 0},
        compiler_params=_cparams(("parallel", "arbitrary")),
        name="ssd_sample",
    )(*common, _per_group(conv_prev), state0, y)
    return y, s_fin, s_new


def _tail_rows(a, n):
    c = a.shape[-1]
    p = a[T_PROMPT - n:T_PROMPT].reshape(1, 1, n, c)
    s = a[T_MAIN:].reshape(DEC_BATCH, DEC_SEQ, c)[:, DEC_SEQ - n:].reshape(1, DEC_BATCH, n, c)
    return p, s


def kernel(x_prompt, x_sample, cache_sc_conv, state_ssd_conv, state_ssd, meta_tokens, ln_g, ln_b,
           ffn_w1, ffn_w3, ffn_w2, sc_w_in, sc_w_conv, sc_w_out,
           ssd_w_in, ssd_w_conv, ssd_b_conv, ssd_dt_bias, ssd_a_log, ssd_d, ssd_norm_w, ssd_w_out):
    w1, w3, w2 = ffn_w1, ffn_w3, ffn_w2
    sc_out_w = sc_w_out.astype(BF16)
    ssd_out_w = ssd_w_out.astype(BF16)
    lng = ln_g.reshape(DEPTH, 3, 1, D_MODEL)
    lnb = ln_b.reshape(DEPTH, 3, 1, D_MODEL)

    h = jnp.concatenate([meta_tokens.astype(F32), x_prompt[0], jnp.zeros((N_PAD, D_MODEL), F32),
                         x_sample.reshape(N_SAMP, D_MODEL)], axis=0)

    h, hb = _ffn(h, w1, w3, w2, lng, lnb, 0, 0, 0, emit_bf16=True)
    v, u_tail = _sc_mix(hb, sc_w_in, 0, sc_w_conv[0], cache_sc_conv[0])
    keep = SC_WIDTH - 1
    new_sc_p = u_tail[T_PROMPT - keep - SC_TAIL0:T_PROMPT - SC_TAIL0].reshape(1, 1, keep, D_MODEL)
    new_sc_s = u_tail[SC_SAMP_OFF:].reshape(DEC_BATCH, DEC_SEQ, D_MODEL)[:, DEC_SEQ - keep:].reshape(
        1, DEC_BATCH, keep, D_MODEL)
    h = _proj_ln(v, sc_out_w, 0, h, lng, lnb, 0, 1)
    h = _ffn(h, w1, w3, w2, lng, lnb, 0, 1, 2)

    h, hb = _ffn(h, w1, w3, w2, lng, lnb, 1, 0, 0, emit_bf16=True)
    ssd_wt = jnp.swapaxes(ssd_w_in[0], 0, 1)
    zxd = _mm(hb, ssd_wt, ZXD_W, ZXD_TN)
    new_conv_p, new_conv_s = _tail_rows(zxd[:, D_INNER:DT_COL], SSD_CONV_WIDTH - 1)
    state0 = state_ssd[0].reshape(DEC_BATCH, GROUPS, GROUP_W, STATE)
    y, s_fin, s_new = _ssd(zxd, hb, ssd_wt, ssd_w_conv[0], ssd_b_conv[0], ssd_dt_bias[0], ssd_a_log[0],
                           ssd_d[0], ssd_norm_w[0], state_ssd_conv[0], state0)
    h = _proj_ln(y, ssd_out_w, 0, h, lng, lnb, 1, 1)
    h = _ffn(h, w1, w3, w2, lng, lnb, 1, 1, 2)

    y_prompt = h[N_META:T_PROMPT].reshape(1, SEQ, D_MODEL)
    y_sample = h[T_MAIN:].reshape(DEC_BATCH, DEC_SEQ, D_MODEL)
    new_ssd_p = jnp.swapaxes(s_fin, 1, 2).reshape(1, 1, HEADS, HEADDIM, STATE).astype(state_ssd.dtype)
    new_ssd_s = s_new.reshape(1, DEC_BATCH, HEADS, HEADDIM, STATE).astype(state_ssd.dtype)
    return (y_prompt, y_sample, new_sc_p, new_sc_s, new_conv_p, new_conv_s, new_ssd_p, new_ssd_s)
```

```python
import functools

import numpy as np

import jax
import jax.numpy as jnp
from jax import lax
from jax.experimental import pallas as pl
from jax.experimental.pallas import tpu as pltpu

F32 = jnp.float32
BF16 = jnp.bfloat16

D_MODEL = 2048
SEQ = 8192
DEPTH = 2
DEC_BATCH = 32
DEC_SEQ = 16
N_META = 16
D_FF = 5632
SC_WIDTH = 3
D_INNER = 4096
HEADDIM = 64
HEADS = 64
GROUPS = 8
HEADS_PER_GROUP = 8
STATE = 128
SSD_CONV_WIDTH = 4
CONV_DIM = D_INNER + 2 * GROUPS * STATE
GROUP_W = HEADS_PER_GROUP * HEADDIM
XBC_W = GROUP_W + 2 * STATE
ALPHA = (2.0 * DEPTH) ** 0.25
LN_EPS = 1e-5
RMS_EPS = 1e-5

CHUNK = 128
T_PROMPT = N_META + SEQ
N_MAIN_CHUNKS = -(-T_PROMPT // CHUNK)
T_MAIN = N_MAIN_CHUNKS * CHUNK
N_PAD = T_MAIN - T_PROMPT
N_SAMP = DEC_BATCH * DEC_SEQ
N_SAMP_CHUNKS = N_SAMP // CHUNK
SEQ_PER_CHUNK = CHUNK // DEC_SEQ
T_ALL = T_MAIN + N_SAMP

TM = 736
TM_IN = 2 * TM
TF = 512
TF_HEAD = 256
TM_WIDE = 1104
FFN_SLAB = 368
PROJ_ROW_SPLIT = 2
SC_ROW_SPLIT = 2
DT_COL = D_INNER + CONV_DIM
ZXD_W = DT_COL
ZXD_TN = 1280
SSD_NB = 13
SSD_RB = SSD_NB * CHUNK
PREP_NB = 3
LANES = 128
SUBLANES = 8
CONV_HEAD = SUBLANES

VMEM_LIMIT_BYTES = 56 * 1024 * 1024
VMEM_LIMIT_WIDE_BYTES = 61 * 1024 * 1024


def _cparams(sem, vmem_limit_bytes=VMEM_LIMIT_BYTES):
    return pltpu.CompilerParams(dimension_semantics=sem, vmem_limit_bytes=vmem_limit_bytes)


def _layer_norm(v, g, b):
    mu = jnp.mean(v, axis=-1, keepdims=True)
    c = v - mu
    var = jnp.mean(c * c, axis=-1, keepdims=True)
    return c * lax.rsqrt(var + LN_EPS) * g + b


def _silu(x):
    h = 0.5 * x
    return h + h * jnp.tanh(h)


def _softplus(x):
    return jnp.maximum(x, 0.0) + jnp.log(1.0 + jnp.exp(-jnp.abs(x)))


def _ffn_kernel(*refs, nf, n_alias, emit_bf16, convert):
    x_ref, w1_ref, w3_ref, w2_ref, g_ref, b_ref = refs[:6]
    outs = refs[6 + n_alias:]
    o_ref = outs[0]
    pos = 1
    ob_ref = None
    if emit_bf16:
        ob_ref = outs[pos]
        pos += 1
    if convert:
        wb_refs = outs[pos:pos + 3]
        for src, dst in zip((w1_ref, w3_ref, w2_ref), wb_refs):
            dst[...] = src[...].astype(BF16)
        w1_ref, w3_ref, w2_ref = wb_refs
        pos += 3
    xb_ref = outs[pos]
    f = pl.program_id(1)

    @pl.when(f == 0)
    def _():
        xb_ref[...] = x_ref[...].astype(BF16)
        o_ref[...] = jnp.zeros_like(o_ref)

    rs = FFN_SLAB
    n_slabs = x_ref.shape[0] // rs

    def slab(s, last):
        rows = pl.ds(s * rs, rs)
        xb = xb_ref[rows, :]
        h1 = jnp.dot(xb, w1_ref[...], preferred_element_type=F32)
        h3 = jnp.dot(xb, w3_ref[...], preferred_element_type=F32)
        gate = (_silu(h1) * h3).astype(BF16)
        acc = o_ref[rows, :] + jnp.dot(gate, w2_ref[...], preferred_element_type=F32)
        if last:
            res = _layer_norm(ALPHA * x_ref[rows, :] + 0.5 * acc, g_ref[...], b_ref[...])
            o_ref[rows, :] = res
            if ob_ref is not None:
                ob_ref[rows, :] = res.astype(BF16)
        else:
            o_ref[rows, :] = acc

    @pl.when(f < nf - 1)
    def _():
        for s in range(n_slabs):
            slab(s, False)

    @pl.when(f == nf - 1)
    def _():
        for s in range(n_slabs):
            slab(s, True)


def _ffn(h, w1, w3, w2, lng, lnb, i, j, k, emit_bf16=False):
    tm, vmem = (TM, VMEM_LIMIT_BYTES) if emit_bf16 else (TM_WIDE, VMEM_LIMIT_WIDE_BYTES)
    ln_spec = pl.BlockSpec((None, None, 1, D_MODEL), lambda m, f: (i, k, 0, 0))
    h_shapes = [jax.ShapeDtypeStruct((T_ALL, D_MODEL), F32)]
    if emit_bf16:
        h_shapes.append(jax.ShapeDtypeStruct((T_ALL, D_MODEL), BF16))
    n_h = len(h_shapes)
    scratch = [pltpu.VMEM((tm, D_MODEL), BF16)]

    head_row = pl.BlockSpec((tm, D_MODEL), lambda m, f: (0, 0))
    nf = D_FF // TF
    nf_head = D_FF // TF_HEAD
    per = TF // TF_HEAD
    head = pl.pallas_call(
        functools.partial(_ffn_kernel, nf=nf_head, n_alias=0, emit_bf16=emit_bf16, convert=True),
        grid=(1, nf_head),
        in_specs=[
            head_row,
            pl.BlockSpec((None, None, D_MODEL, TF_HEAD), lambda m, f: (i, j, 0, f)),
            pl.BlockSpec((None, None, D_MODEL, TF_HEAD), lambda m, f: (i, j, 0, f)),
            pl.BlockSpec((None, None, TF_HEAD, D_MODEL), lambda m, f: (i, j, f, 0)),
            ln_spec, ln_spec,
        ],
        out_specs=[head_row] * n_h + [
            pl.BlockSpec((None, D_MODEL, TF_HEAD), lambda m, f: (f // per, 0, f % per)),
            pl.BlockSpec((None, D_MODEL, TF_HEAD), lambda m, f: (f // per, 0, f % per)),
            pl.BlockSpec((TF_HEAD, D_MODEL), lambda m, f: (f, 0)),
        ],
        out_shape=h_shapes + [
            jax.ShapeDtypeStruct((nf, D_MODEL, TF), BF16),
            jax.ShapeDtypeStruct((nf, D_MODEL, TF), BF16),
            jax.ShapeDtypeStruct((D_FF, D_MODEL), BF16),
        ],
        scratch_shapes=scratch,
        compiler_params=_cparams(("parallel", "arbitrary"), vmem),
        name="ffn_ln_head",
    )(h, w1, w3, w2, lng, lnb)
    h_parts, (w1b, w3b, w2b) = head[:n_h], head[n_h:]

    row = pl.BlockSpec((tm, D_MODEL), lambda m, f: (m + 1, 0))
    out = pl.pallas_call(
        functools.partial(_ffn_kernel, nf=nf, n_alias=n_h, emit_bf16=emit_bf16, convert=False),
        grid=(T_ALL // tm - 1, nf),
        in_specs=[
            row,
            pl.BlockSpec((None, D_MODEL, TF), lambda m, f: (f, 0, 0)),
            pl.BlockSpec((None, D_MODEL, TF), lambda m, f: (f, 0, 0)),
            pl.BlockSpec((TF, D_MODEL), lambda m, f: (f, 0)),
            ln_spec, ln_spec,
        ] + [pl.BlockSpec(memory_space=pl.ANY)] * n_h,
        out_specs=[row] * n_h,
        out_shape=h_shapes,
        scratch_shapes=scratch,
        input_output_aliases={6 + a: a for a in range(n_h)},
        compiler_params=_cparams(("parallel", "arbitrary"), vmem),
        name="ffn_ln",
    )(h, w1b, w3b, w2b, lng, lnb, *h_parts)
    return tuple(out) if emit_bf16 else out[0]


def _mm_kernel(x_ref, wt_ref, o_ref, wb_ref):
    @pl.when(pl.program_id(1) == 0)
    def _():
        wb_ref[...] = wt_ref[...].astype(BF16)

    o_ref[...] = lax.dot_general(x_ref[...], wb_ref[...], (((1,), (1,)), ((), ())),
                                 preferred_element_type=F32)


def _mm(xb, wt, n_out, tn):
    n, k = wt.shape
    assert n_out % tn == 0 and n_out <= n
    return pl.pallas_call(
        _mm_kernel,
        grid=(n_out // tn, T_ALL // TM_IN),
        in_specs=[
            pl.BlockSpec((TM_IN, k), lambda j, m: (m, 0)),
            pl.BlockSpec((tn, k), lambda j, m: (j, 0)),
        ],
        out_specs=pl.BlockSpec((TM_IN, tn), lambda j, m: (m, j)),
        out_shape=jax.ShapeDtypeStruct((T_ALL, n_out), F32),
        scratch_shapes=[pltpu.VMEM((tn, k), BF16)],
        compiler_params=_cparams(("parallel", "arbitrary"), VMEM_LIMIT_WIDE_BYTES),
        name="in_proj",
    )(xb, wt)


SC_TAIL0 = T_ALL - TM
SC_SAMP_OFF = T_MAIN - SC_TAIL0
assert SC_SAMP_OFF % DEC_SEQ == 0 and T_PROMPT - (SC_WIDTH - 1) >= SC_TAIL0


def _sc_mix_kernel(x_ref, wb_ref, wc_ref, wh_ref, cw_ref, o1_ref, o2_ref, v_ref, ut_ref,
                   wbb_ref, wcb_ref, whb_ref, ubuf_ref, *, nm):
    m = pl.program_id(1)
    tn = ubuf_ref.shape[1]

    @pl.when(m == 0)
    def _():
        wbb_ref[...] = wb_ref[...].astype(BF16)
        wcb_ref[...] = wc_ref[...].astype(BF16)
        whb_ref[...] = wh_ref[...].astype(BF16)
        ubuf_ref[0:CONV_HEAD, :] = jnp.zeros((CONV_HEAD, tn), F32)

    cw = cw_ref[...]
    ov_refs = (o1_ref, o2_ref)
    rs = TM // SC_ROW_SPLIT

    def slab(s, last_tile):
        r0 = s * rs
        xb = x_ref[r0:r0 + rs, :]
        bg = jnp.dot(xb, wbb_ref[...], preferred_element_type=F32)
        c = jnp.dot(xb, wcb_ref[...], preferred_element_type=F32)
        hh = jnp.dot(xb, whb_ref[...], preferred_element_type=F32)
        u = c * hh
        ubuf_ref[CONV_HEAD + r0:CONV_HEAD + r0 + rs, :] = u
        acc = cw[SC_WIDTH - 1:SC_WIDTH, :] * u
        if last_tile:
            lrow = lax.broadcasted_iota(jnp.int32, (rs, 1), 0) + r0
            spos = jnp.bitwise_and(lrow, DEC_SEQ - 1)
        for d in range(1, SC_WIDTH):
            ud = ubuf_ref[CONV_HEAD + r0 - d:CONV_HEAD + r0 - d + rs, :]
            if last_tile:
                first = jnp.logical_and(lrow >= SC_SAMP_OFF, spos < d)
                ud = jnp.where(first, ov_refs[d - 1][r0:r0 + rs, :], ud)
            acc = acc + cw[SC_WIDTH - 1 - d:SC_WIDTH - d, :] * ud
        v_ref[r0:r0 + rs, :] = (bg * acc).astype(BF16)
        if last_tile:
            ut_ref[r0:r0 + rs, :] = u

    @pl.when(m < nm - 1)
    def _():
        for s in range(SC_ROW_SPLIT):
            slab(s, False)

    @pl.when(m == nm - 1)
    def _():
        for s in range(SC_ROW_SPLIT):
            slab(s, True)

    ubuf_ref[0:CONV_HEAD, :] = ubuf_ref[TM:TM + CONV_HEAD, :]


def _sc_mix(xb, w_in, j, w_conv, conv_prev):
    tn = 512
    nb = D_MODEL // tn
    nm = T_ALL // TM
    ovs = [jnp.pad(o, ((SC_SAMP_OFF, 0), (0, 0))) for o in _conv_overrides(conv_prev, SC_WIDTH)]
    tail = pl.BlockSpec((TM, tn), lambda n, m: (0, n))
    wscratch = pltpu.VMEM((D_MODEL, tn), BF16)
    return pl.pallas_call(
        functools.partial(_sc_mix_kernel, nm=nm),
        grid=(nb, nm),
        in_specs=[
            pl.BlockSpec((TM, D_MODEL), lambda n, m: (m, 0)),
            pl.BlockSpec((None, D_MODEL, tn), lambda n, m: (j, 0, n)),
            pl.BlockSpec((None, D_MODEL, tn), lambda n, m: (j, 0, n + nb)),
            pl.BlockSpec((None, D_MODEL, tn), lambda n, m: (j, 0, n + 2 * nb)),
            pl.BlockSpec((SC_WIDTH, tn), lambda n, m: (0, n)),
            tail, tail,
        ],
        out_specs=[pl.BlockSpec((TM, tn), lambda n, m: (m, n)), tail],
        out_shape=[jax.ShapeDtypeStruct((T_ALL, D_MODEL), BF16),
                   jax.ShapeDtypeStruct((TM, D_MODEL), F32)],
        scratch_shapes=[wscratch, wscratch, wscratch, pltpu.VMEM((CONV_HEAD + TM, tn), F32)],
        compiler_params=_cparams(("parallel", "arbitrary")),
        name="sc_mix",
    )(xb, w_in, w_in, w_in, w_conv.astype(F32), *ovs)


def _conv_overrides(prev, taps):
    c = prev.shape[-1]
    out = []
    for d in range(1, taps):
        o = jnp.zeros((DEC_BATCH, DEC_SEQ, c), F32)
        for p in range(d):
            o = o.at[:, p].set(prev[:, taps - 1 + p - d])
        out.append(o.reshape(N_SAMP, c))
    return out


def _proj_ln_kernel(y_ref, w_ref, h_ref, g_ref, b_ref, o_ref, *, nk):
    rs = TM // PROJ_ROW_SPLIT

    def slab_dot(rows):
        return jnp.dot(y_ref[rows, :], w_ref[...], preferred_element_type=F32)

    def finish(rows, acc):
        v = ALPHA * h_ref[rows, :] + acc
        o_ref[rows, :] = _layer_norm(v, g_ref[...], b_ref[...])

    slabs = [pl.ds(s * rs, rs) for s in range(PROJ_ROW_SPLIT)]
    if nk == 1:
        for rows in slabs:
            finish(rows, slab_dot(rows))
        return

    k = pl.program_id(1)

    @pl.when(k == 0)
    def _():
        for rows in slabs:
            o_ref[rows, :] = slab_dot(rows)

    @pl.when(k == 1)
    def _():
        for rows in slabs:
            finish(rows, o_ref[rows, :] + slab_dot(rows))


def _proj_ln(y, w, j, h, lng, lnb, i, k):
    kdim = w.shape[1]
    tk = 2048
    nk = kdim // tk
    assert nk in (1, 2)
    return pl.pallas_call(
        functools.partial(_proj_ln_kernel, nk=nk),
        grid=(T_ALL // TM, nk),
        in_specs=[
            pl.BlockSpec((TM, tk), lambda m, kk: (m, kk)),
            pl.BlockSpec((None, tk, D_MODEL), lambda m, kk: (j, kk, 0)),
            pl.BlockSpec((TM, D_MODEL), lambda m, kk: (m, 0)),
            pl.BlockSpec((None, None, 1, D_MODEL), lambda m, kk: (i, k, 0, 0)),
            pl.BlockSpec((None, None, 1, D_MODEL), lambda m, kk: (i, k, 0, 0)),
        ],
        out_specs=pl.BlockSpec((TM, D_MODEL), lambda m, kk: (m, 0)),
        out_shape=jax.ShapeDtypeStruct((T_ALL, D_MODEL), F32),
        compiler_params=_cparams(("parallel", "arbitrary")),
        name="out_proj_ln",
    )(y, w, h, lng, lnb)


_NT = (((1,), (1,)), ((), ()))
_TN = (((0,), (0,)), ((), ()))
SPLIT_K = 2 * LANES


def _bf16_pieces(v):
    hi = v.astype(BF16)
    r1 = v - hi.astype(F32)
    mid = r1.astype(BF16)
    lo = (r1 - mid.astype(F32)).astype(BF16)
    return hi, mid, lo


def _split_pack(v, lane_lo):
    hi, mid, lo = _bf16_pieces(v)
    a = jnp.where(lane_lo, hi.astype(F32), pltpu.roll(mid.astype(F32), HEADS, axis=1))
    b = jnp.where(lane_lo, lo.astype(F32), 0.0)
    return jnp.concatenate([a, b], axis=1).astype(BF16)


def _split_pack2(v, lane_lo):
    hi = v.astype(BF16)
    mid = (v - hi.astype(F32)).astype(BF16)
    return jnp.where(lane_lo, hi.astype(F32), pltpu.roll(mid.astype(F32), HEADS, axis=1)).astype(BF16)


def _masked_cumsum(mask_bf, da):
    hi, mid, lo = _bf16_pieces(da)
    p = jnp.dot(mask_bf, jnp.concatenate([hi, mid, lo], axis=1), preferred_element_type=F32)
    return p[:, :LANES] + p[:, LANES:2 * LANES] + p[:, 2 * LANES:]


def _conv_silu(buf_ref, base, n, cw, cbias):
    taps = SSD_CONV_WIDTH
    acc = cw[taps - 1:taps, :] * buf_ref[base:base + n, :]
    for d in range(1, taps):
        acc = acc + cw[taps - 1 - d:taps - d, :] * buf_ref[base - d:base - d + n, :]
    return _silu(acc + cbias)


def _ssd_prep_kernel(x_ref, wdt_ref, dtb_ref, al_ref, pk_acs_ref, pk3_ref, acst_ref, tott_ref):
    li, si = _chunk_iotas()
    lane_lo = si < HEADS
    same_seq = (li // DEC_SEQ) == (si // DEC_SEQ)
    a_row = -jnp.exp(al_ref[...])
    wdt = wdt_ref[...].astype(BF16)
    wdt = jnp.concatenate([wdt, jnp.zeros_like(wdt)], axis=0)
    dt_raw = lax.dot_general(x_ref[...], wdt, _NT, preferred_element_type=F32)
    for k in range(PREP_NB):
        c = pl.program_id(0) * PREP_NB + k
        rows = pl.ds(k * CHUNK, CHUNK)
        same = jnp.logical_or(c < N_MAIN_CHUNKS, same_seq)
        causal = jnp.logical_and(si <= li, same)
        row = lax.broadcasted_iota(jnp.int32, (CHUNK, 1), 0) + c * CHUNK
        valid = jnp.logical_or(row < T_PROMPT, row >= T_MAIN)
        dt = jnp.where(jnp.logical_and(valid, lane_lo),
                       _softplus(dt_raw[k * CHUNK:(k + 1) * CHUNK, :] + dtb_ref[...]), 0.0)
        da = dt * a_row
        acs = _masked_cumsum(causal.astype(F32).astype(BF16), da)
        tot = _masked_cumsum(same.astype(F32).astype(BF16), da)
        pk_acs_ref[rows, :] = _split_pack(acs, lane_lo)
        pk3_ref[rows, :] = jnp.concatenate([_split_pack2(dt, lane_lo), _split_pack2(jnp.exp(tot - acs), lane_lo),
                                            _split_pack2(jnp.exp(acs), lane_lo)], axis=1)
        acst_ref[k] = acs.T
        tott_ref[k] = tot.T


def _ssd_prep(xb, wt, dt_bias, a_log):
    n_chunks = T_ALL // CHUNK
    rows = PREP_NB * CHUNK
    tr = pl.BlockSpec((PREP_NB, LANES, CHUNK), lambda c: (c, 0, 0))
    return pl.pallas_call(
        _ssd_prep_kernel,
        grid=(n_chunks // PREP_NB,),
        in_specs=[
            pl.BlockSpec((rows, D_MODEL), lambda c: (c, 0)),
            pl.BlockSpec((HEADS, D_MODEL), lambda c: (DT_COL // HEADS, 0)),
            pl.BlockSpec((1, LANES), lambda c: (0, 0)),
            pl.BlockSpec((1, LANES), lambda c: (0, 0)),
        ],
        out_specs=[
            pl.BlockSpec((rows, SPLIT_K), lambda c: (c, 0)),
            pl.BlockSpec((rows, 3 * LANES), lambda c: (c, 0)),
            tr, tr,
        ],
        out_shape=[
            jax.ShapeDtypeStruct((T_ALL, SPLIT_K), BF16),
            jax.ShapeDtypeStruct((T_ALL, 3 * LANES), BF16),
            jax.ShapeDtypeStruct((n_chunks, LANES, CHUNK), F32),
            jax.ShapeDtypeStruct((n_chunks, LANES, CHUNK), F32),
        ],
        compiler_params=_cparams(("parallel",)),
        name="ssd_prep",
    )(xb, wt, dt_bias, a_log)


def _intra_chunk(xbc, pk_acs, pk3, arow, causal, lane_lo, r_ref, e_ref):
    xs = xbc[:, :GROUP_W]
    bmb = xbc[:, GROUP_W:GROUP_W + STATE].astype(BF16)
    cmb = xbc[:, GROUP_W + STATE:].astype(BF16)
    colmat = jnp.dot(pk_acs, r_ref[...], preferred_element_type=F32)
    packed = jnp.concatenate([pk3[:, :LANES], pk3[:, LANES:2 * LANES], pk3[:, 2 * LANES:]], axis=0)
    ex = jnp.dot(packed, e_ref[...], preferred_element_type=F32)
    dtx, tex, eax = ex[:CHUNK], ex[CHUNK:2 * CHUNK], ex[2 * CHUNK:]
    xdt = xs * dtx
    cb = lax.dot_general(cmb, bmb, _NT, preferred_element_type=F32)
    ys = []
    for q in range(HEADS_PER_GROUP // 2):
        scs = []
        for r in (2 * q, 2 * q + 1):
            seg = colmat[:, r * CHUNK:(r + 1) * CHUNK] - arow[r:r + 1, :]
            scs.append((cb * jnp.exp(jnp.where(causal, seg, -jnp.inf))).astype(BF16))
        xp = xdt[:, q * LANES:(q + 1) * LANES]
        rhs = jnp.concatenate([jnp.where(lane_lo, xp, 0.0).astype(BF16),
                               jnp.where(lane_lo, 0.0, xp).astype(BF16)], axis=0)
        ys.append(jnp.dot(jnp.concatenate(scs, axis=1), rhs, preferred_element_type=F32))
    y_intra = jnp.concatenate(ys, axis=1)
    xw = (xdt * tex).astype(BF16)
    return xs, bmb, cmb, y_intra, eax, xw


def _gate_norm(y, z, nw):
    y = y * _silu(z)
    ms = jnp.mean(y * y, axis=-1, keepdims=True)
    return (y * lax.rsqrt(ms + RMS_EPS) * nw).astype(BF16)


def _chunk_iotas():
    li = lax.broadcasted_iota(jnp.int32, (CHUNK, CHUNK), 0)
    si = lax.broadcasted_iota(jnp.int32, (CHUNK, CHUNK), 1)
    return li, si


def _ssd_main_kernel(z_ref, xr_ref, br_ref, cr_ref, pk_acs_ref, pk3_ref, acst_ref, tott_ref, cw_ref, cbias_ref,
                     dx_ref, nw_ref, r_ref, e_ref, y_ref, sfin_ref, buf_ref, st_ref):
    del tott_ref
    g = pl.program_id(0)
    rb = pl.program_id(1)

    @pl.when(rb == 0)
    def _():
        st_ref[...] = jnp.zeros_like(st_ref)
        buf_ref[0:CONV_HEAD, :] = jnp.zeros((CONV_HEAD, XBC_W), F32)

    @pl.when(rb > 0)
    def _():
        buf_ref[0:CONV_HEAD, :] = buf_ref[SSD_RB:SSD_RB + CONV_HEAD, :]

    buf_ref[CONV_HEAD:, 0:GROUP_W] = xr_ref[...]
    buf_ref[CONV_HEAD:, GROUP_W:GROUP_W + STATE] = br_ref[...]
    buf_ref[CONV_HEAD:, GROUP_W + STATE:] = cr_ref[...]

    li, si = _chunk_iotas()
    causal = si <= li
    lane_lo = si < HEADS
    g8 = pl.multiple_of(g * HEADS_PER_GROUP, SUBLANES)
    cw = cw_ref[...]
    cbias = cbias_ref[...]

    for k in range(SSD_NB):
        rows = pl.ds(k * CHUNK, CHUNK)
        xbc = _conv_silu(buf_ref, CONV_HEAD + k * CHUNK, CHUNK, cw, cbias)
        arow = acst_ref[k, pl.ds(g8, HEADS_PER_GROUP), :]
        xs, bmb, cmb, y_intra, eax, xw = _intra_chunk(xbc, pk_acs_ref[rows, :], pk3_ref[rows, :], arow,
                                                      causal, lane_lo, r_ref, e_ref)
        st = st_ref[...]
        y = y_intra + jnp.dot(cmb, st.astype(BF16), preferred_element_type=F32) * eax + dx_ref[...] * xs
        y_ref[rows, :] = _gate_norm(y, z_ref[rows, :], nw_ref[...])
        st_ref[...] = st * eax[CHUNK - 1:CHUNK, :] + lax.dot_general(bmb, xw, _TN, preferred_element_type=F32)

    @pl.when(rb == T_MAIN // SSD_RB - 1)
    def _():
        sfin_ref[...] = st_ref[...]


def _ssd_samp_kernel(z_ref, xr_ref, br_ref, cr_ref, pk_acs_ref, pk3_ref, acst_ref, tott_ref, cw_ref, cbias_ref,
                     dx_ref, nw_ref, r_ref, e_ref, prev_ref, s0_ref, yprev_ref, y_ref, s1_ref,
                     buf_ref, xbc_ref, yint_ref):
    del yprev_ref
    g = pl.program_id(0)
    slot = CONV_HEAD + DEC_SEQ
    cw = cw_ref[...]
    cbias = cbias_ref[...]
    for q in range(SEQ_PER_CHUNK):
        r0 = q * DEC_SEQ
        top = q * slot + CONV_HEAD
        buf_ref[top - (SSD_CONV_WIDTH - 1):top, :] = prev_ref[q]
        buf_ref[top:top + DEC_SEQ, 0:GROUP_W] = xr_ref[r0:r0 + DEC_SEQ, :]
        buf_ref[top:top + DEC_SEQ, GROUP_W:GROUP_W + STATE] = br_ref[r0:r0 + DEC_SEQ, :]
        buf_ref[top:top + DEC_SEQ, GROUP_W + STATE:] = cr_ref[r0:r0 + DEC_SEQ, :]
    for q in range(SEQ_PER_CHUNK):
        r0 = q * DEC_SEQ
        xbc_ref[r0:r0 + DEC_SEQ, :] = _conv_silu(buf_ref, q * slot + CONV_HEAD, DEC_SEQ, cw, cbias)

    li, si = _chunk_iotas()
    same = (li // DEC_SEQ) == (si // DEC_SEQ)
    causal = jnp.logical_and(si <= li, same)
    lane_lo = si < HEADS
    g8 = pl.multiple_of(g * HEADS_PER_GROUP, SUBLANES)

    arow = acst_ref[0, pl.ds(g8, HEADS_PER_GROUP), :]
    trow = tott_ref[0, pl.ds(g8, HEADS_PER_GROUP), :]
    xs, bmb, cmb, y_intra, eax, xw = _intra_chunk(xbc_ref[...], pk_acs_ref[...], pk3_ref[...], arow,
                                                  causal, lane_lo, r_ref, e_ref)
    for q in range(SEQ_PER_CHUNK):
        r0 = q * DEC_SEQ
        yint_ref[r0:r0 + DEC_SEQ, :] = lax.dot_general(cmb[r0:r0 + DEC_SEQ, :], s0_ref[q].astype(BF16), _NT,
                                                       preferred_element_type=F32)
    y = y_intra + yint_ref[...] * eax + dx_ref[...] * xs
    y_ref[...] = _gate_norm(y, z_ref[...], nw_ref[...])
    for q in range(SEQ_PER_CHUNK):
        r0 = q * DEC_SEQ
        upd = lax.dot_general(xw[r0:r0 + DEC_SEQ, :], bmb[r0:r0 + DEC_SEQ, :], _TN, preferred_element_type=F32)
        for r in range(HEADS_PER_GROUP):
            lo, hi = r * HEADDIM, (r + 1) * HEADDIM
            s1_ref[q, lo:hi, :] = s0_ref[q, lo:hi, :] * jnp.exp(trow[r:r + 1, r0:r0 + 1]) + upd[lo:hi, :]


def _ssd_specs(rows, row0):
    xcol = D_INNER // GROUP_W
    bcol = (2 * D_INNER) // STATE
    return [
        pl.BlockSpec((rows, GROUP_W), lambda g, c: (row0 + c, g)),
        pl.BlockSpec((rows, GROUP_W), lambda g, c: (row0 + c, xcol + g)),
        pl.BlockSpec((rows, STATE), lambda g, c: (row0 + c, bcol + g)),
        pl.BlockSpec((rows, STATE), lambda g, c: (row0 + c, bcol + GROUPS + g)),
        pl.BlockSpec((rows, SPLIT_K), lambda g, c: (row0 + c, 0)),
        pl.BlockSpec((rows, 3 * LANES), lambda g, c: (row0 + c, 0)),
        pl.BlockSpec((rows // CHUNK, LANES, CHUNK), lambda g, c: (row0 + c, 0, 0)),
        pl.BlockSpec((rows // CHUNK, LANES, CHUNK), lambda g, c: (row0 + c, 0, 0)),
        pl.BlockSpec((None, SSD_CONV_WIDTH, XBC_W), lambda g, c: (g, 0, 0)),
        pl.BlockSpec((None, 1, XBC_W), lambda g, c: (g, 0, 0)),
        pl.BlockSpec((1, GROUP_W), lambda g, c: (0, g)),
        pl.BlockSpec((1, GROUP_W), lambda g, c: (0, g)),
        pl.BlockSpec((None, SPLIT_K, HEADS_PER_GROUP * CHUNK), lambda g, c: (g, 0, 0)),
        pl.BlockSpec((None, LANES, GROUP_W), lambda g, c: (g, 0, 0)),
    ]


def _spread_matrices():
    k = np.arange(SPLIT_K)
    head = k % HEADS
    used = (k // HEADS) < 3
    g = np.arange(GROUPS)[:, None, None]
    rj = np.arange(HEADS_PER_GROUP * CHUNK)[None, None, :] // CHUNK
    ej = np.arange(GROUP_W)[None, None, :] // HEADDIM
    hk = head[None, :, None]
    uk = used[None, :, None]
    r = (uk & (hk == g * HEADS_PER_GROUP + rj)).astype(np.float32)
    e = (hk == g * HEADS_PER_GROUP + ej)[:, :LANES].astype(np.float32)
    return jnp.asarray(r, BF16), jnp.asarray(e, BF16)


def _per_group(a):
    lead = a.shape[:-1]
    x = a[..., :D_INNER].reshape(*lead, GROUPS, GROUP_W)
    b = a[..., D_INNER:D_INNER + GROUPS * STATE].reshape(*lead, GROUPS, STATE)
    c = a[..., D_INNER + GROUPS * STATE:].reshape(*lead, GROUPS, STATE)
    return jnp.moveaxis(jnp.concatenate([x, b, c], axis=-1), -2, 0)


def _ssd(zxd, xb, wt, conv_w, conv_b, dt_bias, a_log, d_skip, norm_w, conv_prev, state0):
    r_mat, e_mat = _spread_matrices()
    pad = ((0, 0), (0, LANES - HEADS))
    prep = _ssd_prep(xb, wt, jnp.pad(dt_bias.reshape(1, HEADS).astype(F32), pad),
                     jnp.pad(a_log.reshape(1, HEADS).astype(F32), pad))
    params = (_per_group(conv_w), _per_group(conv_b.reshape(1, CONV_DIM)),
              jnp.repeat(d_skip.astype(F32), HEADDIM).reshape(1, D_INNER),
              norm_w.reshape(1, D_INNER).astype(F32), r_mat, e_mat)
    common = (zxd,) * 4 + tuple(prep) + params
    y, s_fin = pl.pallas_call(
        _ssd_main_kernel,
        grid=(GROUPS, T_MAIN // SSD_RB),
        in_specs=_ssd_specs(SSD_RB, 0),
        out_specs=[
            pl.BlockSpec((SSD_RB, GROUP_W), lambda g, c: (c, g)),
            pl.BlockSpec((None, STATE, GROUP_W), lambda g, c: (g, 0, 0)),
        ],
        out_shape=[
            jax.ShapeDtypeStruct((T_ALL, D_INNER), BF16),
            jax.ShapeDtypeStruct((GROUPS, STATE, GROUP_W), F32),
        ],
        scratch_shapes=[
            pltpu.VMEM((CONV_HEAD + SSD_RB, XBC_W), F32),
            pltpu.VMEM((STATE, GROUP_W), F32),
        ],
        compiler_params=_cparams(("parallel", "arbitrary")),
        name="ssd_prompt",
    )(*common)

    n_in = len(common)
    y, s_new = pl.pallas_call(
        _ssd_samp_kernel,
        grid=(GROUPS, N_SAMP_CHUNKS),
        in_specs=_ssd_specs(CHUNK, N_MAIN_CHUNKS) + [
            pl.BlockSpec((None, SEQ_PER_CHUNK, SSD_CONV_WIDTH - 1, XBC_W), lambda g, c: (g, c, 0, 0)),
            pl.BlockSpec((SEQ_PER_CHUNK, None, GROUP_W, STATE), lambda g, c: (c, g, 0, 0)),
            pl.BlockSpec(memory_space=pl.ANY),
        ],
        out_specs=[
            pl.BlockSpec((CHUNK, GROUP_W), lambda g, c: (N_MAIN_CHUNKS + c, g)),
            pl.BlockSpec((SEQ_PER_CHUNK, None, GROUP_W, STATE), lambda g, c: (c, g, 0, 0)),
        ],
        out_shape=[
            jax.ShapeDtypeStruct((T_ALL, D_INNER), BF16),
            jax.ShapeDtypeStruct((DEC_BATCH, GROUPS, GROUP_W, STATE), F32),
        ],
        scratch_shapes=[
            pltpu.VMEM((SEQ_PER_CHUNK * (CONV_HEAD + DEC_SEQ), XBC_W), F32),
            pltpu.VMEM((CHUNK, XBC_W), F32),
            pltpu.VMEM((CHUNK, GROUP_W), F32),
        ],
        input_output_aliases={n_in + 2: 0},
        compiler_params=_cparams(("parallel", "arbitrary")),
        name="ssd_sample",
    )(*common, _per_group(conv_prev), state0, y)
    return y, s_fin, s_new


def _tail_rows(a, n):
    c = a.shape[-1]
    p = a[T_PROMPT - n:T_PROMPT].reshape(1, 1, n, c)
    s = a[T_MAIN:].reshape(DEC_BATCH, DEC_SEQ, c)[:, DEC_SEQ - n:].reshape(1, DEC_BATCH, n, c)
    return p, s


def kernel(x_prompt, x_sample, cache_sc_conv, state_ssd_conv, state_ssd, meta_tokens, ln_g, ln_b,
           ffn_w1, ffn_w3, ffn_w2, sc_w_in, sc_w_conv, sc_w_out,
           ssd_w_in, ssd_w_conv, ssd_b_conv, ssd_dt_bias, ssd_a_log, ssd_d, ssd_norm_w, ssd_w_out):
    w1, w3, w2 = ffn_w1, ffn_w3, ffn_w2
    sc_out_w = sc_w_out.astype(BF16)
    ssd_out_w = ssd_w_out.astype(BF16)
    lng = ln_g.reshape(DEPTH, 3, 1, D_MODEL)
    lnb = ln_b.reshape(DEPTH, 3, 1, D_MODEL)

    h = jnp.concatenate([meta_tokens.astype(F32), x_prompt[0], jnp.zeros((N_PAD, D_MODEL), F32),
                         x_sample.reshape(N_SAMP, D_MODEL)], axis=0)

    h, hb = _ffn(h, w1, w3, w2, lng, lnb, 0, 0, 0, emit_bf16=True)
    v, u_tail = _sc_mix(hb, sc_w_in, 0, sc_w_conv[0], cache_sc_conv[0])
    keep = SC_WIDTH - 1
    new_sc_p = u_tail[T_PROMPT - keep - SC_TAIL0:T_PROMPT - SC_TAIL0].reshape(1, 1, keep, D_MODEL)
    new_sc_s = u_tail[SC_SAMP_OFF:].reshape(DEC_BATCH, DEC_SEQ, D_MODEL)[:, DEC_SEQ - keep:].reshape(
        1, DEC_BATCH, keep, D_MODEL)
    h = _proj_ln(v, sc_out_w, 0, h, lng, lnb, 0, 1)
    h = _ffn(h, w1, w3, w2, lng, lnb, 0, 1, 2)

    h, hb = _ffn(h, w1, w3, w2, lng, lnb, 1, 0, 0, emit_bf16=True)
    ssd_wt = jnp.swapaxes(ssd_w_in[0], 0, 1)
    zxd = _mm(hb, ssd_wt, ZXD_W, ZXD_TN)
    new_conv_p, new_conv_s = _tail_rows(zxd[:, D_INNER:DT_COL], SSD_CONV_WIDTH - 1)
    state0 = state_ssd[0].reshape(DEC_BATCH, GROUPS, GROUP_W, STATE)
    y, s_fin, s_new = _ssd(zxd, hb, ssd_wt, ssd_w_conv[0], ssd_b_conv[0], ssd_dt_bias[0], ssd_a_log[0],
                           ssd_d[0], ssd_norm_w[0], state_ssd_conv[0], state0)
    h = _proj_ln(y, ssd_out_w, 0, h, lng, lnb, 1, 1)
    h = _ffn(h, w1, w3, w2, lng, lnb, 1, 1, 2)

    y_prompt = h[N_META:T_PROMPT].reshape(1, SEQ, D_MODEL)
    y_sample = h[T_MAIN:].reshape(DEC_BATCH, DEC_SEQ, D_MODEL)
    new_ssd_p = jnp.swapaxes(s_fin, 1, 2).reshape(1, 1, HEADS, HEADDIM, STATE).astype(state_ssd.dtype)
    new_ssd_s = s_new.reshape(1, DEC_BATCH, HEADS, HEADDIM, STATE).astype(state_ssd.dtype)
    return (y_prompt, y_sample, new_sc_p, new_sc_s, new_conv_p, new_conv_s, new_ssd_p, new_ssd_s)
```

```python
import functools

import numpy as np

import jax
import jax.numpy as jnp
from jax import lax
from jax.experimental import pallas as pl
from jax.experimental.pallas import tpu as pltpu

F32 = jnp.float32
BF16 = jnp.bfloat16

D_MODEL = 2048
SEQ = 8192
DEPTH = 2
DEC_BATCH = 32
DEC_SEQ = 16
N_META = 16
D_FF = 5632
SC_WIDTH = 3
D_INNER = 4096
HEADDIM = 64
HEADS = 64
GROUPS = 8
HEADS_PER_GROUP = 8
STATE = 128
SSD_CONV_WIDTH = 4
CONV_DIM = D_INNER + 2 * GROUPS * STATE
GROUP_W = HEADS_PER_GROUP * HEADDIM
XBC_W = GROUP_W + 2 * STATE
ALPHA = (2.0 * DEPTH) ** 0.25
LN_EPS = 1e-5
RMS_EPS = 1e-5

CHUNK = 128
T_PROMPT = N_META + SEQ
N_MAIN_CHUNKS = -(-T_PROMPT // CHUNK)
T_MAIN = N_MAIN_CHUNKS * CHUNK
N_PAD = T_MAIN - T_PROMPT
N_SAMP = DEC_BATCH * DEC_SEQ
N_SAMP_CHUNKS = N_SAMP // CHUNK
SEQ_PER_CHUNK = CHUNK // DEC_SEQ
T_ALL = T_MAIN + N_SAMP

TM = 736
TM_IN = 2 * TM
TF = 512
TF_HEAD = 256
TM_WIDE = 1104
FFN_SLAB = 368
SC_ROW_SPLIT = 2
DT_COL = D_INNER + CONV_DIM
ZXD_W = DT_COL
ZXD_TN = 1280
SSD_NB = 13
SSD_RB = SSD_NB * CHUNK
PREP_NB = 3
LANES = 128
SUBLANES = 8
CONV_HEAD = SUBLANES

VMEM_LIMIT_BYTES = 56 * 1024 * 1024
VMEM_LIMIT_WIDE_BYTES = 61 * 1024 * 1024


def _cparams(sem, vmem_limit_bytes=VMEM_LIMIT_BYTES):
    return pltpu.CompilerParams(dimension_semantics=sem, vmem_limit_bytes=vmem_limit_bytes)


def _layer_norm(v, g, b):
    mu = jnp.mean(v, axis=-1, keepdims=True)
    c = v - mu
    var = jnp.mean(c * c, axis=-1, keepdims=True)
    return c * lax.rsqrt(var + LN_EPS) * g + b


def _silu(x):
    h = 0.5 * x
    return h + h * jnp.tanh(h)


def _softplus(x):
    return jnp.maximum(x, 0.0) + jnp.log(1.0 + jnp.exp(-jnp.abs(x)))


def _ffn_kernel(*refs, nf, n_alias, emit_bf16, convert):
    x_ref, w1_ref, w3_ref, w2_ref, g_ref, b_ref = refs[:6]
    outs = refs[6 + n_alias:]
    o_ref = outs[0]
    pos = 1
    ob_ref = None
    if emit_bf16:
        ob_ref = outs[pos]
        pos += 1
    if convert:
        wb_refs = outs[pos:pos + 3]
        for src, dst in zip((w1_ref, w3_ref, w2_ref), wb_refs):
            dst[...] = src[...].astype(BF16)
        w1_ref, w3_ref, w2_ref = wb_refs
        pos += 3
    xb_ref = outs[pos]
    f = pl.program_id(1)

    @pl.when(f == 0)
    def _():
        xb_ref[...] = x_ref[...].astype(BF16)
        o_ref[...] = jnp.zeros_like(o_ref)

    rs = FFN_SLAB
    n_slabs = x_ref.shape[0] // rs

    def slab(s, last):
        rows = pl.ds(s * rs, rs)
        xb = xb_ref[rows, :]
        h1 = jnp.dot(xb, w1_ref[...], preferred_element_type=F32)
        h3 = jnp.dot(xb, w3_ref[...], preferred_element_type=F32)
        gate = (_silu(h1) * h3).astype(BF16)
        acc = o_ref[rows, :] + jnp.dot(gate, w2_ref[...], preferred_element_type=F32)
        if last:
            res = _layer_norm(ALPHA * x_ref[rows, :] + 0.5 * acc, g_ref[...], b_ref[...])
            o_ref[rows, :] = res
            if ob_ref is not None:
                ob_ref[rows, :] = res.astype(BF16)
        else:
            o_ref[rows, :] = acc

    @pl.when(f < nf - 1)
    def _():
        for s in range(n_slabs):
            slab(s, False)

    @pl.when(f == nf - 1)
    def _():
        for s in range(n_slabs):
            slab(s, True)


def _ffn(h, w1, w3, w2, lng, lnb, i, j, k, emit_bf16=False):
    tm, vmem = TM_WIDE, VMEM_LIMIT_WIDE_BYTES
    ln_spec = pl.BlockSpec((None, None, 1, D_MODEL), lambda m, f: (i, k, 0, 0))
    h_shapes = [jax.ShapeDtypeStruct((T_ALL, D_MODEL), F32)]
    if emit_bf16:
        h_shapes.append(jax.ShapeDtypeStruct((T_ALL, D_MODEL), BF16))
    n_h = len(h_shapes)
    scratch = [pltpu.VMEM((tm, D_MODEL), BF16)]

    head_row = pl.BlockSpec((tm, D_MODEL), lambda m, f: (0, 0))
    head_row_once = pl.BlockSpec((tm, D_MODEL), lambda m, f: (0, 0), pipeline_mode=pl.Buffered(1))
    nf = D_FF // TF
    nf_head = D_FF // TF_HEAD
    per = TF // TF_HEAD
    head = pl.pallas_call(
        functools.partial(_ffn_kernel, nf=nf_head, n_alias=0, emit_bf16=emit_bf16, convert=True),
        grid=(1, nf_head),
        in_specs=[
            head_row,
            pl.BlockSpec((None, None, D_MODEL, TF_HEAD), lambda m, f: (i, j, 0, f)),
            pl.BlockSpec((None, None, D_MODEL, TF_HEAD), lambda m, f: (i, j, 0, f)),
            pl.BlockSpec((None, None, TF_HEAD, D_MODEL), lambda m, f: (i, j, f, 0)),
            ln_spec, ln_spec,
        ],
        out_specs=[head_row, head_row_once][:n_h] + [
            pl.BlockSpec((None, D_MODEL, TF_HEAD), lambda m, f: (f // per, 0, f % per)),
            pl.BlockSpec((None, D_MODEL, TF_HEAD), lambda m, f: (f // per, 0, f % per)),
            pl.BlockSpec((TF_HEAD, D_MODEL), lambda m, f: (f, 0)),
        ],
        out_shape=h_shapes + [
            jax.ShapeDtypeStruct((nf, D_MODEL, TF), BF16),
            jax.ShapeDtypeStruct((nf, D_MODEL, TF), BF16),
            jax.ShapeDtypeStruct((D_FF, D_MODEL), BF16),
        ],
        scratch_shapes=scratch,
        compiler_params=_cparams(("parallel", "arbitrary"), vmem),
        name="ffn_ln_head",
    )(h, w1, w3, w2, lng, lnb)
    h_parts, (w1b, w3b, w2b) = head[:n_h], head[n_h:]

    row = pl.BlockSpec((tm, D_MODEL), lambda m, f: (m + 1, 0))
    row_once = pl.BlockSpec((tm, D_MODEL), lambda m, f: (m + 1, 0), pipeline_mode=pl.Buffered(1))
    out = pl.pallas_call(
        functools.partial(_ffn_kernel, nf=nf, n_alias=n_h, emit_bf16=emit_bf16, convert=False),
        grid=(T_ALL // tm - 1, nf),
        in_specs=[
            row,
            pl.BlockSpec((None, D_MODEL, TF), lambda m, f: (f, 0, 0)),
            pl.BlockSpec((None, D_MODEL, TF), lambda m, f: (f, 0, 0)),
            pl.BlockSpec((TF, D_MODEL), lambda m, f: (f, 0)),
            ln_spec, ln_spec,
        ] + [pl.BlockSpec(memory_space=pl.ANY)] * n_h,
        out_specs=[row, row_once][:n_h],
        out_shape=h_shapes,
        scratch_shapes=scratch,
        input_output_aliases={6 + a: a for a in range(n_h)},
        compiler_params=_cparams(("parallel", "arbitrary"), vmem),
        name="ffn_ln",
    )(h, w1b, w3b, w2b, lng, lnb, *h_parts)
    return tuple(out) if emit_bf16 else out[0]


def _mm_kernel(x_ref, wt_ref, o_ref, wb_ref):
    @pl.when(pl.program_id(1) == 0)
    def _():
        wb_ref[...] = wt_ref[...].astype(BF16)

    o_ref[...] = lax.dot_general(x_ref[...], wb_ref[...], (((1,), (1,)), ((), ())),
                                 preferred_element_type=F32)


def _mm(xb, wt, n_out, tn):
    n, k = wt.shape
    assert n_out % tn == 0 and n_out <= n
    return pl.pallas_call(
        _mm_kernel,
        grid=(n_out // tn, T_ALL // TM_IN),
        in_specs=[
            pl.BlockSpec((TM_IN, k), lambda j, m: (m, 0)),
            pl.BlockSpec((tn, k), lambda j, m: (j, 0)),
        ],
        out_specs=pl.BlockSpec((TM_IN, tn), lambda j, m: (m, j)),
        out_shape=jax.ShapeDtypeStruct((T_ALL, n_out), F32),
        scratch_shapes=[pltpu.VMEM((tn, k), BF16)],
        compiler_params=_cparams(("parallel", "arbitrary"), VMEM_LIMIT_WIDE_BYTES),
        name="in_proj",
    )(xb, wt)


SC_TAIL0 = T_ALL - TM
SC_SAMP_OFF = T_MAIN - SC_TAIL0
assert SC_SAMP_OFF % DEC_SEQ == 0 and T_PROMPT - (SC_WIDTH - 1) >= SC_TAIL0


def _sc_mix_kernel(x_ref, wb_ref, wc_ref, wh_ref, cw_ref, o1_ref, o2_ref, v_ref, ut_ref,
                   wbb_ref, wcb_ref, whb_ref, ubuf_ref, *, nm):
    m = pl.program_id(1)
    tn = ubuf_ref.shape[1]

    @pl.when(m == 0)
    def _():
        wbb_ref[...] = wb_ref[...].astype(BF16)
        wcb_ref[...] = wc_ref[...].astype(BF16)
        whb_ref[...] = wh_ref[...].astype(BF16)
        ubuf_ref[0:CONV_HEAD, :] = jnp.zeros((CONV_HEAD, tn), F32)

    cw = cw_ref[...]
    ov_refs = (o1_ref, o2_ref)
    rs = TM // SC_ROW_SPLIT

    def slab(s, last_tile):
        r0 = s * rs
        xb = x_ref[r0:r0 + rs, :]
        bg = jnp.dot(xb, wbb_ref[...], preferred_element_type=F32)
        c = jnp.dot(xb, wcb_ref[...], preferred_element_type=F32)
        hh = jnp.dot(xb, whb_ref[...], preferred_element_type=F32)
        u = c * hh
        ubuf_ref[CONV_HEAD + r0:CONV_HEAD + r0 + rs, :] = u
        acc = cw[SC_WIDTH - 1:SC_WIDTH, :] * u
        if last_tile:
            lrow = lax.broadcasted_iota(jnp.int32, (rs, 1), 0) + r0
            spos = jnp.bitwise_and(lrow, DEC_SEQ - 1)
        for d in range(1, SC_WIDTH):
            ud = ubuf_ref[CONV_HEAD + r0 - d:CONV_HEAD + r0 - d + rs, :]
            if last_tile:
                first = jnp.logical_and(lrow >= SC_SAMP_OFF, spos < d)
                ud = jnp.where(first, ov_refs[d - 1][r0:r0 + rs, :], ud)
            acc = acc + cw[SC_WIDTH - 1 - d:SC_WIDTH - d, :] * ud
        v_ref[r0:r0 + rs, :] = (bg * acc).astype(BF16)
        if last_tile:
            ut_ref[r0:r0 + rs, :] = u

    @pl.when(m < nm - 1)
    def _():
        for s in range(SC_ROW_SPLIT):
            slab(s, False)

    @pl.when(m == nm - 1)
    def _():
        for s in range(SC_ROW_SPLIT):
            slab(s, True)

    ubuf_ref[0:CONV_HEAD, :] = ubuf_ref[TM:TM + CONV_HEAD, :]


def _sc_mix(xb, w_in, j, w_conv, conv_prev):
    tn = 512
    nb = D_MODEL // tn
    nm = T_ALL // TM
    ovs = [jnp.pad(o, ((SC_SAMP_OFF, 0), (0, 0))) for o in _conv_overrides(conv_prev, SC_WIDTH)]
    tail = pl.BlockSpec((TM, tn), lambda n, m: (0, n))
    wscratch = pltpu.VMEM((D_MODEL, tn), BF16)
    return pl.pallas_call(
        functools.partial(_sc_mix_kernel, nm=nm),
        grid=(nb, nm),
        in_specs=[
            pl.BlockSpec((TM, D_MODEL), lambda n, m: (m, 0)),
            pl.BlockSpec((None, D_MODEL, tn), lambda n, m: (j, 0, n)),
            pl.BlockSpec((None, D_MODEL, tn), lambda n, m: (j, 0, n + nb)),
            pl.BlockSpec((None, D_MODEL, tn), lambda n, m: (j, 0, n + 2 * nb)),
            pl.BlockSpec((SC_WIDTH, tn), lambda n, m: (0, n)),
            tail, tail,
        ],
        out_specs=[pl.BlockSpec((TM, tn), lambda n, m: (m, n)), tail],
        out_shape=[jax.ShapeDtypeStruct((T_ALL, D_MODEL), BF16),
                   jax.ShapeDtypeStruct((TM, D_MODEL), F32)],
        scratch_shapes=[wscratch, wscratch, wscratch, pltpu.VMEM((CONV_HEAD + TM, tn), F32)],
        compiler_params=_cparams(("parallel", "arbitrary")),
        name="sc_mix",
    )(xb, w_in, w_in, w_in, w_conv.astype(F32), *ovs)


def _conv_overrides(prev, taps):
    c = prev.shape[-1]
    out = []
    for d in range(1, taps):
        o = jnp.zeros((DEC_BATCH, DEC_SEQ, c), F32)
        for p in range(d):
            o = o.at[:, p].set(prev[:, taps - 1 + p - d])
        out.append(o.reshape(N_SAMP, c))
    return out


def _proj_ln_kernel(y_ref, w_ref, h_ref, g_ref, b_ref, o_ref, *, nk):
    rs = FFN_SLAB

    def slab_dot(rows):
        return jnp.dot(y_ref[rows, :], w_ref[...], preferred_element_type=F32)

    def finish(rows, acc):
        v = ALPHA * h_ref[rows, :] + acc
        o_ref[rows, :] = _layer_norm(v, g_ref[...], b_ref[...])

    slabs = [pl.ds(s * rs, rs) for s in range(y_ref.shape[0] // rs)]
    if nk == 1:
        for rows in slabs:
            finish(rows, slab_dot(rows))
        return

    k = pl.program_id(1)

    @pl.when(k == 0)
    def _():
        for rows in slabs:
            o_ref[rows, :] = slab_dot(rows)

    @pl.when(k == 1)
    def _():
        for rows in slabs:
            finish(rows, o_ref[rows, :] + slab_dot(rows))


def _proj_ln(y, w, j, h, lng, lnb, i, k):
    kdim = w.shape[1]
    tk = 2048
    nk = kdim // tk
    assert nk in (1, 2)
    return pl.pallas_call(
        functools.partial(_proj_ln_kernel, nk=nk),
        grid=(T_ALL // TM, nk),
        in_specs=[
            pl.BlockSpec((TM, tk), lambda m, kk: (m, kk)),
            pl.BlockSpec((None, tk, D_MODEL), lambda m, kk: (j, kk, 0)),
            pl.BlockSpec((TM, D_MODEL), lambda m, kk: (m, 0)),
            pl.BlockSpec((None, None, 1, D_MODEL), lambda m, kk: (i, k, 0, 0)),
            pl.BlockSpec((None, None, 1, D_MODEL), lambda m, kk: (i, k, 0, 0)),
        ],
        out_specs=pl.BlockSpec((TM, D_MODEL), lambda m, kk: (m, 0)),
        out_shape=jax.ShapeDtypeStruct((T_ALL, D_MODEL), F32),
        compiler_params=_cparams(("parallel", "arbitrary")),
        name="out_proj_ln",
    )(y, w, h, lng, lnb)


_NT = (((1,), (1,)), ((), ()))
_TN = (((0,), (0,)), ((), ()))
SPLIT_K = 2 * LANES


def _bf16_pieces(v):
    hi = v.astype(BF16)
    r1 = v - hi.astype(F32)
    mid = r1.astype(BF16)
    lo = (r1 - mid.astype(F32)).astype(BF16)
    return hi, mid, lo


def _split_pack(v, lane_lo):
    hi, mid, lo = _bf16_pieces(v)
    a = jnp.where(lane_lo, hi.astype(F32), pltpu.roll(mid.astype(F32), HEADS, axis=1))
    b = jnp.where(lane_lo, lo.astype(F32), 0.0)
    return jnp.concatenate([a, b], axis=1).astype(BF16)


def _split_pack2(v, lane_lo):
    hi = v.astype(BF16)
    mid = (v - hi.astype(F32)).astype(BF16)
    return jnp.where(lane_lo, hi.astype(F32), pltpu.roll(mid.astype(F32), HEADS, axis=1)).astype(BF16)


def _masked_cumsum(mask_bf, da):
    hi, mid, lo = _bf16_pieces(da)
    p = jnp.dot(mask_bf, jnp.concatenate([hi, mid, lo], axis=1), preferred_element_type=F32)
    return p[:, :LANES] + p[:, LANES:2 * LANES] + p[:, 2 * LANES:]


def _conv_silu(buf_ref, base, n, cw, cbias):
    taps = SSD_CONV_WIDTH
    acc = cw[taps - 1:taps, :] * buf_ref[base:base + n, :]
    for d in range(1, taps):
        acc = acc + cw[taps - 1 - d:taps - d, :] * buf_ref[base - d:base - d + n, :]
    return _silu(acc + cbias)


def _ssd_prep_kernel(x_ref, wdt_ref, dtb_ref, al_ref, pk_acs_ref, pk3_ref, acst_ref, tott_ref):
    li, si = _chunk_iotas()
    lane_lo = si < HEADS
    same_seq = (li // DEC_SEQ) == (si // DEC_SEQ)
    a_row = -jnp.exp(al_ref[...])
    wdt = wdt_ref[...].astype(BF16)
    wdt = jnp.concatenate([wdt, jnp.zeros_like(wdt)], axis=0)
    dt_raw = lax.dot_general(x_ref[...], wdt, _NT, preferred_element_type=F32)
    for k in range(PREP_NB):
        c = pl.program_id(0) * PREP_NB + k
        rows = pl.ds(k * CHUNK, CHUNK)
        same = jnp.logical_or(c < N_MAIN_CHUNKS, same_seq)
        causal = jnp.logical_and(si <= li, same)
        row = lax.broadcasted_iota(jnp.int32, (CHUNK, 1), 0) + c * CHUNK
        valid = jnp.logical_or(row < T_PROMPT, row >= T_MAIN)
        dt = jnp.where(jnp.logical_and(valid, lane_lo),
                       _softplus(dt_raw[k * CHUNK:(k + 1) * CHUNK, :] + dtb_ref[...]), 0.0)
        da = dt * a_row
        acs = _masked_cumsum(causal.astype(F32).astype(BF16), da)
        tot = _masked_cumsum(same.astype(F32).astype(BF16), da)
        pk_acs_ref[rows, :] = _split_pack(acs, lane_lo)
        pk3_ref[rows, :] = jnp.concatenate([_split_pack2(dt, lane_lo), _split_pack2(jnp.exp(tot - acs), lane_lo),
                                            _split_pack2(jnp.exp(acs), lane_lo)], axis=1)
        acst_ref[k] = acs.T
        tott_ref[k] = tot.T


def _ssd_prep(xb, wt, dt_bias, a_log):
    n_chunks = T_ALL // CHUNK
    rows = PREP_NB * CHUNK
    tr = pl.BlockSpec((PREP_NB, LANES, CHUNK), lambda c: (c, 0, 0))
    return pl.pallas_call(
        _ssd_prep_kernel,
        grid=(n_chunks // PREP_NB,),
        in_specs=[
            pl.BlockSpec((rows, D_MODEL), lambda c: (c, 0)),
            pl.BlockSpec((HEADS, D_MODEL), lambda c: (DT_COL // HEADS, 0)),
            pl.BlockSpec((1, LANES), lambda c: (0, 0)),
            pl.BlockSpec((1, LANES), lambda c: (0, 0)),
        ],
        out_specs=[
            pl.BlockSpec((rows, SPLIT_K), lambda c: (c, 0)),
            pl.BlockSpec((rows, 3 * LANES), lambda c: (c, 0)),
            tr, tr,
        ],
        out_shape=[
            jax.ShapeDtypeStruct((T_ALL, SPLIT_K), BF16),
            jax.ShapeDtypeStruct((T_ALL, 3 * LANES), BF16),
            jax.ShapeDtypeStruct((n_chunks, LANES, CHUNK), F32),
            jax.ShapeDtypeStruct((n_chunks, LANES, CHUNK), F32),
        ],
        compiler_params=_cparams(("parallel",)),
        name="ssd_prep",
    )(xb, wt, dt_bias, a_log)


def _intra_chunk(xbc, pk_acs, pk3, arow, causal, lane_lo, r_ref, e_ref):
    xs = xbc[:, :GROUP_W]
    bmb = xbc[:, GROUP_W:GROUP_W + STATE].astype(BF16)
    cmb = xbc[:, GROUP_W + STATE:].astype(BF16)
    colmat = jnp.dot(pk_acs, r_ref[...], preferred_element_type=F32)
    packed = jnp.concatenate([pk3[:, :LANES], pk3[:, LANES:2 * LANES], pk3[:, 2 * LANES:]], axis=0)
    ex = jnp.dot(packed, e_ref[...], preferred_element_type=F32)
    dtx, tex, eax = ex[:CHUNK], ex[CHUNK:2 * CHUNK], ex[2 * CHUNK:]
    xdt = xs * dtx
    cb = lax.dot_general(cmb, bmb, _NT, preferred_element_type=F32)
    ys = []
    for q in range(HEADS_PER_GROUP // 2):
        scs = []
        for r in (2 * q, 2 * q + 1):
            seg = colmat[:, r * CHUNK:(r + 1) * CHUNK] - arow[r:r + 1, :]
            scs.append((cb * jnp.exp(jnp.where(causal, seg, -jnp.inf))).astype(BF16))
        xp = xdt[:, q * LANES:(q + 1) * LANES]
        rhs = jnp.concatenate([jnp.where(lane_lo, xp, 0.0).astype(BF16),
                               jnp.where(lane_lo, 0.0, xp).astype(BF16)], axis=0)
        ys.append(jnp.dot(jnp.concatenate(scs, axis=1), rhs, preferred_element_type=F32))
    y_intra = jnp.concatenate(ys, axis=1)
    xw = (xdt * tex).astype(BF16)
    return xs, bmb, cmb, y_intra, eax, xw


def _gate_norm(y, z, nw):
    y = y * _silu(z)
    ms = jnp.mean(y * y, axis=-1, keepdims=True)
    return (y * lax.rsqrt(ms + RMS_EPS) * nw).astype(BF16)


def _chunk_iotas():
    li = lax.broadcasted_iota(jnp.int32, (CHUNK, CHUNK), 0)
    si = lax.broadcasted_iota(jnp.int32, (CHUNK, CHUNK), 1)
    return li, si


def _ssd_main_kernel(z_ref, xr_ref, br_ref, cr_ref, pk_acs_ref, pk3_ref, acst_ref, tott_ref, cw_ref, cbias_ref,
                     dx_ref, nw_ref, r_ref, e_ref, y_ref, sfin_ref, buf_ref, st_ref):
    del tott_ref
    g = pl.program_id(0)
    rb = pl.program_id(1)

    @pl.when(rb == 0)
    def _():
        st_ref[...] = jnp.zeros_like(st_ref)
        buf_ref[0:CONV_HEAD, :] = jnp.zeros((CONV_HEAD, XBC_W), F32)

    @pl.when(rb > 0)
    def _():
        buf_ref[0:CONV_HEAD, :] = buf_ref[SSD_RB:SSD_RB + CONV_HEAD, :]

    buf_ref[CONV_HEAD:, 0:GROUP_W] = xr_ref[...]
    buf_ref[CONV_HEAD:, GROUP_W:GROUP_W + STATE] = br_ref[...]
    buf_ref[CONV_HEAD:, GROUP_W + STATE:] = cr_ref[...]

    li, si = _chunk_iotas()
    causal = si <= li
    lane_lo = si < HEADS
    g8 = pl.multiple_of(g * HEADS_PER_GROUP, SUBLANES)
    cw = cw_ref[...]
    cbias = cbias_ref[...]

    for k in range(SSD_NB):
        rows = pl.ds(k * CHUNK, CHUNK)
        xbc = _conv_silu(buf_ref, CONV_HEAD + k * CHUNK, CHUNK, cw, cbias)
        arow = acst_ref[k, pl.ds(g8, HEADS_PER_GROUP), :]
        xs, bmb, cmb, y_intra, eax, xw = _intra_chunk(xbc, pk_acs_ref[rows, :], pk3_ref[rows, :], arow,
                                                      causal, lane_lo, r_ref, e_ref)
        st = st_ref[...]
        y = y_intra + jnp.dot(cmb, st.astype(BF16), preferred_element_type=F32) * eax + dx_ref[...] * xs
        y_ref[rows, :] = _gate_norm(y, z_ref[rows, :], nw_ref[...])
        st_ref[...] = st * eax[CHUNK - 1:CHUNK, :] + lax.dot_general(bmb, xw, _TN, preferred_element_type=F32)

    @pl.when(rb == T_MAIN // SSD_RB - 1)
    def _():
        sfin_ref[...] = st_ref[...]


def _ssd_samp_kernel(z_ref, xr_ref, br_ref, cr_ref, pk_acs_ref, pk3_ref, acst_ref, tott_ref, cw_ref, cbias_ref,
                     dx_ref, nw_ref, r_ref, e_ref, prev_ref, s0_ref, yprev_ref, y_ref, s1_ref,
                     buf_ref, xbc_ref, yint_ref):
    del yprev_ref
    g = pl.program_id(0)
    slot = CONV_HEAD + DEC_SEQ
    cw = cw_ref[...]
    cbias = cbias_ref[...]
    for q in range(SEQ_PER_CHUNK):
        r0 = q * DEC_SEQ
        top = q * slot + CONV_HEAD
        buf_ref[top - (SSD_CONV_WIDTH - 1):top, :] = prev_ref[q]
        buf_ref[top:top + DEC_SEQ, 0:GROUP_W] = xr_ref[r0:r0 + DEC_SEQ, :]
        buf_ref[top:top + DEC_SEQ, GROUP_W:GROUP_W + STATE] = br_ref[r0:r0 + DEC_SEQ, :]
        buf_ref[top:top + DEC_SEQ, GROUP_W + STATE:] = cr_ref[r0:r0 + DEC_SEQ, :]
    for q in range(SEQ_PER_CHUNK):
        r0 = q * DEC_SEQ
        xbc_ref[r0:r0 + DEC_SEQ, :] = _conv_silu(buf_ref, q * slot + CONV_HEAD, DEC_SEQ, cw, cbias)

    li, si = _chunk_iotas()
    same = (li // DEC_SEQ) == (si // DEC_SEQ)
    causal = jnp.logical_and(si <= li, same)
    lane_lo = si < HEADS
    g8 = pl.multiple_of(g * HEADS_PER_GROUP, SUBLANES)

    arow = acst_ref[0, pl.ds(g8, HEADS_PER_GROUP), :]
    trow = tott_ref[0, pl.ds(g8, HEADS_PER_GROUP), :]
    xs, bmb, cmb, y_intra, eax, xw = _intra_chunk(xbc_ref[...], pk_acs_ref[...], pk3_ref[...], arow,
                                                  causal, lane_lo, r_ref, e_ref)
    for q in range(SEQ_PER_CHUNK):
        r0 = q * DEC_SEQ
        yint_ref[r0:r0 + DEC_SEQ, :] = lax.dot_general(cmb[r0:r0 + DEC_SEQ, :], s0_ref[q].astype(BF16), _NT,
                                                       preferred_element_type=F32)
    y = y_intra + yint_ref[...] * eax + dx_ref[...] * xs
    y_ref[...] = _gate_norm(y, z_ref[...], nw_ref[...])
    for q in range(SEQ_PER_CHUNK):
        r0 = q * DEC_SEQ
        upd = lax.dot_general(xw[r0:r0 + DEC_SEQ, :], bmb[r0:r0 + DEC_SEQ, :], _TN, preferred_element_type=F32)
        for r in range(HEADS_PER_GROUP):
            lo, hi = r * HEADDIM, (r + 1) * HEADDIM
            s1_ref[q, lo:hi, :] = s0_ref[q, lo:hi, :] * jnp.exp(trow[r:r + 1, r0:r0 + 1]) + upd[lo:hi, :]


def _ssd_specs(rows, row0):
    xcol = D_INNER // GROUP_W
    bcol = (2 * D_INNER) // STATE
    return [
        pl.BlockSpec((rows, GROUP_W), lambda g, c: (row0 + c, g)),
        pl.BlockSpec((rows, GROUP_W), lambda g, c: (row0 + c, xcol + g)),
        pl.BlockSpec((rows, STATE), lambda g, c: (row0 + c, bcol + g)),
        pl.BlockSpec((rows, STATE), lambda g, c: (row0 + c, bcol + GROUPS + g)),
        pl.BlockSpec((rows, SPLIT_K), lambda g, c: (row0 + c, 0)),
        pl.BlockSpec((rows, 3 * LANES), lambda g, c: (row0 + c, 0)),
        pl.BlockSpec((rows // CHUNK, LANES, CHUNK), lambda g, c: (row0 + c, 0, 0)),
        pl.BlockSpec((rows // CHUNK, LANES, CHUNK), lambda g, c: (row0 + c, 0, 0)),
        pl.BlockSpec((None, SSD_CONV_WIDTH, XBC_W), lambda g, c: (g, 0, 0)),
        pl.BlockSpec((None, 1, XBC_W), lambda g, c: (g, 0, 0)),
        pl.BlockSpec((1, GROUP_W), lambda g, c: (0, g)),
        pl.BlockSpec((1, GROUP_W), lambda g, c: (0, g)),
        pl.BlockSpec((None, SPLIT_K, HEADS_PER_GROUP * CHUNK), lambda g, c: (g, 0, 0)),
        pl.BlockSpec((None, LANES, GROUP_W), lambda g, c: (g, 0, 0)),
    ]


def _spread_matrices():
    k = np.arange(SPLIT_K)
    head = k % HEADS
    used = (k // HEADS) < 3
    g = np.arange(GROUPS)[:, None, None]
    rj = np.arange(HEADS_PER_GROUP * CHUNK)[None, None, :] // CHUNK
    ej = np.arange(GROUP_W)[None, None, :] // HEADDIM
    hk = head[None, :, None]
    uk = used[None, :, None]
    r = (uk & (hk == g * HEADS_PER_GROUP + rj)).astype(np.float32)
    e = (hk == g * HEADS_PER_GROUP + ej)[:, :LANES].astype(np.float32)
    return jnp.asarray(r, BF16), jnp.asarray(e, BF16)


def _per_group(a):
    lead = a.shape[:-1]
    x = a[..., :D_INNER].reshape(*lead, GROUPS, GROUP_W)
    b = a[..., D_INNER:D_INNER + GROUPS * STATE].reshape(*lead, GROUPS, STATE)
    c = a[..., D_INNER + GROUPS * STATE:].reshape(*lead, GROUPS, STATE)
    return jnp.moveaxis(jnp.concatenate([x, b, c], axis=-1), -2, 0)


def _ssd(zxd, xb, wt, conv_w, conv_b, dt_bias, a_log, d_skip, norm_w, conv_prev, state0):
    r_mat, e_mat = _spread_matrices()
    pad = ((0, 0), (0, LANES - HEADS))
    prep = _ssd_prep(xb, wt, jnp.pad(dt_bias.reshape(1, HEADS).astype(F32), pad),
                     jnp.pad(a_log.reshape(1, HEADS).astype(F32), pad))
    params = (_per_group(conv_w), _per_group(conv_b.reshape(1, CONV_DIM)),
              jnp.repeat(d_skip.astype(F32), HEADDIM).reshape(1, D_INNER),
              norm_w.reshape(1, D_INNER).astype(F32), r_mat, e_mat)
    common = (zxd,) * 4 + tuple(prep) + params
    y, s_fin = pl.pallas_call(
        _ssd_main_kernel,
        grid=(GROUPS, T_MAIN // SSD_RB),
        in_specs=_ssd_specs(SSD_RB, 0),
        out_specs=[
            pl.BlockSpec((SSD_RB, GROUP_W), lambda g, c: (c, g)),
            pl.BlockSpec((None, STATE, GROUP_W), lambda g, c: (g, 0, 0)),
        ],
        out_shape=[
            jax.ShapeDtypeStruct((T_ALL, D_INNER), BF16),
            jax.ShapeDtypeStruct((GROUPS, STATE, GROUP_W), F32),
        ],
        scratch_shapes=[
            pltpu.VMEM((CONV_HEAD + SSD_RB, XBC_W), F32),
            pltpu.VMEM((STATE, GROUP_W), F32),
        ],
        compiler_params=_cparams(("parallel", "arbitrary")),
        name="ssd_prompt",
    )(*common)

    n_in = len(common)
    y, s_new = pl.pallas_call(
        _ssd_samp_kernel,
        grid=(GROUPS, N_SAMP_CHUNKS),
        in_specs=_ssd_specs(CHUNK, N_MAIN_CHUNKS) + [
            pl.BlockSpec((None, SEQ_PER_CHUNK, SSD_CONV_WIDTH - 1, XBC_W), lambda g, c: (g, c, 0, 0)),
            pl.BlockSpec((SEQ_PER_CHUNK, None, GROUP_W, STATE), lambda g, c: (c, g, 0, 0)),
            pl.BlockSpec(memory_space=pl.ANY),
        ],
        out_specs=[
            pl.BlockSpec((CHUNK, GROUP_W), lambda g, c: (N_MAIN_CHUNKS + c, g)),
            pl.BlockSpec((SEQ_PER_CHUNK, None, GROUP_W, STATE), lambda g, c: (c, g, 0, 0)),
        ],
        out_shape=[
            jax.ShapeDtypeStruct((T_ALL, D_INNER), BF16),
            jax.ShapeDtypeStruct((DEC_BATCH, GROUPS, GROUP_W, STATE), F32),
        ],
        scratch_shapes=[
            pltpu.VMEM((SEQ_PER_CHUNK * (CONV_HEAD + DEC_SEQ), XBC_W), F32),
            pltpu.VMEM((CHUNK, XBC_W), F32),
            pltpu.VMEM((CHUNK, GROUP_W), F32),
        ],
        input_output_aliases={n_in + 2: 0},
        compiler_params=_cparams(("parallel", "arbitrary")),
        name="ssd_sample",
    )(*common, _per_group(conv_prev), state0, y)
    return y, s_fin, s_new


def _tail_rows(a, n):
    c = a.shape[-1]
    p = a[T_PROMPT - n:T_PROMPT].reshape(1, 1, n, c)
    s = a[T_MAIN:].reshape(DEC_BATCH, DEC_SEQ, c)[:, DEC_SEQ - n:].reshape(1, DEC_BATCH, n, c)
    return p, s


def kernel(x_prompt, x_sample, cache_sc_conv, state_ssd_conv, state_ssd, meta_tokens, ln_g, ln_b,
           ffn_w1, ffn_w3, ffn_w2, sc_w_in, sc_w_conv, sc_w_out,
           ssd_w_in, ssd_w_conv, ssd_b_conv, ssd_dt_bias, ssd_a_log, ssd_d, ssd_norm_w, ssd_w_out):
    w1, w3, w2 = ffn_w1, ffn_w3, ffn_w2
    sc_out_w = sc_w_out.astype(BF16)
    ssd_out_w = ssd_w_out.astype(BF16)
    lng = ln_g.reshape(DEPTH, 3, 1, D_MODEL)
    lnb = ln_b.reshape(DEPTH, 3, 1, D_MODEL)

    h = jnp.concatenate([meta_tokens.astype(F32), x_prompt[0], jnp.zeros((N_PAD, D_MODEL), F32),
                         x_sample.reshape(N_SAMP, D_MODEL)], axis=0)

    h, hb = _ffn(h, w1, w3, w2, lng, lnb, 0, 0, 0, emit_bf16=True)
    v, u_tail = _sc_mix(hb, sc_w_in, 0, sc_w_conv[0], cache_sc_conv[0])
    keep = SC_WIDTH - 1
    new_sc_p = u_tail[T_PROMPT - keep - SC_TAIL0:T_PROMPT - SC_TAIL0].reshape(1, 1, keep, D_MODEL)
    new_sc_s = u_tail[SC_SAMP_OFF:].reshape(DEC_BATCH, DEC_SEQ, D_MODEL)[:, DEC_SEQ - keep:].reshape(
        1, DEC_BATCH, keep, D_MODEL)
    h = _proj_ln(v, sc_out_w, 0, h, lng, lnb, 0, 1)
    h = _ffn(h, w1, w3, w2, lng, lnb, 0, 1, 2)

    h, hb = _ffn(h, w1, w3, w2, lng, lnb, 1, 0, 0, emit_bf16=True)
    ssd_wt = jnp.swapaxes(ssd_w_in[0], 0, 1)
    zxd = _mm(hb, ssd_wt, ZXD_W, ZXD_TN)
    new_conv_p, new_conv_s = _tail_rows(zxd[:, D_INNER:DT_COL], SSD_CONV_WIDTH - 1)
    state0 = state_ssd[0].reshape(DEC_BATCH, GROUPS, GROUP_W, STATE)
    y, s_fin, s_new = _ssd(zxd, hb, ssd_wt, ssd_w_conv[0], ssd_b_conv[0], ssd_dt_bias[0], ssd_a_log[0],
                           ssd_d[0], ssd_norm_w[0], state_ssd_conv[0], state0)
    h = _proj_ln(y, ssd_out_w, 0, h, lng, lnb, 1, 1)
    h = _ffn(h, w1, w3, w2, lng, lnb, 1, 1, 2)

    y_prompt = h[N_META:T_PROMPT].reshape(1, SEQ, D_MODEL)
    y_sample = h[T_MAIN:].reshape(DEC_BATCH, DEC_SEQ, D_MODEL)
    new_ssd_p = jnp.swapaxes(s_fin, 1, 2).reshape(1, 1, HEADS, HEADDIM, STATE).astype(state_ssd.dtype)
    new_ssd_s = s_new.reshape(1, DEC_BATCH, HEADS, HEADDIM, STATE).astype(state_ssd.dtype)
    return (y_prompt, y_sample, new_sc_p, new_sc_s, new_conv_p, new_conv_s, new_ssd_p, new_ssd_s)
```

```python
import functools

import numpy as np

import jax
import jax.numpy as jnp
from jax import lax
from jax.experimental import pallas as pl
from jax.experimental.pallas import tpu as pltpu

F32 = jnp.float32
BF16 = jnp.bfloat16

D_MODEL = 2048
SEQ = 8192
DEPTH = 2
DEC_BATCH = 32
DEC_SEQ = 16
N_META = 16
D_FF = 5632
SC_WIDTH = 3
D_INNER = 4096
HEADDIM = 64
HEADS = 64
GROUPS = 8
HEADS_PER_GROUP = 8
STATE = 128
SSD_CONV_WIDTH = 4
CONV_DIM = D_INNER + 2 * GROUPS * STATE
GROUP_W = HEADS_PER_GROUP * HEADDIM
XBC_W = GROUP_W + 2 * STATE
ALPHA = (2.0 * DEPTH) ** 0.25
LN_EPS = 1e-5
RMS_EPS = 1e-5

CHUNK = 128
T_PROMPT = N_META + SEQ
N_MAIN_CHUNKS = -(-T_PROMPT // CHUNK)
T_MAIN = N_MAIN_CHUNKS * CHUNK
N_PAD = T_MAIN - T_PROMPT
N_SAMP = DEC_BATCH * DEC_SEQ
N_SAMP_CHUNKS = N_SAMP // CHUNK
SEQ_PER_CHUNK = CHUNK // DEC_SEQ
T_ALL = T_MAIN + N_SAMP

TM = 736
TM_IN = 2 * TM
TF = 512
TF_HEAD = 256
TM_WIDE = 1104
FFN_SLAB = 368
SC_ROW_SPLIT = 2
DT_COL = D_INNER + CONV_DIM
ZXD_W = DT_COL
ZXD_TN = 1280
SSD_NB = 13
SSD_RB = SSD_NB * CHUNK
PREP_NB = 3
LANES = 128
SUBLANES = 8
CONV_HEAD = SUBLANES

VMEM_LIMIT_BYTES = 56 * 1024 * 1024
VMEM_LIMIT_WIDE_BYTES = 61 * 1024 * 1024


def _cparams(sem, vmem_limit_bytes=VMEM_LIMIT_BYTES):
    return pltpu.CompilerParams(dimension_semantics=sem, vmem_limit_bytes=vmem_limit_bytes)


def _layer_norm(v, g, b):
    mu = jnp.mean(v, axis=-1, keepdims=True)
    c = v - mu
    var = jnp.mean(c * c, axis=-1, keepdims=True)
    return c * lax.rsqrt(var + LN_EPS) * g + b


def _silu(x):
    h = 0.5 * x
    return h + h * jnp.tanh(h)


def _softplus(x):
    return jnp.maximum(x, 0.0) + jnp.log(1.0 + jnp.exp(-jnp.abs(x)))


def _ffn_kernel(*refs, nf, n_alias, emit_bf16, convert):
    x_ref, w1_ref, w3_ref, w2_ref, g_ref, b_ref = refs[:6]
    outs = refs[6 + n_alias:]
    o_ref = outs[0]
    pos = 1
    ob_ref = None
    if emit_bf16:
        ob_ref = outs[pos]
        pos += 1
    if convert:
        wb_refs = outs[pos:pos + 3]
        for src, dst in zip((w1_ref, w3_ref, w2_ref), wb_refs):
            dst[...] = src[...].astype(BF16)
        w1_ref, w3_ref, w2_ref = wb_refs
        pos += 3
    xb_ref = outs[pos]
    f = pl.program_id(1)

    @pl.when(f == 0)
    def _():
        xb_ref[...] = x_ref[...].astype(BF16)
        o_ref[...] = jnp.zeros_like(o_ref)

    rs = FFN_SLAB
    n_slabs = x_ref.shape[0] // rs

    def slab(s, last):
        rows = pl.ds(s * rs, rs)
        xb = xb_ref[rows, :]
        h1 = jnp.dot(xb, w1_ref[...], preferred_element_type=F32)
        h3 = jnp.dot(xb, w3_ref[...], preferred_element_type=F32)
        gate = (_silu(h1) * h3).astype(BF16)
        acc = o_ref[rows, :] + jnp.dot(gate, w2_ref[...], preferred_element_type=F32)
        if last:
            res = _layer_norm(ALPHA * x_ref[rows, :] + 0.5 * acc, g_ref[...], b_ref[...])
            o_ref[rows, :] = res
            if ob_ref is not None:
                ob_ref[rows, :] = res.astype(BF16)
        else:
            o_ref[rows, :] = acc

    @pl.when(f < nf - 1)
    def _():
        for s in range(n_slabs):
            slab(s, False)

    @pl.when(f == nf - 1)
    def _():
        for s in range(n_slabs):
            slab(s, True)


def _ffn(h, w1, w3, w2, lng, lnb, i, j, k, emit_bf16=False):
    tm, vmem = (TM, VMEM_LIMIT_BYTES) if emit_bf16 else (TM_WIDE, VMEM_LIMIT_WIDE_BYTES)
    ln_spec = pl.BlockSpec((None, None, 1, D_MODEL), lambda m, f: (i, k, 0, 0))
    h_shapes = [jax.ShapeDtypeStruct((T_ALL, D_MODEL), F32)]
    if emit_bf16:
        h_shapes.append(jax.ShapeDtypeStruct((T_ALL, D_MODEL), BF16))
    n_h = len(h_shapes)
    scratch = [pltpu.VMEM((tm, D_MODEL), BF16)]

    head_row = pl.BlockSpec((tm, D_MODEL), lambda m, f: (0, 0))
    nf = D_FF // TF
    tf_head = TF if emit_bf16 else TF_HEAD
    nf_head = D_FF // tf_head
    per = TF // tf_head
    head = pl.pallas_call(
        functools.partial(_ffn_kernel, nf=nf_head, n_alias=0, emit_bf16=emit_bf16, convert=True),
        grid=(1, nf_head),
        in_specs=[
            head_row,
            pl.BlockSpec((None, None, D_MODEL, tf_head), lambda m, f: (i, j, 0, f)),
            pl.BlockSpec((None, None, D_MODEL, tf_head), lambda m, f: (i, j, 0, f)),
            pl.BlockSpec((None, None, tf_head, D_MODEL), lambda m, f: (i, j, f, 0)),
            ln_spec, ln_spec,
        ],
        out_specs=[head_row] * n_h + [
            pl.BlockSpec((None, D_MODEL, tf_head), lambda m, f: (f // per, 0, f % per)),
            pl.BlockSpec((None, D_MODEL, tf_head), lambda m, f: (f // per, 0, f % per)),
            pl.BlockSpec((tf_head, D_MODEL), lambda m, f: (f, 0)),
        ],
        out_shape=h_shapes + [
            jax.ShapeDtypeStruct((nf, D_MODEL, TF), BF16),
            jax.ShapeDtypeStruct((nf, D_MODEL, TF), BF16),
            jax.ShapeDtypeStruct((D_FF, D_MODEL), BF16),
        ],
        scratch_shapes=scratch,
        compiler_params=_cparams(("parallel", "arbitrary"), VMEM_LIMIT_WIDE_BYTES),
        name="ffn_ln_head",
    )(h, w1, w3, w2, lng, lnb)
    h_parts, (w1b, w3b, w2b) = head[:n_h], head[n_h:]

    row = pl.BlockSpec((tm, D_MODEL), lambda m, f: (m + 1, 0))
    out = pl.pallas_call(
        functools.partial(_ffn_kernel, nf=nf, n_alias=n_h, emit_bf16=emit_bf16, convert=False),
        grid=(T_ALL // tm - 1, nf),
        in_specs=[
            row,
            pl.BlockSpec((None, D_MODEL, TF), lambda m, f: (f, 0, 0)),
            pl.BlockSpec((None, D_MODEL, TF), lambda m, f: (f, 0, 0)),
            pl.BlockSpec((TF, D_MODEL), lambda m, f: (f, 0)),
            ln_spec, ln_spec,
        ] + [pl.BlockSpec(memory_space=pl.ANY)] * n_h,
        out_specs=[row] * n_h,
        out_shape=h_shapes,
        scratch_shapes=scratch,
        input_output_aliases={6 + a: a for a in range(n_h)},
        compiler_params=_cparams(("parallel", "arbitrary"), vmem),
        name="ffn_ln",
    )(h, w1b, w3b, w2b, lng, lnb, *h_parts)
    return tuple(out) if emit_bf16 else out[0]


def _mm_kernel(x_ref, wt_ref, o_ref, wb_ref):
    @pl.when(pl.program_id(1) == 0)
    def _():
        wb_ref[...] = wt_ref[...].astype(BF16)

    o_ref[...] = lax.dot_general(x_ref[...], wb_ref[...], (((1,), (1,)), ((), ())),
                                 preferred_element_type=F32)


def _mm(xb, wt, n_out, tn):
    n, k = wt.shape
    assert n_out % tn == 0 and n_out <= n
    return pl.pallas_call(
        _mm_kernel,
        grid=(n_out // tn, T_ALL // TM_IN),
        in_specs=[
            pl.BlockSpec((TM_IN, k), lambda j, m: (m, 0)),
            pl.BlockSpec((tn, k), lambda j, m: (j, 0)),
        ],
        out_specs=pl.BlockSpec((TM_IN, tn), lambda j, m: (m, j)),
        out_shape=jax.ShapeDtypeStruct((T_ALL, n_out), F32),
        scratch_shapes=[pltpu.VMEM((tn, k), BF16)],
        compiler_params=_cparams(("parallel", "arbitrary"), VMEM_LIMIT_WIDE_BYTES),
        name="in_proj",
    )(xb, wt)


SC_TAIL0 = T_ALL - TM
SC_SAMP_OFF = T_MAIN - SC_TAIL0
assert SC_SAMP_OFF % DEC_SEQ == 0 and T_PROMPT - (SC_WIDTH - 1) >= SC_TAIL0


def _sc_mix_kernel(x_ref, wb_ref, wc_ref, wh_ref, cw_ref, o1_ref, o2_ref, v_ref, ut_ref,
                   wbb_ref, wcb_ref, whb_ref, ubuf_ref, *, nm):
    m = pl.program_id(1)
    tn = ubuf_ref.shape[1]

    @pl.when(m == 0)
    def _():
        wbb_ref[...] = wb_ref[...].astype(BF16)
        wcb_ref[...] = wc_ref[...].astype(BF16)
        whb_ref[...] = wh_ref[...].astype(BF16)
        ubuf_ref[0:CONV_HEAD, :] = jnp.zeros((CONV_HEAD, tn), F32)

    cw = cw_ref[...]
    ov_refs = (o1_ref, o2_ref)
    rs = TM // SC_ROW_SPLIT

    def slab(s, last_tile):
        r0 = s * rs
        xb = x_ref[r0:r0 + rs, :]
        bg = jnp.dot(xb, wbb_ref[...], preferred_element_type=F32)
        c = jnp.dot(xb, wcb_ref[...], preferred_element_type=F32)
        hh = jnp.dot(xb, whb_ref[...], preferred_element_type=F32)
        u = c * hh
        ubuf_ref[CONV_HEAD + r0:CONV_HEAD + r0 + rs, :] = u
        acc = cw[SC_WIDTH - 1:SC_WIDTH, :] * u
        if last_tile:
            lrow = lax.broadcasted_iota(jnp.int32, (rs, 1), 0) + r0
            spos = jnp.bitwise_and(lrow, DEC_SEQ - 1)
        for d in range(1, SC_WIDTH):
            ud = ubuf_ref[CONV_HEAD + r0 - d:CONV_HEAD + r0 - d + rs, :]
            if last_tile:
                first = jnp.logical_and(lrow >= SC_SAMP_OFF, spos < d)
                ud = jnp.where(first, ov_refs[d - 1][r0:r0 + rs, :], ud)
            acc = acc + cw[SC_WIDTH - 1 - d:SC_WIDTH - d, :] * ud
        v_ref[r0:r0 + rs, :] = (bg * acc).astype(BF16)
        if last_tile:
            ut_ref[r0:r0 + rs, :] = u

    @pl.when(m < nm - 1)
    def _():
        for s in range(SC_ROW_SPLIT):
            slab(s, False)

    @pl.when(m == nm - 1)
    def _():
        for s in range(SC_ROW_SPLIT):
            slab(s, True)

    ubuf_ref[0:CONV_HEAD, :] = ubuf_ref[TM:TM + CONV_HEAD, :]


def _sc_mix(xb, w_in, j, w_conv, conv_prev):
    tn = 512
    nb = D_MODEL // tn
    nm = T_ALL // TM
    ovs = [jnp.pad(o, ((SC_SAMP_OFF, 0), (0, 0))) for o in _conv_overrides(conv_prev, SC_WIDTH)]
    tail = pl.BlockSpec((TM, tn), lambda n, m: (0, n))
    wscratch = pltpu.VMEM((D_MODEL, tn), BF16)
    return pl.pallas_call(
        functools.partial(_sc_mix_kernel, nm=nm),
        grid=(nb, nm),
        in_specs=[
            pl.BlockSpec((TM, D_MODEL), lambda n, m: (m, 0)),
            pl.BlockSpec((None, D_MODEL, tn), lambda n, m: (j, 0, n)),
            pl.BlockSpec((None, D_MODEL, tn), lambda n, m: (j, 0, n + nb)),
            pl.BlockSpec((None, D_MODEL, tn), lambda n, m: (j, 0, n + 2 * nb)),
            pl.BlockSpec((SC_WIDTH, tn), lambda n, m: (0, n)),
            tail, tail,
        ],
        out_specs=[pl.BlockSpec((TM, tn), lambda n, m: (m, n)), tail],
        out_shape=[jax.ShapeDtypeStruct((T_ALL, D_MODEL), BF16),
                   jax.ShapeDtypeStruct((TM, D_MODEL), F32)],
        scratch_shapes=[wscratch, wscratch, wscratch, pltpu.VMEM((CONV_HEAD + TM, tn), F32)],
        compiler_params=_cparams(("parallel", "arbitrary")),
        name="sc_mix",
    )(xb, w_in, w_in, w_in, w_conv.astype(F32), *ovs)


def _conv_overrides(prev, taps):
    c = prev.shape[-1]
    out = []
    for d in range(1, taps):
        o = jnp.zeros((DEC_BATCH, DEC_SEQ, c), F32)
        for p in range(d):
            o = o.at[:, p].set(prev[:, taps - 1 + p - d])
        out.append(o.reshape(N_SAMP, c))
    return out


def _proj_ln_kernel(y_ref, w_ref, h_ref, g_ref, b_ref, o_ref, *, nk):
    rs = FFN_SLAB

    def slab_dot(rows):
        return jnp.dot(y_ref[rows, :], w_ref[...], preferred_element_type=F32)

    def finish(rows, acc):
        v = ALPHA * h_ref[rows, :] + acc
        o_ref[rows, :] = _layer_norm(v, g_ref[...], b_ref[...])

    slabs = [pl.ds(s * rs, rs) for s in range(y_ref.shape[0] // rs)]
    if nk == 1:
        for rows in slabs:
            finish(rows, slab_dot(rows))
        return

    k = pl.program_id(1)

    @pl.when(k == 0)
    def _():
        for rows in slabs:
            o_ref[rows, :] = slab_dot(rows)

    @pl.when(k == 1)
    def _():
        for rows in slabs:
            finish(rows, o_ref[rows, :] + slab_dot(rows))


def _proj_ln(y, w, j, h, lng, lnb, i, k):
    kdim = w.shape[1]
    tk = 2048
    nk = kdim // tk
    assert nk in (1, 2)
    return pl.pallas_call(
        functools.partial(_proj_ln_kernel, nk=nk),
        grid=(T_ALL // TM, nk),
        in_specs=[
            pl.BlockSpec((TM, tk), lambda m, kk: (m, kk)),
            pl.BlockSpec((None, tk, D_MODEL), lambda m, kk: (j, kk, 0)),
            pl.BlockSpec((TM, D_MODEL), lambda m, kk: (m, 0)),
            pl.BlockSpec((None, None, 1, D_MODEL), lambda m, kk: (i, k, 0, 0)),
            pl.BlockSpec((None, None, 1, D_MODEL), lambda m, kk: (i, k, 0, 0)),
        ],
        out_specs=pl.BlockSpec((TM, D_MODEL), lambda m, kk: (m, 0)),
        out_shape=jax.ShapeDtypeStruct((T_ALL, D_MODEL), F32),
        compiler_params=_cparams(("parallel", "arbitrary")),
        name="out_proj_ln",
    )(y, w, h, lng, lnb)


_NT = (((1,), (1,)), ((), ()))
_TN = (((0,), (0,)), ((), ()))
SPLIT_K = 2 * LANES


def _bf16_pieces(v):
    hi = v.astype(BF16)
    r1 = v - hi.astype(F32)
    mid = r1.astype(BF16)
    lo = (r1 - mid.astype(F32)).astype(BF16)
    return hi, mid, lo


def _split_pack(v, lane_lo):
    hi, mid, lo = _bf16_pieces(v)
    a = jnp.where(lane_lo, hi.astype(F32), pltpu.roll(mid.astype(F32), HEADS, axis=1))
    b = jnp.where(lane_lo, lo.astype(F32), 0.0)
    return jnp.concatenate([a, b], axis=1).astype(BF16)


def _split_pack2(v, lane_lo):
    hi = v.astype(BF16)
    mid = (v - hi.astype(F32)).astype(BF16)
    return jnp.where(lane_lo, hi.astype(F32), pltpu.roll(mid.astype(F32), HEADS, axis=1)).astype(BF16)


def _masked_cumsum(mask_bf, da):
    hi, mid, lo = _bf16_pieces(da)
    p = jnp.dot(mask_bf, jnp.concatenate([hi, mid, lo], axis=1), preferred_element_type=F32)
    return p[:, :LANES] + p[:, LANES:2 * LANES] + p[:, 2 * LANES:]


def _conv_silu(buf_ref, base, n, cw, cbias):
    taps = SSD_CONV_WIDTH
    acc = cw[taps - 1:taps, :] * buf_ref[base:base + n, :]
    for d in range(1, taps):
        acc = acc + cw[taps - 1 - d:taps - d, :] * buf_ref[base - d:base - d + n, :]
    return _silu(acc + cbias)


def _ssd_prep_kernel(x_ref, wdt_ref, dtb_ref, al_ref, pk_acs_ref, pk3_ref, acst_ref, tott_ref):
    li, si = _chunk_iotas()
    lane_lo = si < HEADS
    same_seq = (li // DEC_SEQ) == (si // DEC_SEQ)
    a_row = -jnp.exp(al_ref[...])
    wdt = wdt_ref[...].astype(BF16)
    wdt = jnp.concatenate([wdt, jnp.zeros_like(wdt)], axis=0)
    dt_raw = lax.dot_general(x_ref[...], wdt, _NT, preferred_element_type=F32)
    for k in range(PREP_NB):
        c = pl.program_id(0) * PREP_NB + k
        rows = pl.ds(k * CHUNK, CHUNK)
        same = jnp.logical_or(c < N_MAIN_CHUNKS, same_seq)
        causal = jnp.logical_and(si <= li, same)
        row = lax.broadcasted_iota(jnp.int32, (CHUNK, 1), 0) + c * CHUNK
        valid = jnp.logical_or(row < T_PROMPT, row >= T_MAIN)
        dt = jnp.where(jnp.logical_and(valid, lane_lo),
                       _softplus(dt_raw[k * CHUNK:(k + 1) * CHUNK, :] + dtb_ref[...]), 0.0)
        da = dt * a_row
        acs = _masked_cumsum(causal.astype(F32).astype(BF16), da)
        tot = _masked_cumsum(same.astype(F32).astype(BF16), da)
        pk_acs_ref[rows, :] = _split_pack(acs, lane_lo)
        pk3_ref[rows, :] = jnp.concatenate([_split_pack2(dt, lane_lo), _split_pack2(jnp.exp(tot - acs), lane_lo),
                                            _split_pack2(jnp.exp(acs), lane_lo)], axis=1)
        acst_ref[k] = acs.T
        tott_ref[k] = tot.T


def _ssd_prep(xb, wt, dt_bias, a_log):
    n_chunks = T_ALL // CHUNK
    rows = PREP_NB * CHUNK
    tr = pl.BlockSpec((PREP_NB, LANES, CHUNK), lambda c: (c, 0, 0))
    return pl.pallas_call(
        _ssd_prep_kernel,
        grid=(n_chunks // PREP_NB,),
        in_specs=[
            pl.BlockSpec((rows, D_MODEL), lambda c: (c, 0)),
            pl.BlockSpec((HEADS, D_MODEL), lambda c: (DT_COL // HEADS, 0)),
            pl.BlockSpec((1, LANES), lambda c: (0, 0)),
            pl.BlockSpec((1, LANES), lambda c: (0, 0)),
        ],
        out_specs=[
            pl.BlockSpec((rows, SPLIT_K), lambda c: (c, 0)),
            pl.BlockSpec((rows, 3 * LANES), lambda c: (c, 0)),
            tr, tr,
        ],
        out_shape=[
            jax.ShapeDtypeStruct((T_ALL, SPLIT_K), BF16),
            jax.ShapeDtypeStruct((T_ALL, 3 * LANES), BF16),
            jax.ShapeDtypeStruct((n_chunks, LANES, CHUNK), F32),
            jax.ShapeDtypeStruct((n_chunks, LANES, CHUNK), F32),
        ],
        compiler_params=_cparams(("parallel",)),
        name="ssd_prep",
    )(xb, wt, dt_bias, a_log)


def _intra_chunk(xbc, pk_acs, pk3, arow, causal, lane_lo, r_ref, e_ref):
    xs = xbc[:, :GROUP_W]
    bmb = xbc[:, GROUP_W:GROUP_W + STATE].astype(BF16)
    cmb = xbc[:, GROUP_W + STATE:].astype(BF16)
    colmat = jnp.dot(pk_acs, r_ref[...], preferred_element_type=F32)
    packed = jnp.concatenate([pk3[:, :LANES], pk3[:, LANES:2 * LANES], pk3[:, 2 * LANES:]], axis=0)
    ex = jnp.dot(packed, e_ref[...], preferred_element_type=F32)
    dtx, tex, eax = ex[:CHUNK], ex[CHUNK:2 * CHUNK], ex[2 * CHUNK:]
    xdt = xs * dtx
    cb = lax.dot_general(cmb, bmb, _NT, preferred_element_type=F32)
    ys = []
    for q in range(HEADS_PER_GROUP // 2):
        scs = []
        for r in (2 * q, 2 * q + 1):
            seg = colmat[:, r * CHUNK:(r + 1) * CHUNK] - arow[r:r + 1, :]
            scs.append((cb * jnp.exp(jnp.where(causal, seg, -jnp.inf))).astype(BF16))
        xp = xdt[:, q * LANES:(q + 1) * LANES]
        rhs = jnp.concatenate([jnp.where(lane_lo, xp, 0.0).astype(BF16),
                               jnp.where(lane_lo, 0.0, xp).astype(BF16)], axis=0)
        ys.append(jnp.dot(jnp.concatenate(scs, axis=1), rhs, preferred_element_type=F32))
    y_intra = jnp.concatenate(ys, axis=1)
    xw = (xdt * tex).astype(BF16)
    return xs, bmb, cmb, y_intra, eax, xw


def _gate_norm(y, z, nw):
    y = y * _silu(z)
    ms = jnp.mean(y * y, axis=-1, keepdims=True)
    return (y * lax.rsqrt(ms + RMS_EPS) * nw).astype(BF16)


def _chunk_iotas():
    li = lax.broadcasted_iota(jnp.int32, (CHUNK, CHUNK), 0)
    si = lax.broadcasted_iota(jnp.int32, (CHUNK, CHUNK), 1)
    return li, si


def _ssd_main_kernel(z_ref, xr_ref, br_ref, cr_ref, pk_acs_ref, pk3_ref, acst_ref, tott_ref, cw_ref, cbias_ref,
                     dx_ref, nw_ref, r_ref, e_ref, y_ref, sfin_ref, buf_ref, st_ref):
    del tott_ref
    g = pl.program_id(0)
    rb = pl.program_id(1)

    @pl.when(rb == 0)
    def _():
        st_ref[...] = jnp.zeros_like(st_ref)
        buf_ref[0:CONV_HEAD, :] = jnp.zeros((CONV_HEAD, XBC_W), F32)

    @pl.when(rb > 0)
    def _():
        buf_ref[0:CONV_HEAD, :] = buf_ref[SSD_RB:SSD_RB + CONV_HEAD, :]

    buf_ref[CONV_HEAD:, 0:GROUP_W] = xr_ref[...]
    buf_ref[CONV_HEAD:, GROUP_W:GROUP_W + STATE] = br_ref[...]
    buf_ref[CONV_HEAD:, GROUP_W + STATE:] = cr_ref[...]

    li, si = _chunk_iotas()
    causal = si <= li
    lane_lo = si < HEADS
    g8 = pl.multiple_of(g * HEADS_PER_GROUP, SUBLANES)
    cw = cw_ref[...]
    cbias = cbias_ref[...]

    for k in range(SSD_NB):
        rows = pl.ds(k * CHUNK, CHUNK)
        xbc = _conv_silu(buf_ref, CONV_HEAD + k * CHUNK, CHUNK, cw, cbias)
        arow = acst_ref[k, pl.ds(g8, HEADS_PER_GROUP), :]
        xs, bmb, cmb, y_intra, eax, xw = _intra_chunk(xbc, pk_acs_ref[rows, :], pk3_ref[rows, :], arow,
                                                      causal, lane_lo, r_ref, e_ref)
        st = st_ref[...]
        y = y_intra + jnp.dot(cmb, st.astype(BF16), preferred_element_type=F32) * eax + dx_ref[...] * xs
        y_ref[rows, :] = _gate_norm(y, z_ref[rows, :], nw_ref[...])
        st_ref[...] = st * eax[CHUNK - 1:CHUNK, :] + lax.dot_general(bmb, xw, _TN, preferred_element_type=F32)

    @pl.when(rb == T_MAIN // SSD_RB - 1)
    def _():
        sfin_ref[...] = st_ref[...]


def _ssd_samp_kernel(z_ref, xr_ref, br_ref, cr_ref, pk_acs_ref, pk3_ref, acst_ref, tott_ref, cw_ref, cbias_ref,
                     dx_ref, nw_ref, r_ref, e_ref, prev_ref, s0_ref, yprev_ref, y_ref, s1_ref,
                     buf_ref, xbc_ref, yint_ref):
    del yprev_ref
    g = pl.program_id(0)
    slot = CONV_HEAD + DEC_SEQ
    cw = cw_ref[...]
    cbias = cbias_ref[...]
    for q in range(SEQ_PER_CHUNK):
        r0 = q * DEC_SEQ
        top = q * slot + CONV_HEAD
        buf_ref[top - (SSD_CONV_WIDTH - 1):top, :] = prev_ref[q]
        buf_ref[top:top + DEC_SEQ, 0:GROUP_W] = xr_ref[r0:r0 + DEC_SEQ, :]
        buf_ref[top:top + DEC_SEQ, GROUP_W:GROUP_W + STATE] = br_ref[r0:r0 + DEC_SEQ, :]
        buf_ref[top:top + DEC_SEQ, GROUP_W + STATE:] = cr_ref[r0:r0 + DEC_SEQ, :]
    for q in range(SEQ_PER_CHUNK):
        r0 = q * DEC_SEQ
        xbc_ref[r0:r0 + DEC_SEQ, :] = _conv_silu(buf_ref, q * slot + CONV_HEAD, DEC_SEQ, cw, cbias)

    li, si = _chunk_iotas()
    same = (li // DEC_SEQ) == (si // DEC_SEQ)
    causal = jnp.logical_and(si <= li, same)
    lane_lo = si < HEADS
    g8 = pl.multiple_of(g * HEADS_PER_GROUP, SUBLANES)

    arow = acst_ref[0, pl.ds(g8, HEADS_PER_GROUP), :]
    trow = tott_ref[0, pl.ds(g8, HEADS_PER_GROUP), :]
    xs, bmb, cmb, y_intra, eax, xw = _intra_chunk(xbc_ref[...], pk_acs_ref[...], pk3_ref[...], arow,
                                                  causal, lane_lo, r_ref, e_ref)
    for q in range(SEQ_PER_CHUNK):
        r0 = q * DEC_SEQ
        yint_ref[r0:r0 + DEC_SEQ, :] = lax.dot_general(cmb[r0:r0 + DEC_SEQ, :], s0_ref[q].astype(BF16), _NT,
                                                       preferred_element_type=F32)
    y = y_intra + yint_ref[...] * eax + dx_ref[...] * xs
    y_ref[...] = _gate_norm(y, z_ref[...], nw_ref[...])
    for q in range(SEQ_PER_CHUNK):
        r0 = q * DEC_SEQ
        upd = lax.dot_general(xw[r0:r0 + DEC_SEQ, :], bmb[r0:r0 + DEC_SEQ, :], _TN, preferred_element_type=F32)
        for r in range(HEADS_PER_GROUP):
            lo, hi = r * HEADDIM, (r + 1) * HEADDIM
            s1_ref[q, lo:hi, :] = s0_ref[q, lo:hi, :] * jnp.exp(trow[r:r + 1, r0:r0 + 1]) + upd[lo:hi, :]


def _ssd_specs(rows, row0):
    xcol = D_INNER // GROUP_W
    bcol = (2 * D_INNER) // STATE
    return [
        pl.BlockSpec((rows, GROUP_W), lambda g, c: (row0 + c, g)),
        pl.BlockSpec((rows, GROUP_W), lambda g, c: (row0 + c, xcol + g)),
        pl.BlockSpec((rows, STATE), lambda g, c: (row0 + c, bcol + g)),
        pl.BlockSpec((rows, STATE), lambda g, c: (row0 + c, bcol + GROUPS + g)),
        pl.BlockSpec((rows, SPLIT_K), lambda g, c: (row0 + c, 0)),
        pl.BlockSpec((rows, 3 * LANES), lambda g, c: (row0 + c, 0)),
        pl.BlockSpec((rows // CHUNK, LANES, CHUNK), lambda g, c: (row0 + c, 0, 0)),
        pl.BlockSpec((rows // CHUNK, LANES, CHUNK), lambda g, c: (row0 + c, 0, 0)),
        pl.BlockSpec((None, SSD_CONV_WIDTH, XBC_W), lambda g, c: (g, 0, 0)),
        pl.BlockSpec((None, 1, XBC_W), lambda g, c: (g, 0, 0)),
        pl.BlockSpec((1, GROUP_W), lambda g, c: (0, g)),
        pl.BlockSpec((1, GROUP_W), lambda g, c: (0, g)),
        pl.BlockSpec((None, SPLIT_K, HEADS_PER_GROUP * CHUNK), lambda g, c: (g, 0, 0)),
        pl.BlockSpec((None, LANES, GROUP_W), lambda g, c: (g, 0, 0)),
    ]


def _spread_matrices():
    k = np.arange(SPLIT_K)
    head = k % HEADS
    used = (k // HEADS) < 3
    g = np.arange(GROUPS)[:, None, None]
    rj = np.arange(HEADS_PER_GROUP * CHUNK)[None, None, :] // CHUNK
    ej = np.arange(GROUP_W)[None, None, :] // HEADDIM
    hk = head[None, :, None]
    uk = used[None, :, None]
    r = (uk & (hk == g * HEADS_PER_GROUP + rj)).astype(np.float32)
    e = (hk == g * HEADS_PER_GROUP + ej)[:, :LANES].astype(np.float32)
    return jnp.asarray(r, BF16), jnp.asarray(e, BF16)


def _per_group(a):
    lead = a.shape[:-1]
    x = a[..., :D_INNER].reshape(*lead, GROUPS, GROUP_W)
    b = a[..., D_INNER:D_INNER + GROUPS * STATE].reshape(*lead, GROUPS, STATE)
    c = a[..., D_INNER + GROUPS * STATE:].reshape(*lead, GROUPS, STATE)
    return jnp.moveaxis(jnp.concatenate([x, b, c], axis=-1), -2, 0)


def _ssd(zxd, xb, wt, conv_w, conv_b, dt_bias, a_log, d_skip, norm_w, conv_prev, state0):
    r_mat, e_mat = _spread_matrices()
    pad = ((0, 0), (0, LANES - HEADS))
    prep = _ssd_prep(xb, wt, jnp.pad(dt_bias.reshape(1, HEADS).astype(F32), pad),
                     jnp.pad(a_log.reshape(1, HEADS).astype(F32), pad))
    params = (_per_group(conv_w), _per_group(conv_b.reshape(1, CONV_DIM)),
              jnp.repeat(d_skip.astype(F32), HEADDIM).reshape(1, D_INNER),
              norm_w.reshape(1, D_INNER).astype(F32), r_mat, e_mat)
    common = (zxd,) * 4 + tuple(prep) + params
    y, s_fin = pl.pallas_call(
        _ssd_main_kernel,
        grid=(GROUPS, T_MAIN // SSD_RB),
        in_specs=_ssd_specs(SSD_RB, 0),
        out_specs=[
            pl.BlockSpec((SSD_RB, GROUP_W), lambda g, c: (c, g)),
            pl.BlockSpec((None, STATE, GROUP_W), lambda g, c: (g, 0, 0)),
        ],
        out_shape=[
            jax.ShapeDtypeStruct((T_ALL, D_INNER), BF16),
            jax.ShapeDtypeStruct((GROUPS, STATE, GROUP_W), F32),
        ],
        scratch_shapes=[
            pltpu.VMEM((CONV_HEAD + SSD_RB, XBC_W), F32),
            pltpu.VMEM((STATE, GROUP_W), F32),
        ],
        compiler_params=_cparams(("parallel", "arbitrary")),
        name="ssd_prompt",
    )(*common)

    n_in = len(common)
    y, s_new = pl.pallas_call(
        _ssd_samp_kernel,
        grid=(GROUPS, N_SAMP_CHUNKS),
        in_specs=_ssd_specs(CHUNK, N_MAIN_CHUNKS) + [
            pl.BlockSpec((None, SEQ_PER_CHUNK, SSD_CONV_WIDTH - 1, XBC_W), lambda g, c: (g, c, 0, 0)),
            pl.BlockSpec((SEQ_PER_CHUNK, None, GROUP_W, STATE), lambda g, c: (c, g, 0, 0)),
            pl.BlockSpec(memory_space=pl.ANY),
        ],
        out_specs=[
            pl.BlockSpec((CHUNK, GROUP_W), lambda g, c: (N_MAIN_CHUNKS + c, g)),
            pl.BlockSpec((SEQ_PER_CHUNK, None, GROUP_W, STATE), lambda g, c: (c, g, 0, 0)),
        ],
        out_shape=[
            jax.ShapeDtypeStruct((T_ALL, D_INNER), BF16),
            jax.ShapeDtypeStruct((DEC_BATCH, GROUPS, GROUP_W, STATE), F32),
        ],
        scratch_shapes=[
            pltpu.VMEM((SEQ_PER_CHUNK * (CONV_HEAD + DEC_SEQ), XBC_W), F32),
            pltpu.VMEM((CHUNK, XBC_W), F32),
            pltpu.VMEM((CHUNK, GROUP_W), F32),
        ],
        input_output_aliases={n_in + 2: 0},
        compiler_params=_cparams(("parallel", "arbitrary")),
        name="ssd_sample",
    )(*common, _per_group(conv_prev), state0, y)
    return y, s_fin, s_new


def _tail_rows(a, n):
    c = a.shape[-1]
    p = a[T_PROMPT - n:T_PROMPT].reshape(1, 1, n, c)
    s = a[T_MAIN:].reshape(DEC_BATCH, DEC_SEQ, c)[:, DEC_SEQ - n:].reshape(1, DEC_BATCH, n, c)
    return p, s


def kernel(x_prompt, x_sample, cache_sc_conv, state_ssd_conv, state_ssd, meta_tokens, ln_g, ln_b,
           ffn_w1, ffn_w3, ffn_w2, sc_w_in, sc_w_conv, sc_w_out,
           ssd_w_in, ssd_w_conv, ssd_b_conv, ssd_dt_bias, ssd_a_log, ssd_d, ssd_norm_w, ssd_w_out):
    w1, w3, w2 = ffn_w1, ffn_w3, ffn_w2
    sc_out_w = sc_w_out.astype(BF16)
    ssd_out_w = ssd_w_out.astype(BF16)
    lng = ln_g.reshape(DEPTH, 3, 1, D_MODEL)
    lnb = ln_b.reshape(DEPTH, 3, 1, D_MODEL)

    h = jnp.concatenate([meta_tokens.astype(F32), x_prompt[0], jnp.zeros((N_PAD, D_MODEL), F32),
                         x_sample.reshape(N_SAMP, D_MODEL)], axis=0)

    h, hb = _ffn(h, w1, w3, w2, lng, lnb, 0, 0, 0, emit_bf16=True)
    v, u_tail = _sc_mix(hb, sc_w_in, 0, sc_w_conv[0], cache_sc_conv[0])
    keep = SC_WIDTH - 1
    new_sc_p = u_tail[T_PROMPT - keep - SC_TAIL0:T_PROMPT - SC_TAIL0].reshape(1, 1, keep, D_MODEL)
    new_sc_s = u_tail[SC_SAMP_OFF:].reshape(DEC_BATCH, DEC_SEQ, D_MODEL)[:, DEC_SEQ - keep:].reshape(
        1, DEC_BATCH, keep, D_MODEL)
    h = _proj_ln(v, sc_out_w, 0, h, lng, lnb, 0, 1)
    h = _ffn(h, w1, w3, w2, lng, lnb, 0, 1, 2)

    h, hb = _ffn(h, w1, w3, w2, lng, lnb, 1, 0, 0, emit_bf16=True)
    ssd_wt = jnp.swapaxes(ssd_w_in[0], 0, 1)
    zxd = _mm(hb, ssd_wt, ZXD_W, ZXD_TN)
    new_conv_p, new_conv_s = _tail_rows(zxd[:, D_INNER:DT_COL], SSD_CONV_WIDTH - 1)
    state0 = state_ssd[0].reshape(DEC_BATCH, GROUPS, GROUP_W, STATE)
    y, s_fin, s_new = _ssd(zxd, hb, ssd_wt, ssd_w_conv[0], ssd_b_conv[0], ssd_dt_bias[0], ssd_a_log[0],
                           ssd_d[0], ssd_norm_w[0], state_ssd_conv[0], state0)
    h = _proj_ln(y, ssd_out_w, 0, h, lng, lnb, 1, 1)
    h = _ffn(h, w1, w3, w2, lng, lnb, 1, 1, 2)

    y_prompt = h[N_META:T_PROMPT].reshape(1, SEQ, D_MODEL)
    y_sample = h[T_MAIN:].reshape(DEC_BATCH, DEC_SEQ, D_MODEL)
    new_ssd_p = jnp.swapaxes(s_fin, 1, 2).reshape(1, 1, HEADS, HEADDIM, STATE).astype(state_ssd.dtype)
    new_ssd_s = s_new.reshape(1, DEC_BATCH, HEADS, HEADDIM, STATE).astype(state_ssd.dtype)
    return (y_prompt, y_sample, new_sc_p, new_sc_s, new_conv_p, new_conv_s, new_ssd_p, new_ssd_s)
```

```python
import functools

import numpy as np

import jax
import jax.numpy as jnp
from jax import lax
from jax.experimental import pallas as pl
from jax.experimental.pallas import tpu as pltpu

F32 = jnp.float32
BF16 = jnp.bfloat16

D_MODEL = 2048
SEQ = 8192
DEPTH = 2
DEC_BATCH = 32
DEC_SEQ = 16
N_META = 16
D_FF = 5632
SC_WIDTH = 3
D_INNER = 4096
HEADDIM = 64
HEADS = 64
GROUPS = 8
HEADS_PER_GROUP = 8
STATE = 128
SSD_CONV_WIDTH = 4
CONV_DIM = D_INNER + 2 * GROUPS * STATE
GROUP_W = HEADS_PER_GROUP * HEADDIM
XBC_W = GROUP_W + 2 * STATE
ALPHA = (2.0 * DEPTH) ** 0.25
LN_EPS = 1e-5
RMS_EPS = 1e-5

CHUNK = 128
T_PROMPT = N_META + SEQ
N_MAIN_CHUNKS = -(-T_PROMPT // CHUNK)
T_MAIN = N_MAIN_CHUNKS * CHUNK
N_PAD = T_MAIN - T_PROMPT
N_SAMP = DEC_BATCH * DEC_SEQ
N_SAMP_CHUNKS = N_SAMP // CHUNK
SEQ_PER_CHUNK = CHUNK // DEC_SEQ
T_ALL = T_MAIN + N_SAMP

TM = 736
TM_IN = 2 * TM
TF = 512
TF_HEAD = 256
TM_WIDE = 1104
FFN_SLAB = 368
DT_COL = D_INNER + CONV_DIM
ZXD_W = DT_COL
ZXD_TN = 1280
SSD_NB = 13
SSD_RB = SSD_NB * CHUNK
PREP_NB = 3
LANES = 128
SUBLANES = 8
CONV_HEAD = SUBLANES

VMEM_LIMIT_BYTES = 56 * 1024 * 1024
VMEM_LIMIT_WIDE_BYTES = 61 * 1024 * 1024


def _cparams(sem, vmem_limit_bytes=VMEM_LIMIT_BYTES):
    return pltpu.CompilerParams(dimension_semantics=sem, vmem_limit_bytes=vmem_limit_bytes)


def _layer_norm(v, g, b):
    mu = jnp.mean(v, axis=-1, keepdims=True)
    c = v - mu
    var = jnp.mean(c * c, axis=-1, keepdims=True)
    return c * lax.rsqrt(var + LN_EPS) * g + b


def _silu(x):
    h = 0.5 * x
    return h + h * jnp.tanh(h)


def _softplus(x):
    return jnp.maximum(x, 0.0) + jnp.log(1.0 + jnp.exp(-jnp.abs(x)))


def _ffn_kernel(*refs, nf, n_alias, emit_bf16, convert):
    x_ref, w1_ref, w3_ref, w2_ref, g_ref, b_ref = refs[:6]
    outs = refs[6 + n_alias:]
    o_ref = outs[0]
    pos = 1
    ob_ref = None
    if emit_bf16:
        ob_ref = outs[pos]
        pos += 1
    if convert:
        wb_refs = outs[pos:pos + 3]
        for src, dst in zip((w1_ref, w3_ref, w2_ref), wb_refs):
            dst[...] = src[...].astype(BF16)
        w1_ref, w3_ref, w2_ref = wb_refs
        pos += 3
    xb_ref = outs[pos]
    f = pl.program_id(1)

    @pl.when(f == 0)
    def _():
        xb_ref[...] = x_ref[...].astype(BF16)
        o_ref[...] = jnp.zeros_like(o_ref)

    rs = FFN_SLAB
    n_slabs = x_ref.shape[0] // rs

    def slab(s, last):
        rows = pl.ds(s * rs, rs)
        xb = xb_ref[rows, :]
        h1 = jnp.dot(xb, w1_ref[...], preferred_element_type=F32)
        h3 = jnp.dot(xb, w3_ref[...], preferred_element_type=F32)
        gate = (_silu(h1) * h3).astype(BF16)
        acc = o_ref[rows, :] + jnp.dot(gate, w2_ref[...], preferred_element_type=F32)
        if last:
            res = _layer_norm(ALPHA * x_ref[rows, :] + 0.5 * acc, g_ref[...], b_ref[...])
            o_ref[rows, :] = res
            if ob_ref is not None:
                ob_ref[rows, :] = res.astype(BF16)
        else:
            o_ref[rows, :] = acc

    @pl.when(f < nf - 1)
    def _():
        for s in range(n_slabs):
            slab(s, False)

    @pl.when(f == nf - 1)
    def _():
        for s in range(n_slabs):
            slab(s, True)


def _ffn(h, w1, w3, w2, lng, lnb, i, j, k, emit_bf16=False):
    tm, vmem = (TM, VMEM_LIMIT_BYTES) if emit_bf16 else (TM_WIDE, VMEM_LIMIT_WIDE_BYTES)
    ln_spec = pl.BlockSpec((None, None, 1, D_MODEL), lambda m, f: (i, k, 0, 0))
    h_shapes = [jax.ShapeDtypeStruct((T_ALL, D_MODEL), F32)]
    if emit_bf16:
        h_shapes.append(jax.ShapeDtypeStruct((T_ALL, D_MODEL), BF16))
    n_h = len(h_shapes)
    scratch = [pltpu.VMEM((tm, D_MODEL), BF16)]

    head_row = pl.BlockSpec((tm, D_MODEL), lambda m, f: (0, 0))
    nf = D_FF // TF
    tf_head = TF if emit_bf16 else TF_HEAD
    nf_head = D_FF // tf_head
    per = TF // tf_head
    head = pl.pallas_call(
        functools.partial(_ffn_kernel, nf=nf_head, n_alias=0, emit_bf16=emit_bf16, convert=True),
        grid=(1, nf_head),
        in_specs=[
            head_row,
            pl.BlockSpec((None, None, D_MODEL, tf_head), lambda m, f: (i, j, 0, f)),
            pl.BlockSpec((None, None, D_MODEL, tf_head), lambda m, f: (i, j, 0, f)),
            pl.BlockSpec((None, None, tf_head, D_MODEL), lambda m, f: (i, j, f, 0)),
            ln_spec, ln_spec,
        ],
        out_specs=[head_row] * n_h + [
            pl.BlockSpec((None, D_MODEL, tf_head), lambda m, f: (f // per, 0, f % per)),
            pl.BlockSpec((None, D_MODEL, tf_head), lambda m, f: (f // per, 0, f % per)),
            pl.BlockSpec((tf_head, D_MODEL), lambda m, f: (f, 0)),
        ],
        out_shape=h_shapes + [
            jax.ShapeDtypeStruct((nf, D_MODEL, TF), BF16),
            jax.ShapeDtypeStruct((nf, D_MODEL, TF), BF16),
            jax.ShapeDtypeStruct((D_FF, D_MODEL), BF16),
        ],
        scratch_shapes=scratch,
        compiler_params=_cparams(("parallel", "arbitrary"), VMEM_LIMIT_WIDE_BYTES),
        name="ffn_ln_head",
    )(h, w1, w3, w2, lng, lnb)
    h_parts, (w1b, w3b, w2b) = head[:n_h], head[n_h:]

    row = pl.BlockSpec((tm, D_MODEL), lambda m, f: (m + 1, 0))
    out = pl.pallas_call(
        functools.partial(_ffn_kernel, nf=nf, n_alias=n_h, emit_bf16=emit_bf16, convert=False),
        grid=(T_ALL // tm - 1, nf),
        in_specs=[
            row,
            pl.BlockSpec((None, D_MODEL, TF), lambda m, f: (f, 0, 0)),
            pl.BlockSpec((None, D_MODEL, TF), lambda m, f: (f, 0, 0)),
            pl.BlockSpec((TF, D_MODEL), lambda m, f: (f, 0)),
            ln_spec, ln_spec,
        ] + [pl.BlockSpec(memory_space=pl.ANY)] * n_h,
        out_specs=[row] * n_h,
        out_shape=h_shapes,
        scratch_shapes=scratch,
        input_output_aliases={6 + a: a for a in range(n_h)},
        compiler_params=_cparams(("parallel", "arbitrary"), vmem),
        name="ffn_ln",
    )(h, w1b, w3b, w2b, lng, lnb, *h_parts)
    return tuple(out) if emit_bf16 else out[0]


def _mm_kernel(x_ref, wt_ref, o_ref, wb_ref):
    @pl.when(pl.program_id(1) == 0)
    def _():
        wb_ref[...] = wt_ref[...].astype(BF16)

    o_ref[...] = lax.dot_general(x_ref[...], wb_ref[...], (((1,), (1,)), ((), ())),
                                 preferred_element_type=F32)


def _mm(xb, wt, n_out, tn):
    n, k = wt.shape
    assert n_out % tn == 0 and n_out <= n
    return pl.pallas_call(
        _mm_kernel,
        grid=(n_out // tn, T_ALL // TM_IN),
        in_specs=[
            pl.BlockSpec((TM_IN, k), lambda j, m: (m, 0)),
            pl.BlockSpec((tn, k), lambda j, m: (j, 0)),
        ],
        out_specs=pl.BlockSpec((TM_IN, tn), lambda j, m: (m, j)),
        out_shape=jax.ShapeDtypeStruct((T_ALL, n_out), F32),
        scratch_shapes=[pltpu.VMEM((tn, k), BF16)],
        compiler_params=_cparams(("parallel", "arbitrary"), VMEM_LIMIT_WIDE_BYTES),
        name="in_proj",
    )(xb, wt)


TM_SC = TM_WIDE
SC_TAIL0 = T_ALL - TM_SC
SC_SAMP_OFF = T_MAIN - SC_TAIL0
assert SC_SAMP_OFF % DEC_SEQ == 0 and T_PROMPT - (SC_WIDTH - 1) >= SC_TAIL0


def _sc_mix_kernel(x_ref, wb_ref, wc_ref, wh_ref, cw_ref, o1_ref, o2_ref, v_ref, ut_ref,
                   wbb_ref, wcb_ref, whb_ref, ubuf_ref, *, nm):
    m = pl.program_id(1)
    tn = ubuf_ref.shape[1]

    @pl.when(m == 0)
    def _():
        wbb_ref[...] = wb_ref[...].astype(BF16)
        wcb_ref[...] = wc_ref[...].astype(BF16)
        whb_ref[...] = wh_ref[...].astype(BF16)
        ubuf_ref[0:CONV_HEAD, :] = jnp.zeros((CONV_HEAD, tn), F32)

    cw = cw_ref[...]
    ov_refs = (o1_ref, o2_ref)
    rs = FFN_SLAB
    n_slabs = TM_SC // rs

    def slab(s, last_tile):
        r0 = s * rs
        xb = x_ref[r0:r0 + rs, :]
        bg = jnp.dot(xb, wbb_ref[...], preferred_element_type=F32)
        c = jnp.dot(xb, wcb_ref[...], preferred_element_type=F32)
        hh = jnp.dot(xb, whb_ref[...], preferred_element_type=F32)
        u = c * hh
        ubuf_ref[CONV_HEAD + r0:CONV_HEAD + r0 + rs, :] = u
        acc = cw[SC_WIDTH - 1:SC_WIDTH, :] * u
        if last_tile:
            lrow = lax.broadcasted_iota(jnp.int32, (rs, 1), 0) + r0
            spos = jnp.bitwise_and(lrow, DEC_SEQ - 1)
        for d in range(1, SC_WIDTH):
            ud = ubuf_ref[CONV_HEAD + r0 - d:CONV_HEAD + r0 - d + rs, :]
            if last_tile:
                first = jnp.logical_and(lrow >= SC_SAMP_OFF, spos < d)
                ud = jnp.where(first, ov_refs[d - 1][r0:r0 + rs, :], ud)
            acc = acc + cw[SC_WIDTH - 1 - d:SC_WIDTH - d, :] * ud
        v_ref[r0:r0 + rs, :] = (bg * acc).astype(BF16)
        if last_tile:
            ut_ref[r0:r0 + rs, :] = u

    @pl.when(m < nm - 1)
    def _():
        for s in range(n_slabs):
            slab(s, False)

    @pl.when(m == nm - 1)
    def _():
        for s in range(n_slabs):
            slab(s, True)

    ubuf_ref[0:CONV_HEAD, :] = ubuf_ref[TM_SC:TM_SC + CONV_HEAD, :]


def _sc_mix(xb, w_in, j, w_conv, conv_prev):
    tn = 512
    nb = D_MODEL // tn
    nm = T_ALL // TM_SC
    ovs = [jnp.pad(o, ((SC_SAMP_OFF, 0), (0, 0))) for o in _conv_overrides(conv_prev, SC_WIDTH)]
    tail = pl.BlockSpec((TM_SC, tn), lambda n, m: (0, n))
    wscratch = pltpu.VMEM((D_MODEL, tn), BF16)
    return pl.pallas_call(
        functools.partial(_sc_mix_kernel, nm=nm),
        grid=(nb, nm),
        in_specs=[
            pl.BlockSpec((TM_SC, D_MODEL), lambda n, m: (m, 0)),
            pl.BlockSpec((None, D_MODEL, tn), lambda n, m: (j, 0, n)),
            pl.BlockSpec((None, D_MODEL, tn), lambda n, m: (j, 0, n + nb)),
            pl.BlockSpec((None, D_MODEL, tn), lambda n, m: (j, 0, n + 2 * nb)),
            pl.BlockSpec((SC_WIDTH, tn), lambda n, m: (0, n)),
            tail, tail,
        ],
        out_specs=[pl.BlockSpec((TM_SC, tn), lambda n, m: (m, n)), tail],
        out_shape=[jax.ShapeDtypeStruct((T_ALL, D_MODEL), BF16),
                   jax.ShapeDtypeStruct((TM_SC, D_MODEL), F32)],
        scratch_shapes=[wscratch, wscratch, wscratch, pltpu.VMEM((CONV_HEAD + TM_SC, tn), F32)],
        compiler_params=_cparams(("parallel", "arbitrary"), VMEM_LIMIT_WIDE_BYTES),
        name="sc_mix",
    )(xb, w_in, w_in, w_in, w_conv.astype(F32), *ovs)


def _conv_overrides(prev, taps):
    c = prev.shape[-1]
    out = []
    for d in range(1, taps):
        o = jnp.zeros((DEC_BATCH, DEC_SEQ, c), F32)
        for p in range(d):
            o = o.at[:, p].set(prev[:, taps - 1 + p - d])
        out.append(o.reshape(N_SAMP, c))
    return out


def _proj_ln_kernel(y_ref, w_ref, h_ref, g_ref, b_ref, o_ref, *, nk):
    rs = FFN_SLAB

    def slab_dot(rows):
        return jnp.dot(y_ref[rows, :], w_ref[...], preferred_element_type=F32)

    def finish(rows, acc):
        v = ALPHA * h_ref[rows, :] + acc
        o_ref[rows, :] = _layer_norm(v, g_ref[...], b_ref[...])

    slabs = [pl.ds(s * rs, rs) for s in range(y_ref.shape[0] // rs)]
    if nk == 1:
        for rows in slabs:
            finish(rows, slab_dot(rows))
        return

    k = pl.program_id(1)

    @pl.when(k == 0)
    def _():
        for rows in slabs:
            o_ref[rows, :] = slab_dot(rows)

    @pl.when(k == 1)
    def _():
        for rows in slabs:
            finish(rows, o_ref[rows, :] + slab_dot(rows))


def _proj_ln(y, w, j, h, lng, lnb, i, k):
    kdim = w.shape[1]
    tk = 2048
    nk = kdim // tk
    assert nk in (1, 2)
    return pl.pallas_call(
        functools.partial(_proj_ln_kernel, nk=nk),
        grid=(T_ALL // TM, nk),
        in_specs=[
            pl.BlockSpec((TM, tk), lambda m, kk: (m, kk)),
            pl.BlockSpec((None, tk, D_MODEL), lambda m, kk: (j, kk, 0)),
            pl.BlockSpec((TM, D_MODEL), lambda m, kk: (m, 0)),
            pl.BlockSpec((None, None, 1, D_MODEL), lambda m, kk: (i, k, 0, 0)),
            pl.BlockSpec((None, None, 1, D_MODEL), lambda m, kk: (i, k, 0, 0)),
        ],
        out_specs=pl.BlockSpec((TM, D_MODEL), lambda m, kk: (m, 0)),
        out_shape=jax.ShapeDtypeStruct((T_ALL, D_MODEL), F32),
        compiler_params=_cparams(("parallel", "arbitrary")),
        name="out_proj_ln",
    )(y, w, h, lng, lnb)


_NT = (((1,), (1,)), ((), ()))
_TN = (((0,), (0,)), ((), ()))
SPLIT_K = 2 * LANES


def _bf16_pieces(v):
    hi = v.astype(BF16)
    r1 = v - hi.astype(F32)
    mid = r1.astype(BF16)
    lo = (r1 - mid.astype(F32)).astype(BF16)
    return hi, mid, lo


def _split_pack(v, lane_lo):
    hi, mid, lo = _bf16_pieces(v)
    a = jnp.where(lane_lo, hi.astype(F32), pltpu.roll(mid.astype(F32), HEADS, axis=1))
    b = jnp.where(lane_lo, lo.astype(F32), 0.0)
    return jnp.concatenate([a, b], axis=1).astype(BF16)


def _split_pack2(v, lane_lo):
    hi = v.astype(BF16)
    mid = (v - hi.astype(F32)).astype(BF16)
    return jnp.where(lane_lo, hi.astype(F32), pltpu.roll(mid.astype(F32), HEADS, axis=1)).astype(BF16)


def _masked_cumsum(mask_bf, da):
    hi, mid, lo = _bf16_pieces(da)
    p = jnp.dot(mask_bf, jnp.concatenate([hi, mid, lo], axis=1), preferred_element_type=F32)
    return p[:, :LANES] + p[:, LANES:2 * LANES] + p[:, 2 * LANES:]


def _conv_silu(buf_ref, base, n, cw, cbias):
    taps = SSD_CONV_WIDTH
    acc = cw[taps - 1:taps, :] * buf_ref[base:base + n, :]
    for d in range(1, taps):
        acc = acc + cw[taps - 1 - d:taps - d, :] * buf_ref[base - d:base - d + n, :]
    return _silu(acc + cbias)


def _ssd_prep_kernel(x_ref, wdt_ref, dtb_ref, al_ref, pk_acs_ref, pk3_ref, acst_ref, tott_ref):
    li, si = _chunk_iotas()
    lane_lo = si < HEADS
    same_seq = (li // DEC_SEQ) == (si // DEC_SEQ)
    a_row = -jnp.exp(al_ref[...])
    wdt = wdt_ref[...].astype(BF16)
    wdt = jnp.concatenate([wdt, jnp.zeros_like(wdt)], axis=0)
    dt_raw = lax.dot_general(x_ref[...], wdt, _NT, preferred_element_type=F32)
    for k in range(PREP_NB):
        c = pl.program_id(0) * PREP_NB + k
        rows = pl.ds(k * CHUNK, CHUNK)
        same = jnp.logical_or(c < N_MAIN_CHUNKS, same_seq)
        causal = jnp.logical_and(si <= li, same)
        row = lax.broadcasted_iota(jnp.int32, (CHUNK, 1), 0) + c * CHUNK
        valid = jnp.logical_or(row < T_PROMPT, row >= T_MAIN)
        dt = jnp.where(jnp.logical_and(valid, lane_lo),
                       _softplus(dt_raw[k * CHUNK:(k + 1) * CHUNK, :] + dtb_ref[...]), 0.0)
        da = dt * a_row
        acs = _masked_cumsum(causal.astype(F32).astype(BF16), da)
        tot = _masked_cumsum(same.astype(F32).astype(BF16), da)
        pk_acs_ref[rows, :] = _split_pack(acs, lane_lo)
        pk3_ref[rows, :] = jnp.concatenate([_split_pack2(dt, lane_lo), _split_pack2(jnp.exp(tot - acs), lane_lo),
                                            _split_pack2(jnp.exp(acs), lane_lo)], axis=1)
        acst_ref[k] = acs.T
        tott_ref[k] = tot.T


def _ssd_prep(xb, wt, dt_bias, a_log):
    n_chunks = T_ALL // CHUNK
    rows = PREP_NB * CHUNK
    tr = pl.BlockSpec((PREP_NB, LANES, CHUNK), lambda c: (c, 0, 0))
    return pl.pallas_call(
        _ssd_prep_kernel,
        grid=(n_chunks // PREP_NB,),
        in_specs=[
            pl.BlockSpec((rows, D_MODEL), lambda c: (c, 0)),
            pl.BlockSpec((HEADS, D_MODEL), lambda c: (DT_COL // HEADS, 0)),
            pl.BlockSpec((1, LANES), lambda c: (0, 0)),
            pl.BlockSpec((1, LANES), lambda c: (0, 0)),
        ],
        out_specs=[
            pl.BlockSpec((rows, SPLIT_K), lambda c: (c, 0)),
            pl.BlockSpec((rows, 3 * LANES), lambda c: (c, 0)),
            tr, tr,
        ],
        out_shape=[
            jax.ShapeDtypeStruct((T_ALL, SPLIT_K), BF16),
            jax.ShapeDtypeStruct((T_ALL, 3 * LANES), BF16),
            jax.ShapeDtypeStruct((n_chunks, LANES, CHUNK), F32),
            jax.ShapeDtypeStruct((n_chunks, LANES, CHUNK), F32),
        ],
        compiler_params=_cparams(("parallel",)),
        name="ssd_prep",
    )(xb, wt, dt_bias, a_log)


def _intra_chunk(xbc, pk_acs, pk3, arow, causal, lane_lo, r_ref, e_ref):
    xs = xbc[:, :GROUP_W]
    bmb = xbc[:, GROUP_W:GROUP_W + STATE].astype(BF16)
    cmb = xbc[:, GROUP_W + STATE:].astype(BF16)
    colmat = jnp.dot(pk_acs, r_ref[...], preferred_element_type=F32)
    packed = jnp.concatenate([pk3[:, :LANES], pk3[:, LANES:2 * LANES], pk3[:, 2 * LANES:]], axis=0)
    ex = jnp.dot(packed, e_ref[...], preferred_element_type=F32)
    dtx, tex, eax = ex[:CHUNK], ex[CHUNK:2 * CHUNK], ex[2 * CHUNK:]
    xdt = xs * dtx
    cb = lax.dot_general(cmb, bmb, _NT, preferred_element_type=F32)
    ys = []
    for q in range(HEADS_PER_GROUP // 2):
        scs = []
        for r in (2 * q, 2 * q + 1):
            seg = colmat[:, r * CHUNK:(r + 1) * CHUNK] - arow[r:r + 1, :]
            scs.append((cb * jnp.exp(jnp.where(causal, seg, -jnp.inf))).astype(BF16))
        xp = xdt[:, q * LANES:(q + 1) * LANES]
        rhs = jnp.concatenate([jnp.where(lane_lo, xp, 0.0).astype(BF16),
                               jnp.where(lane_lo, 0.0, xp).astype(BF16)], axis=0)
        ys.append(jnp.dot(jnp.concatenate(scs, axis=1), rhs, preferred_element_type=F32))
    y_intra = jnp.concatenate(ys, axis=1)
    xw = (xdt * tex).astype(BF16)
    return xs, bmb, cmb, y_intra, eax, xw


def _gate_norm(y, z, nw):
    y = y * _silu(z)
    ms = jnp.mean(y * y, axis=-1, keepdims=True)
    return (y * lax.rsqrt(ms + RMS_EPS) * nw).astype(BF16)


def _chunk_iotas():
    li = lax.broadcasted_iota(jnp.int32, (CHUNK, CHUNK), 0)
    si = lax.broadcasted_iota(jnp.int32, (CHUNK, CHUNK), 1)
    return li, si


def _ssd_main_kernel(z_ref, xr_ref, br_ref, cr_ref, pk_acs_ref, pk3_ref, acst_ref, tott_ref, cw_ref, cbias_ref,
                     dx_ref, nw_ref, r_ref, e_ref, y_ref, sfin_ref, buf_ref, st_ref):
    del tott_ref
    g = pl.program_id(0)
    rb = pl.program_id(1)

    @pl.when(rb == 0)
    def _():
        st_ref[...] = jnp.zeros_like(st_ref)
        buf_ref[0:CONV_HEAD, :] = jnp.zeros((CONV_HEAD, XBC_W), F32)

    @pl.when(rb > 0)
    def _():
        buf_ref[0:CONV_HEAD, :] = buf_ref[SSD_RB:SSD_RB + CONV_HEAD, :]

    buf_ref[CONV_HEAD:, 0:GROUP_W] = xr_ref[...]
    buf_ref[CONV_HEAD:, GROUP_W:GROUP_W + STATE] = br_ref[...]
    buf_ref[CONV_HEAD:, GROUP_W + STATE:] = cr_ref[...]

    li, si = _chunk_iotas()
    causal = si <= li
    lane_lo = si < HEADS
    g8 = pl.multiple_of(g * HEADS_PER_GROUP, SUBLANES)
    cw = cw_ref[...]
    cbias = cbias_ref[...]

    for k in range(SSD_NB):
        rows = pl.ds(k * CHUNK, CHUNK)
        xbc = _conv_silu(buf_ref, CONV_HEAD + k * CHUNK, CHUNK, cw, cbias)
        arow = acst_ref[k, pl.ds(g8, HEADS_PER_GROUP), :]
        xs, bmb, cmb, y_intra, eax, xw = _intra_chunk(xbc, pk_acs_ref[rows, :], pk3_ref[rows, :], arow,
                                                      causal, lane_lo, r_ref, e_ref)
        st = st_ref[...]
        y = y_intra + jnp.dot(cmb, st.astype(BF16), preferred_element_type=F32) * eax + dx_ref[...] * xs
        y_ref[rows, :] = _gate_norm(y, z_ref[rows, :], nw_ref[...])
        st_ref[...] = st * eax[CHUNK - 1:CHUNK, :] + lax.dot_general(bmb, xw, _TN, preferred_element_type=F32)

    @pl.when(rb == T_MAIN // SSD_RB - 1)
    def _():
        sfin_ref[...] = st_ref[...]


def _ssd_samp_kernel(z_ref, xr_ref, br_ref, cr_ref, pk_acs_ref, pk3_ref, acst_ref, tott_ref, cw_ref, cbias_ref,
                     dx_ref, nw_ref, r_ref, e_ref, prev_ref, s0_ref, yprev_ref, y_ref, s1_ref,
                     buf_ref, xbc_ref, yint_ref):
    del yprev_ref
    g = pl.program_id(0)
    slot = CONV_HEAD + DEC_SEQ
    cw = cw_ref[...]
    cbias = cbias_ref[...]
    for q in range(SEQ_PER_CHUNK):
        r0 = q * DEC_SEQ
        top = q * slot + CONV_HEAD
        buf_ref[top - (SSD_CONV_WIDTH - 1):top, :] = prev_ref[q]
        buf_ref[top:top + DEC_SEQ, 0:GROUP_W] = xr_ref[r0:r0 + DEC_SEQ, :]
        buf_ref[top:top + DEC_SEQ, GROUP_W:GROUP_W + STATE] = br_ref[r0:r0 + DEC_SEQ, :]
        buf_ref[top:top + DEC_SEQ, GROUP_W + STATE:] = cr_ref[r0:r0 + DEC_SEQ, :]
    for q in range(SEQ_PER_CHUNK):
        r0 = q * DEC_SEQ
        xbc_ref[r0:r0 + DEC_SEQ, :] = _conv_silu(buf_ref, q * slot + CONV_HEAD, DEC_SEQ, cw, cbias)

    li, si = _chunk_iotas()
    same = (li // DEC_SEQ) == (si // DEC_SEQ)
    causal = jnp.logical_and(si <= li, same)
    lane_lo = si < HEADS
    g8 = pl.multiple_of(g * HEADS_PER_GROUP, SUBLANES)

    arow = acst_ref[0, pl.ds(g8, HEADS_PER_GROUP), :]
    trow = tott_ref[0, pl.ds(g8, HEADS_PER_GROUP), :]
    xs, bmb, cmb, y_intra, eax, xw = _intra_chunk(xbc_ref[...], pk_acs_ref[...], pk3_ref[...], arow,
                                                  causal, lane_lo, r_ref, e_ref)
    for q in range(SEQ_PER_CHUNK):
        r0 = q * DEC_SEQ
        yint_ref[r0:r0 + DEC_SEQ, :] = lax.dot_general(cmb[r0:r0 + DEC_SEQ, :], s0_ref[q].astype(BF16), _NT,
                                                       preferred_element_type=F32)
    y = y_intra + yint_ref[...] * eax + dx_ref[...] * xs
    y_ref[...] = _gate_norm(y, z_ref[...], nw_ref[...])
    for q in range(SEQ_PER_CHUNK):
        r0 = q * DEC_SEQ
        upd = lax.dot_general(xw[r0:r0 + DEC_SEQ, :], bmb[r0:r0 + DEC_SEQ, :], _TN, preferred_element_type=F32)
        for r in range(HEADS_PER_GROUP):
            lo, hi = r * HEADDIM, (r + 1) * HEADDIM
            s1_ref[q, lo:hi, :] = s0_ref[q, lo:hi, :] * jnp.exp(trow[r:r + 1, r0:r0 + 1]) + upd[lo:hi, :]


def _ssd_specs(rows, row0):
    xcol = D_INNER // GROUP_W
    bcol = (2 * D_INNER) // STATE
    return [
        pl.BlockSpec((rows, GROUP_W), lambda g, c: (row0 + c, g)),
        pl.BlockSpec((rows, GROUP_W), lambda g, c: (row0 + c, xcol + g)),
        pl.BlockSpec((rows, STATE), lambda g, c: (row0 + c, bcol + g)),
        pl.BlockSpec((rows, STATE), lambda g, c: (row0 + c, bcol + GROUPS + g)),
        pl.BlockSpec((rows, SPLIT_K), lambda g, c: (row0 + c, 0)),
        pl.BlockSpec((rows, 3 * LANES), lambda g, c: (row0 + c, 0)),
        pl.BlockSpec((rows // CHUNK, LANES, CHUNK), lambda g, c: (row0 + c, 0, 0)),
        pl.BlockSpec((rows // CHUNK, LANES, CHUNK), lambda g, c: (row0 + c, 0, 0)),
        pl.BlockSpec((None, SSD_CONV_WIDTH, XBC_W), lambda g, c: (g, 0, 0)),
        pl.BlockSpec((None, 1, XBC_W), lambda g, c: (g, 0, 0)),
        pl.BlockSpec((1, GROUP_W), lambda g, c: (0, g)),
        pl.BlockSpec((1, GROUP_W), lambda g, c: (0, g)),
        pl.BlockSpec((None, SPLIT_K, HEADS_PER_GROUP * CHUNK), lambda g, c: (g, 0, 0)),
        pl.BlockSpec((None, LANES, GROUP_W), lambda g, c: (g, 0, 0)),
    ]


def _spread_matrices():
    k = np.arange(SPLIT_K)
    head = k % HEADS
    used = (k // HEADS) < 3
    g = np.arange(GROUPS)[:, None, None]
    rj = np.arange(HEADS_PER_GROUP * CHUNK)[None, None, :] // CHUNK
    ej = np.arange(GROUP_W)[None, None, :] // HEADDIM
    hk = head[None, :, None]
    uk = used[None, :, None]
    r = (uk & (hk == g * HEADS_PER_GROUP + rj)).astype(np.float32)
    e = (hk == g * HEADS_PER_GROUP + ej)[:, :LANES].astype(np.float32)
    return jnp.asarray(r, BF16), jnp.asarray(e, BF16)


def _per_group(a):
    lead = a.shape[:-1]
    x = a[..., :D_INNER].reshape(*lead, GROUPS, GROUP_W)
    b = a[..., D_INNER:D_INNER + GROUPS * STATE].reshape(*lead, GROUPS, STATE)
    c = a[..., D_INNER + GROUPS * STATE:].reshape(*lead, GROUPS, STATE)
    return jnp.moveaxis(jnp.concatenate([x, b, c], axis=-1), -2, 0)


def _ssd(zxd, xb, wt, conv_w, conv_b, dt_bias, a_log, d_skip, norm_w, conv_prev, state0):
    r_mat, e_mat = _spread_matrices()
    pad = ((0, 0), (0, LANES - HEADS))
    prep = _ssd_prep(xb, wt, jnp.pad(dt_bias.reshape(1, HEADS).astype(F32), pad),
                     jnp.pad(a_log.reshape(1, HEADS).astype(F32), pad))
    params = (_per_group(conv_w), _per_group(conv_b.reshape(1, CONV_DIM)),
              jnp.repeat(d_skip.astype(F32), HEADDIM).reshape(1, D_INNER),
              norm_w.reshape(1, D_INNER).astype(F32), r_mat, e_mat)
    common = (zxd,) * 4 + tuple(prep) + params
    y, s_fin = pl.pallas_call(
        _ssd_main_kernel,
        grid=(GROUPS, T_MAIN // SSD_RB),
        in_specs=_ssd_specs(SSD_RB, 0),
        out_specs=[
            pl.BlockSpec((SSD_RB, GROUP_W), lambda g, c: (c, g)),
            pl.BlockSpec((None, STATE, GROUP_W), lambda g, c: (g, 0, 0)),
        ],
        out_shape=[
            jax.ShapeDtypeStruct((T_ALL, D_INNER), BF16),
            jax.ShapeDtypeStruct((GROUPS, STATE, GROUP_W), F32),
        ],
        scratch_shapes=[
            pltpu.VMEM((CONV_HEAD + SSD_RB, XBC_W), F32),
            pltpu.VMEM((STATE, GROUP_W), F32),
        ],
        compiler_params=_cparams(("parallel", "arbitrary")),
        name="ssd_prompt",
    )(*common)

    n_in = len(common)
    y, s_new = pl.pallas_call(
        _ssd_samp_kernel,
        grid=(GROUPS, N_SAMP_CHUNKS),
        in_specs=_ssd_specs(CHUNK, N_MAIN_CHUNKS) + [
            pl.BlockSpec((None, SEQ_PER_CHUNK, SSD_CONV_WIDTH - 1, XBC_W), lambda g, c: (g, c, 0, 0)),
            pl.BlockSpec((SEQ_PER_CHUNK, None, GROUP_W, STATE), lambda g, c: (c, g, 0, 0)),
            pl.BlockSpec(memory_space=pl.ANY),
        ],
        out_specs=[
            pl.BlockSpec((CHUNK, GROUP_W), lambda g, c: (N_MAIN_CHUNKS + c, g)),
            pl.BlockSpec((SEQ_PER_CHUNK, None, GROUP_W, STATE), lambda g, c: (c, g, 0, 0)),
        ],
        out_shape=[
            jax.ShapeDtypeStruct((T_ALL, D_INNER), BF16),
            jax.ShapeDtypeStruct((DEC_BATCH, GROUPS, GROUP_W, STATE), F32),
        ],
        scratch_shapes=[
            pltpu.VMEM((SEQ_PER_CHUNK * (CONV_HEAD + DEC_SEQ), XBC_W), F32),
            pltpu.VMEM((CHUNK, XBC_W), F32),
            pltpu.VMEM((CHUNK, GROUP_W), F32),
        ],
        input_output_aliases={n_in + 2: 0},
        compiler_params=_cparams(("parallel", "arbitrary")),
        name="ssd_sample",
    )(*common, _per_group(conv_prev), state0, y)
    return y, s_fin, s_new


def _tail_rows(a, n):
    c = a.shape[-1]
    p = a[T_PROMPT - n:T_PROMPT].reshape(1, 1, n, c)
    s = a[T_MAIN:].reshape(DEC_BATCH, DEC_SEQ, c)[:, DEC_SEQ - n:].reshape(1, DEC_BATCH, n, c)
    return p, s


def kernel(x_prompt, x_sample, cache_sc_conv, state_ssd_conv, state_ssd, meta_tokens, ln_g, ln_b,
           ffn_w1, ffn_w3, ffn_w2, sc_w_in, sc_w_conv, sc_w_out,
           ssd_w_in, ssd_w_conv, ssd_b_conv, ssd_dt_bias, ssd_a_log, ssd_d, ssd_norm_w, ssd_w_out):
    w1, w3, w2 = ffn_w1, ffn_w3, ffn_w2
    sc_out_w = sc_w_out.astype(BF16)
    ssd_out_w = ssd_w_out.astype(BF16)
    lng = ln_g.reshape(DEPTH, 3, 1, D_MODEL)
    lnb = ln_b.reshape(DEPTH, 3, 1, D_MODEL)

    h = jnp.concatenate([meta_tokens.astype(F32), x_prompt[0], jnp.zeros((N_PAD, D_MODEL), F32),
                         x_sample.reshape(N_SAMP, D_MODEL)], axis=0)

    h, hb = _ffn(h, w1, w3, w2, lng, lnb, 0, 0, 0, emit_bf16=True)
    v, u_tail = _sc_mix(hb, sc_w_in, 0, sc_w_conv[0], cache_sc_conv[0])
    keep = SC_WIDTH - 1
    new_sc_p = u_tail[T_PROMPT - keep - SC_TAIL0:T_PROMPT - SC_TAIL0].reshape(1, 1, keep, D_MODEL)
    new_sc_s = u_tail[SC_SAMP_OFF:].reshape(DEC_BATCH, DEC_SEQ, D_MODEL)[:, DEC_SEQ - keep:].reshape(
        1, DEC_BATCH, keep, D_MODEL)
    h = _proj_ln(v, sc_out_w, 0, h, lng, lnb, 0, 1)
    h = _ffn(h, w1, w3, w2, lng, lnb, 0, 1, 2)

    h, hb = _ffn(h, w1, w3, w2, lng, lnb, 1, 0, 0, emit_bf16=True)
    ssd_wt = jnp.swapaxes(ssd_w_in[0], 0, 1)
    zxd = _mm(hb, ssd_wt, ZXD_W, ZXD_TN)
    new_conv_p, new_conv_s = _tail_rows(zxd[:, D_INNER:DT_COL], SSD_CONV_WIDTH - 1)
    state0 = state_ssd[0].reshape(DEC_BATCH, GROUPS, GROUP_W, STATE)
    y, s_fin, s_new = _ssd(zxd, hb, ssd_wt, ssd_w_conv[0], ssd_b_conv[0], ssd_dt_bias[0], ssd_a_log[0],
                           ssd_d[0], ssd_norm_w[0], state_ssd_conv[0], state0)
    h = _proj_ln(y, ssd_out_w, 0, h, lng, lnb, 1, 1)
    h = _ffn(h, w1, w3, w2, lng, lnb, 1, 1, 2)

    y_prompt = h[N_META:T_PROMPT].reshape(1, SEQ, D_MODEL)
    y_sample = h[T_MAIN:].reshape(DEC_BATCH, DEC_SEQ, D_MODEL)
    new_ssd_p = jnp.swapaxes(s_fin, 1, 2).reshape(1, 1, HEADS, HEADDIM, STATE).astype(state_ssd.dtype)
    new_ssd_s = s_new.reshape(1, DEC_BATCH, HEADS, HEADDIM, STATE).astype(state_ssd.dtype)
    return (y_prompt, y_sample, new_sc_p, new_sc_s, new_conv_p, new_conv_s, new_ssd_p, new_ssd_s)
```

```python
import functools

import numpy as np

import jax
import jax.numpy as jnp
from jax import lax
from jax.experimental import pallas as pl
from jax.experimental.pallas import tpu as pltpu

F32 = jnp.float32
BF16 = jnp.bfloat16

D_MODEL = 2048
SEQ = 8192
DEPTH = 2
DEC_BATCH = 32
DEC_SEQ = 16
N_META = 16
D_FF = 5632
SC_WIDTH = 3
D_INNER = 4096
HEADDIM = 64
HEADS = 64
GROUPS = 8
HEADS_PER_GROUP = 8
STATE = 128
SSD_CONV_WIDTH = 4
CONV_DIM = D_INNER + 2 * GROUPS * STATE
GROUP_W = HEADS_PER_GROUP * HEADDIM
XBC_W = GROUP_W + 2 * STATE
ALPHA = (2.0 * DEPTH) ** 0.25
LN_EPS = 1e-5
RMS_EPS = 1e-5

CHUNK = 128
T_PROMPT = N_META + SEQ
N_MAIN_CHUNKS = -(-T_PROMPT // CHUNK)
T_MAIN = N_MAIN_CHUNKS * CHUNK
N_PAD = T_MAIN - T_PROMPT
N_SAMP = DEC_BATCH * DEC_SEQ
N_SAMP_CHUNKS = N_SAMP // CHUNK
SEQ_PER_CHUNK = CHUNK // DEC_SEQ
T_ALL = T_MAIN + N_SAMP

TM = 736
TM_IN = 2 * TM
TF = 512
TF_HEAD = 256
TM_WIDE = 1104
FFN_SLAB = 368
DT_COL = D_INNER + CONV_DIM
ZXD_W = DT_COL
ZXD_TN = 1280
SSD_NB = 13
SSD_RB = SSD_NB * CHUNK
PREP_NB = 3
LANES = 128
SUBLANES = 8
CONV_HEAD = SUBLANES

VMEM_LIMIT_BYTES = 56 * 1024 * 1024
VMEM_LIMIT_WIDE_BYTES = 61 * 1024 * 1024
VMEM_LIMIT_MAX_BYTES = 63 * 1024 * 1024 + 768 * 1024


def _cparams(sem, vmem_limit_bytes=VMEM_LIMIT_BYTES):
    return pltpu.CompilerParams(dimension_semantics=sem, vmem_limit_bytes=vmem_limit_bytes)


def _layer_norm(v, g, b):
    mu = jnp.mean(v, axis=-1, keepdims=True)
    c = v - mu
    var = jnp.mean(c * c, axis=-1, keepdims=True)
    return c * lax.rsqrt(var + LN_EPS) * g + b


def _silu(x):
    h = 0.5 * x
    return h + h * jnp.tanh(h)


def _softplus(x):
    return jnp.maximum(x, 0.0) + jnp.log(1.0 + jnp.exp(-jnp.abs(x)))


def _ffn_kernel(*refs, nf, n_alias, emit_bf16, convert):
    x_ref, w1_ref, w3_ref, w2_ref, g_ref, b_ref = refs[:6]
    outs = refs[6 + n_alias:]
    o_ref = outs[0]
    pos = 1
    ob_ref = None
    if emit_bf16:
        ob_ref = outs[pos]
        pos += 1
    if convert:
        wb_refs = outs[pos:pos + 3]
        for src, dst in zip((w1_ref, w3_ref, w2_ref), wb_refs):
            dst[...] = src[...].astype(BF16)
        w1_ref, w3_ref, w2_ref = wb_refs
        pos += 3
    xb_ref = outs[pos]
    f = pl.program_id(1)

    @pl.when(f == 0)
    def _():
        xb_ref[...] = x_ref[...].astype(BF16)
        o_ref[...] = jnp.zeros_like(o_ref)

    rs = FFN_SLAB
    n_slabs = x_ref.shape[0] // rs

    def slab(s, last):
        rows = pl.ds(s * rs, rs)
        xb = xb_ref[rows, :]
        h1 = jnp.dot(xb, w1_ref[...], preferred_element_type=F32)
        h3 = jnp.dot(xb, w3_ref[...], preferred_element_type=F32)
        gate = (_silu(h1) * h3).astype(BF16)
        acc = o_ref[rows, :] + jnp.dot(gate, w2_ref[...], preferred_element_type=F32)
        if last:
            res = _layer_norm(ALPHA * x_ref[rows, :] + 0.5 * acc, g_ref[...], b_ref[...])
            o_ref[rows, :] = res
            if ob_ref is not None:
                ob_ref[rows, :] = res.astype(BF16)
        else:
            o_ref[rows, :] = acc

    @pl.when(f < nf - 1)
    def _():
        for s in range(n_slabs):
            slab(s, False)

    @pl.when(f == nf - 1)
    def _():
        for s in range(n_slabs):
            slab(s, True)


def _ffn(h, w1, w3, w2, lng, lnb, i, j, k, emit_bf16=False):
    tm, vmem = TM_WIDE, (VMEM_LIMIT_MAX_BYTES if emit_bf16 else VMEM_LIMIT_WIDE_BYTES)
    ln_spec = pl.BlockSpec((None, None, 1, D_MODEL), lambda m, f: (i, k, 0, 0))
    h_shapes = [jax.ShapeDtypeStruct((T_ALL, D_MODEL), F32)]
    if emit_bf16:
        h_shapes.append(jax.ShapeDtypeStruct((T_ALL, D_MODEL), BF16))
    n_h = len(h_shapes)
    scratch = [pltpu.VMEM((tm, D_MODEL), BF16)]

    head_row = pl.BlockSpec((tm, D_MODEL), lambda m, f: (0, 0))
    nf = D_FF // TF
    tf_head = TF_HEAD
    nf_head = D_FF // tf_head
    per = TF // tf_head
    head = pl.pallas_call(
        functools.partial(_ffn_kernel, nf=nf_head, n_alias=0, emit_bf16=emit_bf16, convert=True),
        grid=(1, nf_head),
        in_specs=[
            head_row,
            pl.BlockSpec((None, None, D_MODEL, tf_head), lambda m, f: (i, j, 0, f)),
            pl.BlockSpec((None, None, D_MODEL, tf_head), lambda m, f: (i, j, 0, f)),
            pl.BlockSpec((None, None, tf_head, D_MODEL), lambda m, f: (i, j, f, 0)),
            ln_spec, ln_spec,
        ],
        out_specs=[head_row] * n_h + [
            pl.BlockSpec((None, D_MODEL, tf_head), lambda m, f: (f // per, 0, f % per)),
            pl.BlockSpec((None, D_MODEL, tf_head), lambda m, f: (f // per, 0, f % per)),
            pl.BlockSpec((tf_head, D_MODEL), lambda m, f: (f, 0)),
        ],
        out_shape=h_shapes + [
            jax.ShapeDtypeStruct((nf, D_MODEL, TF), BF16),
            jax.ShapeDtypeStruct((nf, D_MODEL, TF), BF16),
            jax.ShapeDtypeStruct((D_FF, D_MODEL), BF16),
        ],
        scratch_shapes=scratch,
        compiler_params=_cparams(("parallel", "arbitrary"), VMEM_LIMIT_WIDE_BYTES),
        name="ffn_ln_head",
    )(h, w1, w3, w2, lng, lnb)
    h_parts, (w1b, w3b, w2b) = head[:n_h], head[n_h:]

    row = pl.BlockSpec((tm, D_MODEL), lambda m, f: (m + 1, 0))
    out = pl.pallas_call(
        functools.partial(_ffn_kernel, nf=nf, n_alias=n_h, emit_bf16=emit_bf16, convert=False),
        grid=(T_ALL // tm - 1, nf),
        in_specs=[
            row,
            pl.BlockSpec((None, D_MODEL, TF), lambda m, f: (f, 0, 0)),
            pl.BlockSpec((None, D_MODEL, TF), lambda m, f: (f, 0, 0)),
            pl.BlockSpec((TF, D_MODEL), lambda m, f: (f, 0)),
            ln_spec, ln_spec,
        ] + [pl.BlockSpec(memory_space=pl.ANY)] * n_h,
        out_specs=[row] * n_h,
        out_shape=h_shapes,
        scratch_shapes=scratch,
        input_output_aliases={6 + a: a for a in range(n_h)},
        compiler_params=_cparams(("parallel", "arbitrary"), vmem),
        name="ffn_ln",
    )(h, w1b, w3b, w2b, lng, lnb, *h_parts)
    return tuple(out) if emit_bf16 else out[0]


def _mm_kernel(x_ref, wt_ref, o_ref, wb_ref):
    @pl.when(pl.program_id(1) == 0)
    def _():
        wb_ref[...] = wt_ref[...].astype(BF16)

    o_ref[...] = lax.dot_general(x_ref[...], wb_ref[...], (((1,), (1,)), ((), ())),
                                 preferred_element_type=F32)


def _mm(xb, wt, n_out, tn):
    n, k = wt.shape
    assert n_out % tn == 0 and n_out <= n
    return pl.pallas_call(
        _mm_kernel,
        grid=(n_out // tn, T_ALL // TM_IN),
        in_specs=[
            pl.BlockSpec((TM_IN, k), lambda j, m: (m, 0)),
            pl.BlockSpec((tn, k), lambda j, m: (j, 0)),
        ],
        out_specs=pl.BlockSpec((TM_IN, tn), lambda j, m: (m, j)),
        out_shape=jax.ShapeDtypeStruct((T_ALL, n_out), F32),
        scratch_shapes=[pltpu.VMEM((tn, k), BF16)],
        compiler_params=_cparams(("parallel", "arbitrary"), VMEM_LIMIT_WIDE_BYTES),
        name="in_proj",
    )(xb, wt)


TM_SC = TM_WIDE
SC_TAIL0 = T_ALL - TM_SC
SC_SAMP_OFF = T_MAIN - SC_TAIL0
assert SC_SAMP_OFF % DEC_SEQ == 0 and T_PROMPT - (SC_WIDTH - 1) >= SC_TAIL0


def _sc_mix_kernel(x_ref, wb_ref, wc_ref, wh_ref, cw_ref, o1_ref, o2_ref, v_ref, ut_ref,
                   wbb_ref, wcb_ref, whb_ref, ubuf_ref, *, nm):
    m = pl.program_id(1)
    tn = ubuf_ref.shape[1]

    @pl.when(m == 0)
    def _():
        wbb_ref[...] = wb_ref[...].astype(BF16)
        wcb_ref[...] = wc_ref[...].astype(BF16)
        whb_ref[...] = wh_ref[...].astype(BF16)
        ubuf_ref[0:CONV_HEAD, :] = jnp.zeros((CONV_HEAD, tn), F32)

    cw = cw_ref[...]
    ov_refs = (o1_ref, o2_ref)
    rs = FFN_SLAB
    n_slabs = TM_SC // rs

    def slab(s, last_tile):
        r0 = s * rs
        xb = x_ref[r0:r0 + rs, :]
        bg = jnp.dot(xb, wbb_ref[...], preferred_element_type=F32)
        c = jnp.dot(xb, wcb_ref[...], preferred_element_type=F32)
        hh = jnp.dot(xb, whb_ref[...], preferred_element_type=F32)
        u = c * hh
        ubuf_ref[CONV_HEAD + r0:CONV_HEAD + r0 + rs, :] = u
        acc = cw[SC_WIDTH - 1:SC_WIDTH, :] * u
        if last_tile:
            lrow = lax.broadcasted_iota(jnp.int32, (rs, 1), 0) + r0
            spos = jnp.bitwise_and(lrow, DEC_SEQ - 1)
        for d in range(1, SC_WIDTH):
            ud = ubuf_ref[CONV_HEAD + r0 - d:CONV_HEAD + r0 - d + rs, :]
            if last_tile:
                first = jnp.logical_and(lrow >= SC_SAMP_OFF, spos < d)
                ud = jnp.where(first, ov_refs[d - 1][r0:r0 + rs, :], ud)
            acc = acc + cw[SC_WIDTH - 1 - d:SC_WIDTH - d, :] * ud
        v_ref[r0:r0 + rs, :] = (bg * acc).astype(BF16)
        if last_tile:
            ut_ref[r0:r0 + rs, :] = u

    @pl.when(m < nm - 1)
    def _():
        for s in range(n_slabs):
            slab(s, False)

    @pl.when(m == nm - 1)
    def _():
        for s in range(n_slabs):
            slab(s, True)

    ubuf_ref[0:CONV_HEAD, :] = ubuf_ref[TM_SC:TM_SC + CONV_HEAD, :]


def _sc_mix(xb, w_in, j, w_conv, conv_prev):
    tn = 512
    nb = D_MODEL // tn
    nm = T_ALL // TM_SC
    ovs = [jnp.pad(o, ((SC_SAMP_OFF, 0), (0, 0))) for o in _conv_overrides(conv_prev, SC_WIDTH)]
    tail = pl.BlockSpec((TM_SC, tn), lambda n, m: (0, n))
    wscratch = pltpu.VMEM((D_MODEL, tn), BF16)
    return pl.pallas_call(
        functools.partial(_sc_mix_kernel, nm=nm),
        grid=(nb, nm),
        in_specs=[
            pl.BlockSpec((TM_SC, D_MODEL), lambda n, m: (m, 0)),
            pl.BlockSpec((None, D_MODEL, tn), lambda n, m: (j, 0, n)),
            pl.BlockSpec((None, D_MODEL, tn), lambda n, m: (j, 0, n + nb)),
            pl.BlockSpec((None, D_MODEL, tn), lambda n, m: (j, 0, n + 2 * nb)),
            pl.BlockSpec((SC_WIDTH, tn), lambda n, m: (0, n)),
            tail, tail,
        ],
        out_specs=[pl.BlockSpec((TM_SC, tn), lambda n, m: (m, n)), tail],
        out_shape=[jax.ShapeDtypeStruct((T_ALL, D_MODEL), BF16),
                   jax.ShapeDtypeStruct((TM_SC, D_MODEL), F32)],
        scratch_shapes=[wscratch, wscratch, wscratch, pltpu.VMEM((CONV_HEAD + TM_SC, tn), F32)],
        compiler_params=_cparams(("parallel", "arbitrary"), VMEM_LIMIT_WIDE_BYTES),
        name="sc_mix",
    )(xb, w_in, w_in, w_in, w_conv.astype(F32), *ovs)


def _conv_overrides(prev, taps):
    c = prev.shape[-1]
    out = []
    for d in range(1, taps):
        o = jnp.zeros((DEC_BATCH, DEC_SEQ, c), F32)
        for p in range(d):
            o = o.at[:, p].set(prev[:, taps - 1 + p - d])
        out.append(o.reshape(N_SAMP, c))
    return out


def _proj_ln_kernel(y_ref, w_ref, h_ref, g_ref, b_ref, o_ref, *, nk):
    rs = FFN_SLAB

    def slab_dot(rows):
        return jnp.dot(y_ref[rows, :], w_ref[...], preferred_element_type=F32)

    def finish(rows, acc):
        v = ALPHA * h_ref[rows, :] + acc
        o_ref[rows, :] = _layer_norm(v, g_ref[...], b_ref[...])

    slabs = [pl.ds(s * rs, rs) for s in range(y_ref.shape[0] // rs)]
    if nk == 1:
        for rows in slabs:
            finish(rows, slab_dot(rows))
        return

    k = pl.program_id(1)

    @pl.when(k == 0)
    def _():
        for rows in slabs:
            o_ref[rows, :] = slab_dot(rows)

    @pl.when(k == 1)
    def _():
        for rows in slabs:
            finish(rows, o_ref[rows, :] + slab_dot(rows))


def _proj_ln(y, w, j, h, lng, lnb, i, k):
    kdim = w.shape[1]
    tk = 2048
    nk = kdim // tk
    assert nk in (1, 2)
    return pl.pallas_call(
        functools.partial(_proj_ln_kernel, nk=nk),
        grid=(T_ALL // TM, nk),
        in_specs=[
            pl.BlockSpec((TM, tk), lambda m, kk: (m, kk)),
            pl.BlockSpec((None, tk, D_MODEL), lambda m, kk: (j, kk, 0)),
            pl.BlockSpec((TM, D_MODEL), lambda m, kk: (m, 0)),
            pl.BlockSpec((None, None, 1, D_MODEL), lambda m, kk: (i, k, 0, 0)),
            pl.BlockSpec((None, None, 1, D_MODEL), lambda m, kk: (i, k, 0, 0)),
        ],
        out_specs=pl.BlockSpec((TM, D_MODEL), lambda m, kk: (m, 0)),
        out_shape=jax.ShapeDtypeStruct((T_ALL, D_MODEL), F32),
        compiler_params=_cparams(("parallel", "arbitrary")),
        name="out_proj_ln",
    )(y, w, h, lng, lnb)


_NT = (((1,), (1,)), ((), ()))
_TN = (((0,), (0,)), ((), ()))
SPLIT_K = 2 * LANES


def _bf16_pieces(v):
    hi = v.astype(BF16)
    r1 = v - hi.astype(F32)
    mid = r1.astype(BF16)
    lo = (r1 - mid.astype(F32)).astype(BF16)
    return hi, mid, lo


def _split_pack(v, lane_lo):
    hi, mid, lo = _bf16_pieces(v)
    a = jnp.where(lane_lo, hi.astype(F32), pltpu.roll(mid.astype(F32), HEADS, axis=1))
    b = jnp.where(lane_lo, lo.astype(F32), 0.0)
    return jnp.concatenate([a, b], axis=1).astype(BF16)


def _split_pack2(v, lane_lo):
    hi = v.astype(BF16)
    mid = (v - hi.astype(F32)).astype(BF16)
    return jnp.where(lane_lo, hi.astype(F32), pltpu.roll(mid.astype(F32), HEADS, axis=1)).astype(BF16)


def _masked_cumsum(mask_bf, da):
    hi, mid, lo = _bf16_pieces(da)
    p = jnp.dot(mask_bf, jnp.concatenate([hi, mid, lo], axis=1), preferred_element_type=F32)
    return p[:, :LANES] + p[:, LANES:2 * LANES] + p[:, 2 * LANES:]


def _conv_silu(buf_ref, base, n, cw, cbias):
    taps = SSD_CONV_WIDTH
    acc = cw[taps - 1:taps, :] * buf_ref[base:base + n, :]
    for d in range(1, taps):
        acc = acc + cw[taps - 1 - d:taps - d, :] * buf_ref[base - d:base - d + n, :]
    return _silu(acc + cbias)


def _ssd_prep_kernel(x_ref, wdt_ref, dtb_ref, al_ref, pk_acs_ref, pk3_ref, acst_ref, tott_ref):
    li, si = _chunk_iotas()
    lane_lo = si < HEADS
    same_seq = (li // DEC_SEQ) == (si // DEC_SEQ)
    a_row = -jnp.exp(al_ref[...])
    wdt = wdt_ref[...].astype(BF16)
    wdt = jnp.concatenate([wdt, jnp.zeros_like(wdt)], axis=0)
    dt_raw = lax.dot_general(x_ref[...], wdt, _NT, preferred_element_type=F32)
    for k in range(PREP_NB):
        c = pl.program_id(0) * PREP_NB + k
        rows = pl.ds(k * CHUNK, CHUNK)
        same = jnp.logical_or(c < N_MAIN_CHUNKS, same_seq)
        causal = jnp.logical_and(si <= li, same)
        row = lax.broadcasted_iota(jnp.int32, (CHUNK, 1), 0) + c * CHUNK
        valid = jnp.logical_or(row < T_PROMPT, row >= T_MAIN)
        dt = jnp.where(jnp.logical_and(valid, lane_lo),
                       _softplus(dt_raw[k * CHUNK:(k + 1) * CHUNK, :] + dtb_ref[...]), 0.0)
        da = dt * a_row
        acs = _masked_cumsum(causal.astype(F32).astype(BF16), da)
        tot = _masked_cumsum(same.astype(F32).astype(BF16), da)
        pk_acs_ref[rows, :] = _split_pack(acs, lane_lo)
        pk3_ref[rows, :] = jnp.concatenate([_split_pack2(dt, lane_lo), _split_pack2(jnp.exp(tot - acs), lane_lo),
                                            _split_pack2(jnp.exp(acs), lane_lo)], axis=1)
        acst_ref[k] = acs.T
        tott_ref[k] = tot.T


def _ssd_prep(xb, wt, dt_bias, a_log):
    n_chunks = T_ALL // CHUNK
    rows = PREP_NB * CHUNK
    tr = pl.BlockSpec((PREP_NB, LANES, CHUNK), lambda c: (c, 0, 0))
    return pl.pallas_call(
        _ssd_prep_kernel,
        grid=(n_chunks // PREP_NB,),
        in_specs=[
            pl.BlockSpec((rows, D_MODEL), lambda c: (c, 0)),
            pl.BlockSpec((HEADS, D_MODEL), lambda c: (DT_COL // HEADS, 0)),
            pl.BlockSpec((1, LANES), lambda c: (0, 0)),
            pl.BlockSpec((1, LANES), lambda c: (0, 0)),
        ],
        out_specs=[
            pl.BlockSpec((rows, SPLIT_K), lambda c: (c, 0)),
            pl.BlockSpec((rows, 3 * LANES), lambda c: (c, 0)),
            tr, tr,
        ],
        out_shape=[
            jax.ShapeDtypeStruct((T_ALL, SPLIT_K), BF16),
            jax.ShapeDtypeStruct((T_ALL, 3 * LANES), BF16),
            jax.ShapeDtypeStruct((n_chunks, LANES, CHUNK), F32),
            jax.ShapeDtypeStruct((n_chunks, LANES, CHUNK), F32),
        ],
        compiler_params=_cparams(("parallel",)),
        name="ssd_prep",
    )(xb, wt, dt_bias, a_log)


def _intra_chunk(xbc, pk_acs, pk3, arow, causal, lane_lo, r_ref, e_ref):
    xs = xbc[:, :GROUP_W]
    bmb = xbc[:, GROUP_W:GROUP_W + STATE].astype(BF16)
    cmb = xbc[:, GROUP_W + STATE:].astype(BF16)
    colmat = jnp.dot(pk_acs, r_ref[...], preferred_element_type=F32)
    packed = jnp.concatenate([pk3[:, :LANES], pk3[:, LANES:2 * LANES], pk3[:, 2 * LANES:]], axis=0)
    ex = jnp.dot(packed, e_ref[...], preferred_element_type=F32)
    dtx, tex, eax = ex[:CHUNK], ex[CHUNK:2 * CHUNK], ex[2 * CHUNK:]
    xdt = xs * dtx
    cb = lax.dot_general(cmb, bmb, _NT, preferred_element_type=F32)
    ys = []
    for q in range(HEADS_PER_GROUP // 2):
        scs = []
        for r in (2 * q, 2 * q + 1):
            seg = colmat[:, r * CHUNK:(r + 1) * CHUNK] - arow[r:r + 1, :]
            scs.append((cb * jnp.exp(jnp.where(causal, seg, -jnp.inf))).astype(BF16))
        xp = xdt[:, q * LANES:(q + 1) * LANES]
        rhs = jnp.concatenate([jnp.where(lane_lo, xp, 0.0).astype(BF16),
                               jnp.where(lane_lo, 0.0, xp).astype(BF16)], axis=0)
        ys.append(jnp.dot(jnp.concatenate(scs, axis=1), rhs, preferred_element_type=F32))
    y_intra = jnp.concatenate(ys, axis=1)
    xw = (xdt * tex).astype(BF16)
    return xs, bmb, cmb, y_intra, eax, xw


def _gate_norm(y, z, nw):
    y = y * _silu(z)
    ms = jnp.mean(y * y, axis=-1, keepdims=True)
    return (y * lax.rsqrt(ms + RMS_EPS) * nw).astype(BF16)


def _chunk_iotas():
    li = lax.broadcasted_iota(jnp.int32, (CHUNK, CHUNK), 0)
    si = lax.broadcasted_iota(jnp.int32, (CHUNK, CHUNK), 1)
    return li, si


def _ssd_main_kernel(z_ref, xr_ref, br_ref, cr_ref, pk_acs_ref, pk3_ref, acst_ref, tott_ref, cw_ref, cbias_ref,
                     dx_ref, nw_ref, r_ref, e_ref, y_ref, sfin_ref, buf_ref, st_ref):
    del tott_ref
    g = pl.program_id(0)
    rb = pl.program_id(1)

    @pl.when(rb == 0)
    def _():
        st_ref[...] = jnp.zeros_like(st_ref)
        buf_ref[0:CONV_HEAD, :] = jnp.zeros((CONV_HEAD, XBC_W), F32)

    @pl.when(rb > 0)
    def _():
        buf_ref[0:CONV_HEAD, :] = buf_ref[SSD_RB:SSD_RB + CONV_HEAD, :]

    buf_ref[CONV_HEAD:, 0:GROUP_W] = xr_ref[...]
    buf_ref[CONV_HEAD:, GROUP_W:GROUP_W + STATE] = br_ref[...]
    buf_ref[CONV_HEAD:, GROUP_W + STATE:] = cr_ref[...]

    li, si = _chunk_iotas()
    causal = si <= li
    lane_lo = si < HEADS
    g8 = pl.multiple_of(g * HEADS_PER_GROUP, SUBLANES)
    cw = cw_ref[...]
    cbias = cbias_ref[...]

    for k in range(SSD_NB):
        rows = pl.ds(k * CHUNK, CHUNK)
        xbc = _conv_silu(buf_ref, CONV_HEAD + k * CHUNK, CHUNK, cw, cbias)
        arow = acst_ref[k, pl.ds(g8, HEADS_PER_GROUP), :]
        xs, bmb, cmb, y_intra, eax, xw = _intra_chunk(xbc, pk_acs_ref[rows, :], pk3_ref[rows, :], arow,
                                                      causal, lane_lo, r_ref, e_ref)
        st = st_ref[...]
        y = y_intra + jnp.dot(cmb, st.astype(BF16), preferred_element_type=F32) * eax + dx_ref[...] * xs
        y_ref[rows, :] = _gate_norm(y, z_ref[rows, :], nw_ref[...])
        st_ref[...] = st * eax[CHUNK - 1:CHUNK, :] + lax.dot_general(bmb, xw, _TN, preferred_element_type=F32)

    @pl.when(rb == T_MAIN // SSD_RB - 1)
    def _():
        sfin_ref[...] = st_ref[...]


def _ssd_samp_kernel(z_ref, xr_ref, br_ref, cr_ref, pk_acs_ref, pk3_ref, acst_ref, tott_ref, cw_ref, cbias_ref,
                     dx_ref, nw_ref, r_ref, e_ref, prev_ref, s0_ref, yprev_ref, y_ref, s1_ref,
                     buf_ref, xbc_ref, yint_ref):
    del yprev_ref
    g = pl.program_id(0)
    slot = CONV_HEAD + DEC_SEQ
    cw = cw_ref[...]
    cbias = cbias_ref[...]
    for q in range(SEQ_PER_CHUNK):
        r0 = q * DEC_SEQ
        top = q * slot + CONV_HEAD
        buf_ref[top - (SSD_CONV_WIDTH - 1):top, :] = prev_ref[q]
        buf_ref[top:top + DEC_SEQ, 0:GROUP_W] = xr_ref[r0:r0 + DEC_SEQ, :]
        buf_ref[top:top + DEC_SEQ, GROUP_W:GROUP_W + STATE] = br_ref[r0:r0 + DEC_SEQ, :]
        buf_ref[top:top + DEC_SEQ, GROUP_W + STATE:] = cr_ref[r0:r0 + DEC_SEQ, :]
    for q in range(SEQ_PER_CHUNK):
        r0 = q * DEC_SEQ
        xbc_ref[r0:r0 + DEC_SEQ, :] = _conv_silu(buf_ref, q * slot + CONV_HEAD, DEC_SEQ, cw, cbias)

    li, si = _chunk_iotas()
    same = (li // DEC_SEQ) == (si // DEC_SEQ)
    causal = jnp.logical_and(si <= li, same)
    lane_lo = si < HEADS
    g8 = pl.multiple_of(g * HEADS_PER_GROUP, SUBLANES)

    arow = acst_ref[0, pl.ds(g8, HEADS_PER_GROUP), :]
    trow = tott_ref[0, pl.ds(g8, HEADS_PER_GROUP), :]
    xs, bmb, cmb, y_intra, eax, xw = _intra_chunk(xbc_ref[...], pk_acs_ref[...], pk3_ref[...], arow,
                                                  causal, lane_lo, r_ref, e_ref)
    for q in range(SEQ_PER_CHUNK):
        r0 = q * DEC_SEQ
        yint_ref[r0:r0 + DEC_SEQ, :] = lax.dot_general(cmb[r0:r0 + DEC_SEQ, :], s0_ref[q].astype(BF16), _NT,
                                                       preferred_element_type=F32)
    y = y_intra + yint_ref[...] * eax + dx_ref[...] * xs
    y_ref[...] = _gate_norm(y, z_ref[...], nw_ref[...])
    for q in range(SEQ_PER_CHUNK):
        r0 = q * DEC_SEQ
        upd = lax.dot_general(xw[r0:r0 + DEC_SEQ, :], bmb[r0:r0 + DEC_SEQ, :], _TN, preferred_element_type=F32)
        for r in range(HEADS_PER_GROUP):
            lo, hi = r * HEADDIM, (r + 1) * HEADDIM
            s1_ref[q, lo:hi, :] = s0_ref[q, lo:hi, :] * jnp.exp(trow[r:r + 1, r0:r0 + 1]) + upd[lo:hi, :]


def _ssd_specs(rows, row0):
    xcol = D_INNER // GROUP_W
    bcol = (2 * D_INNER) // STATE
    return [
        pl.BlockSpec((rows, GROUP_W), lambda g, c: (row0 + c, g)),
        pl.BlockSpec((rows, GROUP_W), lambda g, c: (row0 + c, xcol + g)),
        pl.BlockSpec((rows, STATE), lambda g, c: (row0 + c, bcol + g)),
        pl.BlockSpec((rows, STATE), lambda g, c: (row0 + c, bcol + GROUPS + g)),
        pl.BlockSpec((rows, SPLIT_K), lambda g, c: (row0 + c, 0)),
        pl.BlockSpec((rows, 3 * LANES), lambda g, c: (row0 + c, 0)),
        pl.BlockSpec((rows // CHUNK, LANES, CHUNK), lambda g, c: (row0 + c, 0, 0)),
        pl.BlockSpec((rows // CHUNK, LANES, CHUNK), lambda g, c: (row0 + c, 0, 0)),
        pl.BlockSpec((None, SSD_CONV_WIDTH, XBC_W), lambda g, c: (g, 0, 0)),
        pl.BlockSpec((None, 1, XBC_W), lambda g, c: (g, 0, 0)),
        pl.BlockSpec((1, GROUP_W), lambda g, c: (0, g)),
        pl.BlockSpec((1, GROUP_W), lambda g, c: (0, g)),
        pl.BlockSpec((None, SPLIT_K, HEADS_PER_GROUP * CHUNK), lambda g, c: (g, 0, 0)),
        pl.BlockSpec((None, LANES, GROUP_W), lambda g, c: (g, 0, 0)),
    ]


def _spread_matrices():
    k = np.arange(SPLIT_K)
    head = k % HEADS
    used = (k // HEADS) < 3
    g = np.arange(GROUPS)[:, None, None]
    rj = np.arange(HEADS_PER_GROUP * CHUNK)[None, None, :] // CHUNK
    ej = np.arange(GROUP_W)[None, None, :] // HEADDIM
    hk = head[None, :, None]
    uk = used[None, :, None]
    r = (uk & (hk == g * HEADS_PER_GROUP + rj)).astype(np.float32)
    e = (hk == g * HEADS_PER_GROUP + ej)[:, :LANES].astype(np.float32)
    return jnp.asarray(r, BF16), jnp.asarray(e, BF16)


def _per_group(a):
    lead = a.shape[:-1]
    x = a[..., :D_INNER].reshape(*lead, GROUPS, GROUP_W)
    b = a[..., D_INNER:D_INNER + GROUPS * STATE].reshape(*lead, GROUPS, STATE)
    c = a[..., D_INNER + GROUPS * STATE:].reshape(*lead, GROUPS, STATE)
    return jnp.moveaxis(jnp.concatenate([x, b, c], axis=-1), -2, 0)


def _ssd(zxd, xb, wt, conv_w, conv_b, dt_bias, a_log, d_skip, norm_w, conv_prev, state0):
    r_mat, e_mat = _spread_matrices()
    pad = ((0, 0), (0, LANES - HEADS))
    prep = _ssd_prep(xb, wt, jnp.pad(dt_bias.reshape(1, HEADS).astype(F32), pad),
                     jnp.pad(a_log.reshape(1, HEADS).astype(F32), pad))
    params = (_per_group(conv_w), _per_group(conv_b.reshape(1, CONV_DIM)),
              jnp.repeat(d_skip.astype(F32), HEADDIM).reshape(1, D_INNER),
              norm_w.reshape(1, D_INNER).astype(F32), r_mat, e_mat)
    common = (zxd,) * 4 + tuple(prep) + params
    y, s_fin = pl.pallas_call(
        _ssd_main_kernel,
        grid=(GROUPS, T_MAIN // SSD_RB),
        in_specs=_ssd_specs(SSD_RB, 0),
        out_specs=[
            pl.BlockSpec((SSD_RB, GROUP_W), lambda g, c: (c, g)),
            pl.BlockSpec((None, STATE, GROUP_W), lambda g, c: (g, 0, 0)),
        ],
        out_shape=[
            jax.ShapeDtypeStruct((T_ALL, D_INNER), BF16),
            jax.ShapeDtypeStruct((GROUPS, STATE, GROUP_W), F32),
        ],
        scratch_shapes=[
            pltpu.VMEM((CONV_HEAD + SSD_RB, XBC_W), F32),
            pltpu.VMEM((STATE, GROUP_W), F32),
        ],
        compiler_params=_cparams(("parallel", "arbitrary")),
        name="ssd_prompt",
    )(*common)

    n_in = len(common)
    y, s_new = pl.pallas_call(
        _ssd_samp_kernel,
        grid=(GROUPS, N_SAMP_CHUNKS),
        in_specs=_ssd_specs(CHUNK, N_MAIN_CHUNKS) + [
            pl.BlockSpec((None, SEQ_PER_CHUNK, SSD_CONV_WIDTH - 1, XBC_W), lambda g, c: (g, c, 0, 0)),
            pl.BlockSpec((SEQ_PER_CHUNK, None, GROUP_W, STATE), lambda g, c: (c, g, 0, 0)),
            pl.BlockSpec(memory_space=pl.ANY),
        ],
        out_specs=[
            pl.BlockSpec((CHUNK, GROUP_W), lambda g, c: (N_MAIN_CHUNKS + c, g)),
            pl.BlockSpec((SEQ_PER_CHUNK, None, GROUP_W, STATE), lambda g, c: (c, g, 0, 0)),
        ],
        out_shape=[
            jax.ShapeDtypeStruct((T_ALL, D_INNER), BF16),
            jax.ShapeDtypeStruct((DEC_BATCH, GROUPS, GROUP_W, STATE), F32),
        ],
        scratch_shapes=[
            pltpu.VMEM((SEQ_PER_CHUNK * (CONV_HEAD + DEC_SEQ), XBC_W), F32),
            pltpu.VMEM((CHUNK, XBC_W), F32),
            pltpu.VMEM((CHUNK, GROUP_W), F32),
        ],
        input_output_aliases={n_in + 2: 0},
        compiler_params=_cparams(("parallel", "arbitrary")),
        name="ssd_sample",
    )(*common, _per_group(conv_prev), state0, y)
    return y, s_fin, s_new


def _tail_rows(a, n):
    c = a.shape[-1]
    p = a[T_PROMPT - n:T_PROMPT].reshape(1, 1, n, c)
    s = a[T_MAIN:].reshape(DEC_BATCH, DEC_SEQ, c)[:, DEC_SEQ - n:].reshape(1, DEC_BATCH, n, c)
    return p, s


def kernel(x_prompt, x_sample, cache_sc_conv, state_ssd_conv, state_ssd, meta_tokens, ln_g, ln_b,
           ffn_w1, ffn_w3, ffn_w2, sc_w_in, sc_w_conv, sc_w_out,
           ssd_w_in, ssd_w_conv, ssd_b_conv, ssd_dt_bias, ssd_a_log, ssd_d, ssd_norm_w, ssd_w_out):
    w1, w3, w2 = ffn_w1, ffn_w3, ffn_w2
    sc_out_w = sc_w_out.astype(BF16)
    ssd_out_w = ssd_w_out.astype(BF16)
    lng = ln_g.reshape(DEPTH, 3, 1, D_MODEL)
    lnb = ln_b.reshape(DEPTH, 3, 1, D_MODEL)

    h = jnp.concatenate([meta_tokens.astype(F32), x_prompt[0], jnp.zeros((N_PAD, D_MODEL), F32),
                         x_sample.reshape(N_SAMP, D_MODEL)], axis=0)

    h, hb = _ffn(h, w1, w3, w2, lng, lnb, 0, 0, 0, emit_bf16=True)
    v, u_tail = _sc_mix(hb, sc_w_in, 0, sc_w_conv[0], cache_sc_conv[0])
    keep = SC_WIDTH - 1
    new_sc_p = u_tail[T_PROMPT - keep - SC_TAIL0:T_PROMPT - SC_TAIL0].reshape(1, 1, keep, D_MODEL)
    new_sc_s = u_tail[SC_SAMP_OFF:].reshape(DEC_BATCH, DEC_SEQ, D_MODEL)[:, DEC_SEQ - keep:].reshape(
        1, DEC_BATCH, keep, D_MODEL)
    h = _proj_ln(v, sc_out_w, 0, h, lng, lnb, 0, 1)
    h = _ffn(h, w1, w3, w2, lng, lnb, 0, 1, 2)

    h, hb = _ffn(h, w1, w3, w2, lng, lnb, 1, 0, 0, emit_bf16=True)
    ssd_wt = jnp.swapaxes(ssd_w_in[0], 0, 1)
    zxd = _mm(hb, ssd_wt, ZXD_W, ZXD_TN)
    new_conv_p, new_conv_s = _tail_rows(zxd[:, D_INNER:DT_COL], SSD_CONV_WIDTH - 1)
    state0 = state_ssd[0].reshape(DEC_BATCH, GROUPS, GROUP_W, STATE)
    y, s_fin, s_new = _ssd(zxd, hb, ssd_wt, ssd_w_conv[0], ssd_b_conv[0], ssd_dt_bias[0], ssd_a_log[0],
                           ssd_d[0], ssd_norm_w[0], state_ssd_conv[0], state0)
    h = _proj_ln(y, ssd_out_w, 0, h, lng, lnb, 1, 1)
    h = _ffn(h, w1, w3, w2, lng, lnb, 1, 1, 2)

    y_prompt = h[N_META:T_PROMPT].reshape(1, SEQ, D_MODEL)
    y_sample = h[T_MAIN:].reshape(DEC_BATCH, DEC_SEQ, D_MODEL)
    new_ssd_p = jnp.swapaxes(s_fin, 1, 2).reshape(1, 1, HEADS, HEADDIM, STATE).astype(state_ssd.dtype)
    new_ssd_s = s_new.reshape(1, DEC_BATCH, HEADS, HEADDIM, STATE).astype(state_ssd.dtype)
    return (y_prompt, y_sample, new_sc_p, new_sc_s, new_conv_p, new_conv_s, new_ssd_p, new_ssd_s)
```

```python
import functools

import numpy as np

import jax
import jax.numpy as jnp
from jax import lax
from jax.experimental import pallas as pl
from jax.experimental.pallas import tpu as pltpu

F32 = jnp.float32
BF16 = jnp.bfloat16

D_MODEL = 2048
SEQ = 8192
DEPTH = 2
DEC_BATCH = 32
DEC_SEQ = 16
N_META = 16
D_FF = 5632
SC_WIDTH = 3
D_INNER = 4096
HEADDIM = 64
HEADS = 64
GROUPS = 8
HEADS_PER_GROUP = 8
STATE = 128
SSD_CONV_WIDTH = 4
CONV_DIM = D_INNER + 2 * GROUPS * STATE
GROUP_W = HEADS_PER_GROUP * HEADDIM
XBC_W = GROUP_W + 2 * STATE
ALPHA = (2.0 * DEPTH) ** 0.25
LN_EPS = 1e-5
RMS_EPS = 1e-5

CHUNK = 128
T_PROMPT = N_META + SEQ
N_MAIN_CHUNKS = -(-T_PROMPT // CHUNK)
T_MAIN = N_MAIN_CHUNKS * CHUNK
N_PAD = T_MAIN - T_PROMPT
N_SAMP = DEC_BATCH * DEC_SEQ
N_SAMP_CHUNKS = N_SAMP // CHUNK
SEQ_PER_CHUNK = CHUNK // DEC_SEQ
T_ALL = T_MAIN + N_SAMP

TM = 736
TM_IN = 2 * TM
TF = 512
TF_HEAD = 256
TM_WIDE = 1104
FFN_SLAB = 368
DT_COL = D_INNER + CONV_DIM
ZXD_W = DT_COL
ZXD_TN = 1280
SSD_NB = 13
SSD_RB = SSD_NB * CHUNK
PREP_NB = 3
LANES = 128
SUBLANES = 8
CONV_HEAD = SUBLANES

VMEM_LIMIT_BYTES = 56 * 1024 * 1024
VMEM_LIMIT_WIDE_BYTES = 61 * 1024 * 1024
VMEM_LIMIT_MAX_BYTES = 63 * 1024 * 1024 + 768 * 1024


def _cparams(sem, vmem_limit_bytes=VMEM_LIMIT_BYTES):
    return pltpu.CompilerParams(dimension_semantics=sem, vmem_limit_bytes=vmem_limit_bytes)


def _layer_norm(v, g, b):
    mu = jnp.mean(v, axis=-1, keepdims=True)
    c = v - mu
    var = jnp.mean(c * c, axis=-1, keepdims=True)
    return c * lax.rsqrt(var + LN_EPS) * g + b


def _silu(x):
    h = 0.5 * x
    return h + h * jnp.tanh(h)


def _softplus(x):
    return jnp.maximum(x, 0.0) + jnp.log(1.0 + jnp.exp(-jnp.abs(x)))


def _ffn_kernel(*refs, nf, n_alias, emit_bf16, convert):
    x_ref, w1_ref, w3_ref, w2_ref, g_ref, b_ref = refs[:6]
    outs = refs[6 + n_alias:]
    o_ref = outs[0]
    pos = 1
    ob_ref = None
    if emit_bf16:
        ob_ref = outs[pos]
        pos += 1
    if convert:
        wb_refs = outs[pos:pos + 3]
        for src, dst in zip((w1_ref, w3_ref, w2_ref), wb_refs):
            dst[...] = src[...].astype(BF16)
        w1_ref, w3_ref, w2_ref = wb_refs
        pos += 3
    xb_ref = outs[pos]
    f = pl.program_id(1)

    @pl.when(f == 0)
    def _():
        xb_ref[...] = x_ref[...].astype(BF16)
        o_ref[...] = jnp.zeros_like(o_ref)

    rs = FFN_SLAB
    n_slabs = x_ref.shape[0] // rs

    def slab(s, last):
        rows = pl.ds(s * rs, rs)
        xb = xb_ref[rows, :]
        h1 = jnp.dot(xb, w1_ref[...], preferred_element_type=F32)
        h3 = jnp.dot(xb, w3_ref[...], preferred_element_type=F32)
        gate = (_silu(h1) * h3).astype(BF16)
        acc = o_ref[rows, :] + jnp.dot(gate, w2_ref[...], preferred_element_type=F32)
        if last:
            res = _layer_norm(ALPHA * x_ref[rows, :] + 0.5 * acc, g_ref[...], b_ref[...])
            o_ref[rows, :] = res
            if ob_ref is not None:
                ob_ref[rows, :] = res.astype(BF16)
        else:
            o_ref[rows, :] = acc

    @pl.when(f < nf - 1)
    def _():
        for s in range(n_slabs):
            slab(s, False)

    @pl.when(f == nf - 1)
    def _():
        for s in range(n_slabs):
            slab(s, True)


def _ffn(h, w1, w3, w2, lng, lnb, i, j, k, emit_bf16=False):
    tm, vmem = TM_WIDE, (VMEM_LIMIT_MAX_BYTES if emit_bf16 else VMEM_LIMIT_WIDE_BYTES)
    ln_spec = pl.BlockSpec((None, None, 1, D_MODEL), lambda m, f: (i, k, 0, 0))
    h_shapes = [jax.ShapeDtypeStruct((T_ALL, D_MODEL), F32)]
    if emit_bf16:
        h_shapes.append(jax.ShapeDtypeStruct((T_ALL, D_MODEL), BF16))
    n_h = len(h_shapes)
    scratch = [pltpu.VMEM((tm, D_MODEL), BF16)]

    head_row = pl.BlockSpec((tm, D_MODEL), lambda m, f: (0, 0))
    nf = D_FF // TF
    tf_head = TF_HEAD if emit_bf16 else TF
    nf_head = D_FF // tf_head
    per = TF // tf_head
    head = pl.pallas_call(
        functools.partial(_ffn_kernel, nf=nf_head, n_alias=0, emit_bf16=emit_bf16, convert=True),
        grid=(1, nf_head),
        in_specs=[
            head_row,
            pl.BlockSpec((None, None, D_MODEL, tf_head), lambda m, f: (i, j, 0, f)),
            pl.BlockSpec((None, None, D_MODEL, tf_head), lambda m, f: (i, j, 0, f)),
            pl.BlockSpec((None, None, tf_head, D_MODEL), lambda m, f: (i, j, f, 0)),
            ln_spec, ln_spec,
        ],
        out_specs=[head_row] * n_h + [
            pl.BlockSpec((None, D_MODEL, tf_head), lambda m, f: (f // per, 0, f % per)),
            pl.BlockSpec((None, D_MODEL, tf_head), lambda m, f: (f // per, 0, f % per)),
            pl.BlockSpec((tf_head, D_MODEL), lambda m, f: (f, 0)),
        ],
        out_shape=h_shapes + [
            jax.ShapeDtypeStruct((nf, D_MODEL, TF), BF16),
            jax.ShapeDtypeStruct((nf, D_MODEL, TF), BF16),
            jax.ShapeDtypeStruct((D_FF, D_MODEL), BF16),
        ],
        scratch_shapes=scratch,
        compiler_params=_cparams(("parallel", "arbitrary"), VMEM_LIMIT_MAX_BYTES),
        name="ffn_ln_head",
    )(h, w1, w3, w2, lng, lnb)
    h_parts, (w1b, w3b, w2b) = head[:n_h], head[n_h:]

    row = pl.BlockSpec((tm, D_MODEL), lambda m, f: (m + 1, 0))
    out = pl.pallas_call(
        functools.partial(_ffn_kernel, nf=nf, n_alias=n_h, emit_bf16=emit_bf16, convert=False),
        grid=(T_ALL // tm - 1, nf),
        in_specs=[
            row,
            pl.BlockSpec((None, D_MODEL, TF), lambda m, f: (f, 0, 0)),
            pl.BlockSpec((None, D_MODEL, TF), lambda m, f: (f, 0, 0)),
            pl.BlockSpec((TF, D_MODEL), lambda m, f: (f, 0)),
            ln_spec, ln_spec,
        ] + [pl.BlockSpec(memory_space=pl.ANY)] * n_h,
        out_specs=[row] * n_h,
        out_shape=h_shapes,
        scratch_shapes=scratch,
        input_output_aliases={6 + a: a for a in range(n_h)},
        compiler_params=_cparams(("parallel", "arbitrary"), vmem),
        name="ffn_ln",
    )(h, w1b, w3b, w2b, lng, lnb, *h_parts)
    return tuple(out) if emit_bf16 else out[0]


def _mm_kernel(x_ref, wt_ref, o_ref, wb_ref):
    @pl.when(pl.program_id(1) == 0)
    def _():
        wb_ref[...] = wt_ref[...].astype(BF16)

    o_ref[...] = lax.dot_general(x_ref[...], wb_ref[...], (((1,), (1,)), ((), ())),
                                 preferred_element_type=F32)


def _mm(xb, wt, n_out, tn):
    n, k = wt.shape
    assert n_out % tn == 0 and n_out <= n
    return pl.pallas_call(
        _mm_kernel,
        grid=(n_out // tn, T_ALL // TM_IN),
        in_specs=[
            pl.BlockSpec((TM_IN, k), lambda j, m: (m, 0)),
            pl.BlockSpec((tn, k), lambda j, m: (j, 0)),
        ],
        out_specs=pl.BlockSpec((TM_IN, tn), lambda j, m: (m, j)),
        out_shape=jax.ShapeDtypeStruct((T_ALL, n_out), F32),
        scratch_shapes=[pltpu.VMEM((tn, k), BF16)],
        compiler_params=_cparams(("parallel", "arbitrary"), VMEM_LIMIT_WIDE_BYTES),
        name="in_proj",
    )(xb, wt)


TM_SC = TM_WIDE
SC_TAIL0 = T_ALL - TM_SC
SC_SAMP_OFF = T_MAIN - SC_TAIL0
assert SC_SAMP_OFF % DEC_SEQ == 0 and T_PROMPT - (SC_WIDTH - 1) >= SC_TAIL0


def _sc_mix_kernel(x_ref, wb_ref, wc_ref, wh_ref, cw_ref, o1_ref, o2_ref, v_ref, ut_ref,
                   wbb_ref, wcb_ref, whb_ref, ubuf_ref, *, nm):
    m = pl.program_id(1)
    tn = ubuf_ref.shape[1]

    @pl.when(m == 0)
    def _():
        wbb_ref[...] = wb_ref[...].astype(BF16)
        wcb_ref[...] = wc_ref[...].astype(BF16)
        whb_ref[...] = wh_ref[...].astype(BF16)
        ubuf_ref[0:CONV_HEAD, :] = jnp.zeros((CONV_HEAD, tn), F32)

    cw = cw_ref[...]
    ov_refs = (o1_ref, o2_ref)
    rs = FFN_SLAB
    n_slabs = TM_SC // rs

    def slab(s, last_tile):
        r0 = s * rs
        xb = x_ref[r0:r0 + rs, :]
        bg = jnp.dot(xb, wbb_ref[...], preferred_element_type=F32)
        c = jnp.dot(xb, wcb_ref[...], preferred_element_type=F32)
        hh = jnp.dot(xb, whb_ref[...], preferred_element_type=F32)
        u = c * hh
        ubuf_ref[CONV_HEAD + r0:CONV_HEAD + r0 + rs, :] = u
        acc = cw[SC_WIDTH - 1:SC_WIDTH, :] * u
        if last_tile:
            lrow = lax.broadcasted_iota(jnp.int32, (rs, 1), 0) + r0
            spos = jnp.bitwise_and(lrow, DEC_SEQ - 1)
        for d in range(1, SC_WIDTH):
            ud = ubuf_ref[CONV_HEAD + r0 - d:CONV_HEAD + r0 - d + rs, :]
            if last_tile:
                first = jnp.logical_and(lrow >= SC_SAMP_OFF, spos < d)
                ud = jnp.where(first, ov_refs[d - 1][r0:r0 + rs, :], ud)
            acc = acc + cw[SC_WIDTH - 1 - d:SC_WIDTH - d, :] * ud
        v_ref[r0:r0 + rs, :] = (bg * acc).astype(BF16)
        if last_tile:
            ut_ref[r0:r0 + rs, :] = u

    @pl.when(m < nm - 1)
    def _():
        for s in range(n_slabs):
            slab(s, False)

    @pl.when(m == nm - 1)
    def _():
        for s in range(n_slabs):
            slab(s, True)

    ubuf_ref[0:CONV_HEAD, :] = ubuf_ref[TM_SC:TM_SC + CONV_HEAD, :]


def _sc_mix(xb, w_in, j, w_conv, conv_prev):
    tn = 512
    nb = D_MODEL // tn
    nm = T_ALL // TM_SC
    ovs = [jnp.pad(o, ((SC_SAMP_OFF, 0), (0, 0))) for o in _conv_overrides(conv_prev, SC_WIDTH)]
    tail = pl.BlockSpec((TM_SC, tn), lambda n, m: (0, n))
    wscratch = pltpu.VMEM((D_MODEL, tn), BF16)
    return pl.pallas_call(
        functools.partial(_sc_mix_kernel, nm=nm),
        grid=(nb, nm),
        in_specs=[
            pl.BlockSpec((TM_SC, D_MODEL), lambda n, m: (m, 0)),
            pl.BlockSpec((None, D_MODEL, tn), lambda n, m: (j, 0, n)),
            pl.BlockSpec((None, D_MODEL, tn), lambda n, m: (j, 0, n + nb)),
            pl.BlockSpec((None, D_MODEL, tn), lambda n, m: (j, 0, n + 2 * nb)),
            pl.BlockSpec((SC_WIDTH, tn), lambda n, m: (0, n)),
            tail, tail,
        ],
        out_specs=[pl.BlockSpec((TM_SC, tn), lambda n, m: (m, n)), tail],
        out_shape=[jax.ShapeDtypeStruct((T_ALL, D_MODEL), BF16),
                   jax.ShapeDtypeStruct((TM_SC, D_MODEL), F32)],
        scratch_shapes=[wscratch, wscratch, wscratch, pltpu.VMEM((CONV_HEAD + TM_SC, tn), F32)],
        compiler_params=_cparams(("parallel", "arbitrary"), VMEM_LIMIT_WIDE_BYTES),
        name="sc_mix",
    )(xb, w_in, w_in, w_in, w_conv.astype(F32), *ovs)


def _conv_overrides(prev, taps):
    c = prev.shape[-1]
    out = []
    for d in range(1, taps):
        o = jnp.zeros((DEC_BATCH, DEC_SEQ, c), F32)
        for p in range(d):
            o = o.at[:, p].set(prev[:, taps - 1 + p - d])
        out.append(o.reshape(N_SAMP, c))
    return out


def _proj_ln_kernel(y_ref, w_ref, h_ref, g_ref, b_ref, o_ref, *, nk):
    rs = FFN_SLAB

    def slab_dot(rows):
        return jnp.dot(y_ref[rows, :], w_ref[...], preferred_element_type=F32)

    def finish(rows, acc):
        v = ALPHA * h_ref[rows, :] + acc
        o_ref[rows, :] = _layer_norm(v, g_ref[...], b_ref[...])

    slabs = [pl.ds(s * rs, rs) for s in range(y_ref.shape[0] // rs)]
    if nk == 1:
        for rows in slabs:
            finish(rows, slab_dot(rows))
        return

    k = pl.program_id(1)

    @pl.when(k == 0)
    def _():
        for rows in slabs:
            o_ref[rows, :] = slab_dot(rows)

    @pl.when(k == 1)
    def _():
        for rows in slabs:
            finish(rows, o_ref[rows, :] + slab_dot(rows))


def _proj_ln(y, w, j, h, lng, lnb, i, k):
    kdim = w.shape[1]
    tk = 2048
    nk = kdim // tk
    assert nk in (1, 2)
    return pl.pallas_call(
        functools.partial(_proj_ln_kernel, nk=nk),
        grid=(T_ALL // TM, nk),
        in_specs=[
            pl.BlockSpec((TM, tk), lambda m, kk: (m, kk)),
            pl.BlockSpec((None, tk, D_MODEL), lambda m, kk: (j, kk, 0)),
            pl.BlockSpec((TM, D_MODEL), lambda m, kk: (m, 0)),
            pl.BlockSpec((None, None, 1, D_MODEL), lambda m, kk: (i, k, 0, 0)),
            pl.BlockSpec((None, None, 1, D_MODEL), lambda m, kk: (i, k, 0, 0)),
        ],
        out_specs=pl.BlockSpec((TM, D_MODEL), lambda m, kk: (m, 0)),
        out_shape=jax.ShapeDtypeStruct((T_ALL, D_MODEL), F32),
        compiler_params=_cparams(("parallel", "arbitrary")),
        name="out_proj_ln",
    )(y, w, h, lng, lnb)


_NT = (((1,), (1,)), ((), ()))
_TN = (((0,), (0,)), ((), ()))
SPLIT_K = 2 * LANES


def _bf16_pieces(v):
    hi = v.astype(BF16)
    r1 = v - hi.astype(F32)
    mid = r1.astype(BF16)
    lo = (r1 - mid.astype(F32)).astype(BF16)
    return hi, mid, lo


def _split_pack(v, lane_lo):
    hi, mid, lo = _bf16_pieces(v)
    a = jnp.where(lane_lo, hi.astype(F32), pltpu.roll(mid.astype(F32), HEADS, axis=1))
    b = jnp.where(lane_lo, lo.astype(F32), 0.0)
    return jnp.concatenate([a, b], axis=1).astype(BF16)


def _split_pack2(v, lane_lo):
    hi = v.astype(BF16)
    mid = (v - hi.astype(F32)).astype(BF16)
    return jnp.where(lane_lo, hi.astype(F32), pltpu.roll(mid.astype(F32), HEADS, axis=1)).astype(BF16)


def _masked_cumsum(mask_bf, da):
    hi, mid, lo = _bf16_pieces(da)
    p = jnp.dot(mask_bf, jnp.concatenate([hi, mid, lo], axis=1), preferred_element_type=F32)
    return p[:, :LANES] + p[:, LANES:2 * LANES] + p[:, 2 * LANES:]


def _conv_silu(buf_ref, base, n, cw, cbias):
    taps = SSD_CONV_WIDTH
    acc = cw[taps - 1:taps, :] * buf_ref[base:base + n, :]
    for d in range(1, taps):
        acc = acc + cw[taps - 1 - d:taps - d, :] * buf_ref[base - d:base - d + n, :]
    return _silu(acc + cbias)


def _ssd_prep_kernel(x_ref, wdt_ref, dtb_ref, al_ref, pk_acs_ref, pk3_ref, acst_ref, tott_ref):
    li, si = _chunk_iotas()
    lane_lo = si < HEADS
    same_seq = (li // DEC_SEQ) == (si // DEC_SEQ)
    a_row = -jnp.exp(al_ref[...])
    wdt = wdt_ref[...].astype(BF16)
    wdt = jnp.concatenate([wdt, jnp.zeros_like(wdt)], axis=0)
    dt_raw = lax.dot_general(x_ref[...], wdt, _NT, preferred_element_type=F32)
    for k in range(PREP_NB):
        c = pl.program_id(0) * PREP_NB + k
        rows = pl.ds(k * CHUNK, CHUNK)
        same = jnp.logical_or(c < N_MAIN_CHUNKS, same_seq)
        causal = jnp.logical_and(si <= li, same)
        row = lax.broadcasted_iota(jnp.int32, (CHUNK, 1), 0) + c * CHUNK
        valid = jnp.logical_or(row < T_PROMPT, row >= T_MAIN)
        dt = jnp.where(jnp.logical_and(valid, lane_lo),
                       _softplus(dt_raw[k * CHUNK:(k + 1) * CHUNK, :] + dtb_ref[...]), 0.0)
        da = dt * a_row
        acs = _masked_cumsum(causal.astype(F32).astype(BF16), da)
        tot = _masked_cumsum(same.astype(F32).astype(BF16), da)
        pk_acs_ref[rows, :] = _split_pack(acs, lane_lo)
        pk3_ref[rows, :] = jnp.concatenate([_split_pack2(dt, lane_lo), _split_pack2(jnp.exp(tot - acs), lane_lo),
                                            _split_pack2(jnp.exp(acs), lane_lo)], axis=1)
        acst_ref[k] = acs.T
        tott_ref[k] = tot.T


def _ssd_prep(xb, wt, dt_bias, a_log):
    n_chunks = T_ALL // CHUNK
    rows = PREP_NB * CHUNK
    tr = pl.BlockSpec((PREP_NB, LANES, CHUNK), lambda c: (c, 0, 0))
    return pl.pallas_call(
        _ssd_prep_kernel,
        grid=(n_chunks // PREP_NB,),
        in_specs=[
            pl.BlockSpec((rows, D_MODEL), lambda c: (c, 0)),
            pl.BlockSpec((HEADS, D_MODEL), lambda c: (DT_COL // HEADS, 0)),
            pl.BlockSpec((1, LANES), lambda c: (0, 0)),
            pl.BlockSpec((1, LANES), lambda c: (0, 0)),
        ],
        out_specs=[
            pl.BlockSpec((rows, SPLIT_K), lambda c: (c, 0)),
            pl.BlockSpec((rows, 3 * LANES), lambda c: (c, 0)),
            tr, tr,
        ],
        out_shape=[
            jax.ShapeDtypeStruct((T_ALL, SPLIT_K), BF16),
            jax.ShapeDtypeStruct((T_ALL, 3 * LANES), BF16),
            jax.ShapeDtypeStruct((n_chunks, LANES, CHUNK), F32),
            jax.ShapeDtypeStruct((n_chunks, LANES, CHUNK), F32),
        ],
        compiler_params=_cparams(("parallel",)),
        name="ssd_prep",
    )(xb, wt, dt_bias, a_log)


def _intra_chunk(xbc, pk_acs, pk3, arow, causal, lane_lo, r_ref, e_ref):
    xs = xbc[:, :GROUP_W]
    bmb = xbc[:, GROUP_W:GROUP_W + STATE].astype(BF16)
    cmb = xbc[:, GROUP_W + STATE:].astype(BF16)
    colmat = jnp.dot(pk_acs, r_ref[...], preferred_element_type=F32)
    packed = jnp.concatenate([pk3[:, :LANES], pk3[:, LANES:2 * LANES], pk3[:, 2 * LANES:]], axis=0)
    ex = jnp.dot(packed, e_ref[...], preferred_element_type=F32)
    dtx, tex, eax = ex[:CHUNK], ex[CHUNK:2 * CHUNK], ex[2 * CHUNK:]
    xdt = xs * dtx
    cb = lax.dot_general(cmb, bmb, _NT, preferred_element_type=F32)
    ys = []
    for q in range(HEADS_PER_GROUP // 2):
        scs = []
        for r in (2 * q, 2 * q + 1):
            seg = colmat[:, r * CHUNK:(r + 1) * CHUNK] - arow[r:r + 1, :]
            scs.append((cb * jnp.exp(jnp.where(causal, seg, -jnp.inf))).astype(BF16))
        xp = xdt[:, q * LANES:(q + 1) * LANES]
        rhs = jnp.concatenate([jnp.where(lane_lo, xp, 0.0).astype(BF16),
                               jnp.where(lane_lo, 0.0, xp).astype(BF16)], axis=0)
        ys.append(jnp.dot(jnp.concatenate(scs, axis=1), rhs, preferred_element_type=F32))
    y_intra = jnp.concatenate(ys, axis=1)
    xw = (xdt * tex).astype(BF16)
    return xs, bmb, cmb, y_intra, eax, xw


def _gate_norm(y, z, nw):
    y = y * _silu(z)
    ms = jnp.mean(y * y, axis=-1, keepdims=True)
    return (y * lax.rsqrt(ms + RMS_EPS) * nw).astype(BF16)


def _chunk_iotas():
    li = lax.broadcasted_iota(jnp.int32, (CHUNK, CHUNK), 0)
    si = lax.broadcasted_iota(jnp.int32, (CHUNK, CHUNK), 1)
    return li, si


def _ssd_main_kernel(z_ref, xr_ref, br_ref, cr_ref, pk_acs_ref, pk3_ref, acst_ref, tott_ref, cw_ref, cbias_ref,
                     dx_ref, nw_ref, r_ref, e_ref, y_ref, sfin_ref, buf_ref, st_ref):
    del tott_ref
    g = pl.program_id(0)
    rb = pl.program_id(1)

    @pl.when(rb == 0)
    def _():
        st_ref[...] = jnp.zeros_like(st_ref)
        buf_ref[0:CONV_HEAD, :] = jnp.zeros((CONV_HEAD, XBC_W), F32)

    @pl.when(rb > 0)
    def _():
        buf_ref[0:CONV_HEAD, :] = buf_ref[SSD_RB:SSD_RB + CONV_HEAD, :]

    buf_ref[CONV_HEAD:, 0:GROUP_W] = xr_ref[...]
    buf_ref[CONV_HEAD:, GROUP_W:GROUP_W + STATE] = br_ref[...]
    buf_ref[CONV_HEAD:, GROUP_W + STATE:] = cr_ref[...]

    li, si = _chunk_iotas()
    causal = si <= li
    lane_lo = si < HEADS
    g8 = pl.multiple_of(g * HEADS_PER_GROUP, SUBLANES)
    cw = cw_ref[...]
    cbias = cbias_ref[...]

    for k in range(SSD_NB):
        rows = pl.ds(k * CHUNK, CHUNK)
        xbc = _conv_silu(buf_ref, CONV_HEAD + k * CHUNK, CHUNK, cw, cbias)
        arow = acst_ref[k, pl.ds(g8, HEADS_PER_GROUP), :]
        xs, bmb, cmb, y_intra, eax, xw = _intra_chunk(xbc, pk_acs_ref[rows, :], pk3_ref[rows, :], arow,
                                                      causal, lane_lo, r_ref, e_ref)
        st = st_ref[...]
        y = y_intra + jnp.dot(cmb, st.astype(BF16), preferred_element_type=F32) * eax + dx_ref[...] * xs
        y_ref[rows, :] = _gate_norm(y, z_ref[rows, :], nw_ref[...])
        st_ref[...] = st * eax[CHUNK - 1:CHUNK, :] + lax.dot_general(bmb, xw, _TN, preferred_element_type=F32)

    @pl.when(rb == T_MAIN // SSD_RB - 1)
    def _():
        sfin_ref[...] = st_ref[...]


def _ssd_samp_kernel(z_ref, xr_ref, br_ref, cr_ref, pk_acs_ref, pk3_ref, acst_ref, tott_ref, cw_ref, cbias_ref,
                     dx_ref, nw_ref, r_ref, e_ref, prev_ref, s0_ref, yprev_ref, y_ref, s1_ref,
                     buf_ref, xbc_ref, yint_ref):
    del yprev_ref
    g = pl.program_id(0)
    slot = CONV_HEAD + DEC_SEQ
    cw = cw_ref[...]
    cbias = cbias_ref[...]
    for q in range(SEQ_PER_CHUNK):
        r0 = q * DEC_SEQ
        top = q * slot + CONV_HEAD
        buf_ref[top - (SSD_CONV_WIDTH - 1):top, :] = prev_ref[q]
        buf_ref[top:top + DEC_SEQ, 0:GROUP_W] = xr_ref[r0:r0 + DEC_SEQ, :]
        buf_ref[top:top + DEC_SEQ, GROUP_W:GROUP_W + STATE] = br_ref[r0:r0 + DEC_SEQ, :]
        buf_ref[top:top + DEC_SEQ, GROUP_W + STATE:] = cr_ref[r0:r0 + DEC_SEQ, :]
    for q in range(SEQ_PER_CHUNK):
        r0 = q * DEC_SEQ
        xbc_ref[r0:r0 + DEC_SEQ, :] = _conv_silu(buf_ref, q * slot + CONV_HEAD, DEC_SEQ, cw, cbias)

    li, si = _chunk_iotas()
    same = (li // DEC_SEQ) == (si // DEC_SEQ)
    causal = jnp.logical_and(si <= li, same)
    lane_lo = si < HEADS
    g8 = pl.multiple_of(g * HEADS_PER_GROUP, SUBLANES)

    arow = acst_ref[0, pl.ds(g8, HEADS_PER_GROUP), :]
    trow = tott_ref[0, pl.ds(g8, HEADS_PER_GROUP), :]
    xs, bmb, cmb, y_intra, eax, xw = _intra_chunk(xbc_ref[...], pk_acs_ref[...], pk3_ref[...], arow,
                                                  causal, lane_lo, r_ref, e_ref)
    for q in range(SEQ_PER_CHUNK):
        r0 = q * DEC_SEQ
        yint_ref[r0:r0 + DEC_SEQ, :] = lax.dot_general(cmb[r0:r0 + DEC_SEQ, :], s0_ref[q].astype(BF16), _NT,
                                                       preferred_element_type=F32)
    y = y_intra + yint_ref[...] * eax + dx_ref[...] * xs
    y_ref[...] = _gate_norm(y, z_ref[...], nw_ref[...])
    for q in range(SEQ_PER_CHUNK):
        r0 = q * DEC_SEQ
        upd = lax.dot_general(xw[r0:r0 + DEC_SEQ, :], bmb[r0:r0 + DEC_SEQ, :], _TN, preferred_element_type=F32)
        for r in range(HEADS_PER_GROUP):
            lo, hi = r * HEADDIM, (r + 1) * HEADDIM
            s1_ref[q, lo:hi, :] = s0_ref[q, lo:hi, :] * jnp.exp(trow[r:r + 1, r0:r0 + 1]) + upd[lo:hi, :]


def _ssd_specs(rows, row0):
    xcol = D_INNER // GROUP_W
    bcol = (2 * D_INNER) // STATE
    return [
        pl.BlockSpec((rows, GROUP_W), lambda g, c: (row0 + c, g)),
        pl.BlockSpec((rows, GROUP_W), lambda g, c: (row0 + c, xcol + g)),
        pl.BlockSpec((rows, STATE), lambda g, c: (row0 + c, bcol + g)),
        pl.BlockSpec((rows, STATE), lambda g, c: (row0 + c, bcol + GROUPS + g)),
        pl.BlockSpec((rows, SPLIT_K), lambda g, c: (row0 + c, 0)),
        pl.BlockSpec((rows, 3 * LANES), lambda g, c: (row0 + c, 0)),
        pl.BlockSpec((rows // CHUNK, LANES, CHUNK), lambda g, c: (row0 + c, 0, 0)),
        pl.BlockSpec((rows // CHUNK, LANES, CHUNK), lambda g, c: (row0 + c, 0, 0)),
        pl.BlockSpec((None, SSD_CONV_WIDTH, XBC_W), lambda g, c: (g, 0, 0)),
        pl.BlockSpec((None, 1, XBC_W), lambda g, c: (g, 0, 0)),
        pl.BlockSpec((1, GROUP_W), lambda g, c: (0, g)),
        pl.BlockSpec((1, GROUP_W), lambda g, c: (0, g)),
        pl.BlockSpec((None, SPLIT_K, HEADS_PER_GROUP * CHUNK), lambda g, c: (g, 0, 0)),
        pl.BlockSpec((None, LANES, GROUP_W), lambda g, c: (g, 0, 0)),
    ]


def _spread_matrices():
    k = np.arange(SPLIT_K)
    head = k % HEADS
    used = (k // HEADS) < 3
    g = np.arange(GROUPS)[:, None, None]
    rj = np.arange(HEADS_PER_GROUP * CHUNK)[None, None, :] // CHUNK
    ej = np.arange(GROUP_W)[None, None, :] // HEADDIM
    hk = head[None, :, None]
    uk = used[None, :, None]
    r = (uk & (hk == g * HEADS_PER_GROUP + rj)).astype(np.float32)
    e = (hk == g * HEADS_PER_GROUP + ej)[:, :LANES].astype(np.float32)
    return jnp.asarray(r, BF16), jnp.asarray(e, BF16)


def _per_group(a):
    lead = a.shape[:-1]
    x = a[..., :D_INNER].reshape(*lead, GROUPS, GROUP_W)
    b = a[..., D_INNER:D_INNER + GROUPS * STATE].reshape(*lead, GROUPS, STATE)
    c = a[..., D_INNER + GROUPS * STATE:].reshape(*lead, GROUPS, STATE)
    return jnp.moveaxis(jnp.concatenate([x, b, c], axis=-1), -2, 0)


def _ssd(zxd, xb, wt, conv_w, conv_b, dt_bias, a_log, d_skip, norm_w, conv_prev, state0):
    r_mat, e_mat = _spread_matrices()
    pad = ((0, 0), (0, LANES - HEADS))
    prep = _ssd_prep(xb, wt, jnp.pad(dt_bias.reshape(1, HEADS).astype(F32), pad),
                     jnp.pad(a_log.reshape(1, HEADS).astype(F32), pad))
    params = (_per_group(conv_w), _per_group(conv_b.reshape(1, CONV_DIM)),
              jnp.repeat(d_skip.astype(F32), HEADDIM).reshape(1, D_INNER),
              norm_w.reshape(1, D_INNER).astype(F32), r_mat, e_mat)
    common = (zxd,) * 4 + tuple(prep) + params
    y, s_fin = pl.pallas_call(
        _ssd_main_kernel,
        grid=(GROUPS, T_MAIN // SSD_RB),
        in_specs=_ssd_specs(SSD_RB, 0),
        out_specs=[
            pl.BlockSpec((SSD_RB, GROUP_W), lambda g, c: (c, g)),
            pl.BlockSpec((None, STATE, GROUP_W), lambda g, c: (g, 0, 0)),
        ],
        out_shape=[
            jax.ShapeDtypeStruct((T_ALL, D_INNER), BF16),
            jax.ShapeDtypeStruct((GROUPS, STATE, GROUP_W), F32),
        ],
        scratch_shapes=[
            pltpu.VMEM((CONV_HEAD + SSD_RB, XBC_W), F32),
            pltpu.VMEM((STATE, GROUP_W), F32),
        ],
        compiler_params=_cparams(("parallel", "arbitrary")),
        name="ssd_prompt",
    )(*common)

    n_in = len(common)
    y, s_new = pl.pallas_call(
        _ssd_samp_kernel,
        grid=(GROUPS, N_SAMP_CHUNKS),
        in_specs=_ssd_specs(CHUNK, N_MAIN_CHUNKS) + [
            pl.BlockSpec((None, SEQ_PER_CHUNK, SSD_CONV_WIDTH - 1, XBC_W), lambda g, c: (g, c, 0, 0)),
            pl.BlockSpec((SEQ_PER_CHUNK, None, GROUP_W, STATE), lambda g, c: (c, g, 0, 0)),
            pl.BlockSpec(memory_space=pl.ANY),
        ],
        out_specs=[
            pl.BlockSpec((CHUNK, GROUP_W), lambda g, c: (N_MAIN_CHUNKS + c, g)),
            pl.BlockSpec((SEQ_PER_CHUNK, None, GROUP_W, STATE), lambda g, c: (c, g, 0, 0)),
        ],
        out_shape=[
            jax.ShapeDtypeStruct((T_ALL, D_INNER), BF16),
            jax.ShapeDtypeStruct((DEC_BATCH, GROUPS, GROUP_W, STATE), F32),
        ],
        scratch_shapes=[
            pltpu.VMEM((SEQ_PER_CHUNK * (CONV_HEAD + DEC_SEQ), XBC_W), F32),
            pltpu.VMEM((CHUNK, XBC_W), F32),
            pltpu.VMEM((CHUNK, GROUP_W), F32),
        ],
        input_output_aliases={n_in + 2: 0},
        compiler_params=_cparams(("parallel", "arbitrary")),
        name="ssd_sample",
    )(*common, _per_group(conv_prev), state0, y)
    return y, s_fin, s_new


def _tail_rows(a, n):
    c = a.shape[-1]
    p = a[T_PROMPT - n:T_PROMPT].reshape(1, 1, n, c)
    s = a[T_MAIN:].reshape(DEC_BATCH, DEC_SEQ, c)[:, DEC_SEQ - n:].reshape(1, DEC_BATCH, n, c)
    return p, s


def kernel(x_prompt, x_sample, cache_sc_conv, state_ssd_conv, state_ssd, meta_tokens, ln_g, ln_b,
           ffn_w1, ffn_w3, ffn_w2, sc_w_in, sc_w_conv, sc_w_out,
           ssd_w_in, ssd_w_conv, ssd_b_conv, ssd_dt_bias, ssd_a_log, ssd_d, ssd_norm_w, ssd_w_out):
    w1, w3, w2 = ffn_w1, ffn_w3, ffn_w2
    sc_out_w = sc_w_out.astype(BF16)
    ssd_out_w = ssd_w_out.astype(BF16)
    lng = ln_g.reshape(DEPTH, 3, 1, D_MODEL)
    lnb = ln_b.reshape(DEPTH, 3, 1, D_MODEL)

    h = jnp.concatenate([meta_tokens.astype(F32), x_prompt[0], jnp.zeros((N_PAD, D_MODEL), F32),
                         x_sample.reshape(N_SAMP, D_MODEL)], axis=0)

    h, hb = _ffn(h, w1, w3, w2, lng, lnb, 0, 0, 0, emit_bf16=True)
    v, u_tail = _sc_mix(hb, sc_w_in, 0, sc_w_conv[0], cache_sc_conv[0])
    keep = SC_WIDTH - 1
    new_sc_p = u_tail[T_PROMPT - keep - SC_TAIL0:T_PROMPT - SC_TAIL0].reshape(1, 1, keep, D_MODEL)
    new_sc_s = u_tail[SC_SAMP_OFF:].reshape(DEC_BATCH, DEC_SEQ, D_MODEL)[:, DEC_SEQ - keep:].reshape(
        1, DEC_BATCH, keep, D_MODEL)
    h = _proj_ln(v, sc_out_w, 0, h, lng, lnb, 0, 1)
    h = _ffn(h, w1, w3, w2, lng, lnb, 0, 1, 2)

    h, hb = _ffn(h, w1, w3, w2, lng, lnb, 1, 0, 0, emit_bf16=True)
    ssd_wt = jnp.swapaxes(ssd_w_in[0], 0, 1)
    zxd = _mm(hb, ssd_wt, ZXD_W, ZXD_TN)
    new_conv_p, new_conv_s = _tail_rows(zxd[:, D_INNER:DT_COL], SSD_CONV_WIDTH - 1)
    state0 = state_ssd[0].reshape(DEC_BATCH, GROUPS, GROUP_W, STATE)
    y, s_fin, s_new = _ssd(zxd, hb, ssd_wt, ssd_w_conv[0], ssd_b_conv[0], ssd_dt_bias[0], ssd_a_log[0],
                           ssd_d[0], ssd_norm_w[0], state_ssd_conv[0], state0)
    h = _proj_ln(y, ssd_out_w, 0, h, lng, lnb, 1, 1)
    h = _ffn(h, w1, w3, w2, lng, lnb, 1, 1, 2)

    y_prompt = h[N_META:T_PROMPT].reshape(1, SEQ, D_MODEL)
    y_sample = h[T_MAIN:].reshape(DEC_BATCH, DEC_SEQ, D_MODEL)
    new_ssd_p = jnp.swapaxes(s_fin, 1, 2).reshape(1, 1, HEADS, HEADDIM, STATE).astype(state_ssd.dtype)
    new_ssd_s = s_new.reshape(1, DEC_BATCH, HEADS, HEADDIM, STATE).astype(state_ssd.dtype)
    return (y_prompt, y_sample, new_sc_p, new_sc_s, new_conv_p, new_conv_s, new_ssd_p, new_ssd_s)
```
